```python
import jax
import jax.numpy as jnp
from jax import lax
import numpy as np

D_MODEL = 1024
BATCH = 2
SEQ = 8192
DEPTH = 1

POOL_WINDOWS = (2, 4, 8, 16)
POOL_GROUPS = 4
POOL_GROUP_W = 128
POOL_W = POOL_GROUPS * POOL_GROUP_W
GLA_HEADS = 4
GLA_DK = 64
GLA_DV = 128
GLA_KEY = GLA_HEADS * GLA_DK
GLA_VAL = GLA_HEADS * GLA_DV
GATE_RANK = 16
GATE_NORMALIZER = 16.0
CHUNK = 64
MIX_W = POOL_W + GLA_VAL
IN_COLS = POOL_W + 2 * GLA_KEY + GLA_VAL + GATE_RANK + GLA_VAL
IN_SPLITS = (POOL_W,
             POOL_W + GLA_KEY,
             POOL_W + 2 * GLA_KEY,
             POOL_W + 2 * GLA_KEY + GLA_VAL,
             POOL_W + 2 * GLA_KEY + GLA_VAL + GATE_RANK)
MEM_LEN = 256
XATTN_HEADS = 4
XATTN_HD = D_MODEL // XATTN_HEADS
N_EXPERTS = 32
TOP_K = 4
D_FF = D_MODEL
SWIGLU_LIMIT = 7.0
SWIGLU_ALPHA = 1.702
MOE_BLOCK = 128
EPS = 1e-6

kernel_name = 'hybrid_pool_gla_xattn_moe_block'


def rms_norm(x, g):
    xf = x.astype(jnp.float32)
    y = xf * lax.rsqrt(jnp.mean(xf * xf, axis=-1, keepdims=True) + EPS)
    return (y * g.astype(jnp.float32)).astype(x.dtype)


def pool_mixer(u, w_pool, pool_scale):
    b, s, _ = u.shape
    uf = u.astype(jnp.float32).reshape(b, s, POOL_GROUPS, POOL_GROUP_W)
    cs = jnp.cumsum(uf, axis=1)
    pos = jnp.arange(s) + 1
    outs = []
    for gi, w in enumerate(POOL_WINDOWS):
        c = cs[:, :, gi]
        lag = jnp.pad(c, ((0, 0), (w, 0), (0, 0)))[:, :s]
        cnt = jnp.minimum(pos, w).astype(jnp.float32)[None, :, None]
        outs.append((c - lag) / cnt - uf[:, :, gi])
    p = jnp.stack(outs, axis=2).astype(u.dtype)
    y = jnp.einsum('bsgc,gcd->bsgd', p, w_pool).reshape(b, s, POOL_W)
    return y * pool_scale


def gla_mixer(q, k, v, g_lr, r, w_gate_up, b_gate_up, gla_norm_g):
    b, s, _ = q.shape
    n = s // CHUNK

    def chunked(a, d):
        return a.reshape(b, n, CHUNK, GLA_HEADS, d).transpose(0, 3, 1, 2, 4).astype(jnp.float32)

    g = jax.nn.log_sigmoid((g_lr @ w_gate_up + b_gate_up).astype(jnp.float32)) / GATE_NORMALIZER
    qc = chunked(q, GLA_DK) * (GLA_DK ** -0.5)
    kc = chunked(k, GLA_DK)
    vc = chunked(v, GLA_DV)
    gc = jnp.cumsum(chunked(g, GLA_DK), axis=3)
    qe = qc * jnp.exp(gc)
    ke = kc * jnp.exp(-gc)
    causal = jnp.tril(jnp.ones((CHUNK, CHUNK), dtype=bool))
    att = jnp.where(causal, jnp.einsum('bhnid,bhnjd->bhnij', qe, ke), 0.0)
    o = jnp.einsum('bhnij,bhnje->bhnie', att, vc)
    g_last = gc[:, :, :, -1]
    d_state = jnp.einsum('bhnjd,bhnje->bhnde',
                         kc * jnp.exp(g_last[:, :, :, None] - gc), vc)

    def step(state, inp):
        dec, ds = inp
        return dec[..., None] * state + ds, state

    init = jnp.zeros((b, GLA_HEADS, GLA_DK, GLA_DV), jnp.float32)
    _, s_prev = lax.scan(step, init,
                         (jnp.exp(g_last).transpose(2, 0, 1, 3),
                          d_state.transpose(2, 0, 1, 3, 4)))
    s_prev = s_prev.transpose(1, 2, 0, 3, 4)
    o = o + jnp.einsum('bhnid,bhnde->bhnie', qe, s_prev)
    o = o.transpose(0, 2, 3, 1, 4).reshape(b, s, GLA_HEADS, GLA_DV)
    o = o * lax.rsqrt(jnp.mean(o * o, axis=-1, keepdims=True) + EPS) * gla_norm_g.astype(jnp.float32)
    rf = jax.nn.silu(r.astype(jnp.float32)).reshape(b, s, GLA_HEADS, GLA_DV)
    return (o * rf).reshape(b, s, GLA_VAL).astype(q.dtype)


def memory_xattn(h, hm, w_xq, w_xk, w_xv, w_xo):
    b, s, _ = h.shape
    m = hm.shape[1]
    q = (h @ w_xq).reshape(b, s, XATTN_HEADS, XATTN_HD)
    k = (hm @ w_xk).reshape(b, m, XATTN_HEADS, XATTN_HD)
    v = (hm @ w_xv).reshape(b, m, XATTN_HEADS, XATTN_HD)
    sc = jnp.einsum('bshd,bmhd->bhsm', q, k).astype(jnp.float32) * (XATTN_HD ** -0.5)
    p = jax.nn.softmax(sc, axis=-1).astype(v.dtype)
    o = jnp.einsum('bhsm,bmhd->bshd', p, v).reshape(b, s, D_MODEL)
    return o @ w_xo


def clamped_swiglu(gu):
    gate = jnp.minimum(gu[..., :D_FF], SWIGLU_LIMIT)
    up = jnp.clip(gu[..., D_FF:], -SWIGLU_LIMIT, SWIGLU_LIMIT)
    return gate * jax.nn.sigmoid(SWIGLU_ALPHA * gate) * (up + 1.0)


def moe(h, w_router, b_router, w_gu, b_gu, w_dn, b_dn):
    b, s, d = h.shape
    n_tok = b * s
    t = h.reshape(n_tok, d)
    logits = (t @ w_router + b_router).astype(jnp.float32)
    top_val, top_idx = lax.top_k(logits, TOP_K)
    gates = jax.nn.softmax(top_val, axis=-1)
    n_assign = n_tok * TOP_K
    e_flat = top_idx.reshape(n_assign)
    tok_flat = jnp.repeat(jnp.arange(n_tok, dtype=jnp.int32), TOP_K)
    g_flat = gates.reshape(n_assign)
    order = jnp.argsort(e_flat)
    e_s = e_flat[order]
    tok_s = tok_flat[order]
    g_s = g_flat[order]
    counts = jnp.bincount(e_flat, length=N_EXPERTS)
    starts = jnp.cumsum(counts) - counts
    pcounts = (counts + MOE_BLOCK - 1) // MOE_BLOCK * MOE_BLOCK
    pends = jnp.cumsum(pcounts)
    pstarts = pends - pcounts
    dest = pstarts[e_s] + jnp.arange(n_assign) - starts[e_s]
    n_blocks = -(-n_assign // MOE_BLOCK) + N_EXPERTS
    n_rows = n_blocks * MOE_BLOCK
    tok_pad = jnp.zeros((n_rows,), jnp.int32).at[dest].set(tok_s)
    g_pad = jnp.zeros((n_rows,), jnp.float32).at[dest].set(g_s)
    blk_e = jnp.minimum(jnp.searchsorted(pends, jnp.arange(n_blocks) * MOE_BLOCK, side='right'),
                        N_EXPERTS - 1)

    def expert_block(args):
        tok_b, g_b, e = args
        xb = t[tok_b]
        a = clamped_swiglu((xb @ w_gu[e] + b_gu[e]).astype(jnp.float32)).astype(t.dtype)
        yb = a @ w_dn[e] + b_dn[e]
        return (yb.astype(jnp.float32) * g_b[:, None]).astype(t.dtype)

    y = lax.map(expert_block, (tok_pad.reshape(n_blocks, MOE_BLOCK),
                               g_pad.reshape(n_blocks, MOE_BLOCK), blk_e))
    out = jnp.zeros((n_tok, d), jnp.float32).at[tok_pad].add(
        y.reshape(n_rows, d).astype(jnp.float32))
    return out.reshape(b, s, d).astype(h.dtype)


def setup_inputs(seed: int = 0) -> dict:
    key = jax.random.key(seed)
    ks = jax.random.split(key, 24)
    L = DEPTH

    def nrm(k, shape, scale):
        return jax.random.normal(k, shape, jnp.float32) * scale

    return {
        'x': nrm(ks[0], (BATCH, SEQ, D_MODEL), 1.0),
        'mem': nrm(ks[1], (BATCH, MEM_LEN, D_MODEL), 1.0),
        'norm_mix_g': 1.0 + nrm(ks[2], (L, D_MODEL), 0.05),
        'w_in': nrm(ks[3], (L, D_MODEL, IN_COLS), D_MODEL ** -0.5),
        'w_pool': nrm(ks[4], (L, POOL_GROUPS, POOL_GROUP_W, POOL_GROUP_W), POOL_GROUP_W ** -0.5),
        'pool_scale': 1.0 + nrm(ks[5], (L, POOL_W), 0.1),
        'w_gate_up': nrm(ks[6], (L, GATE_RANK, GLA_KEY), GATE_RANK ** -0.5),
        'b_gate_up': nrm(ks[7], (L, GLA_KEY), 0.1),
        'gla_norm_g': 1.0 + nrm(ks[8], (L, GLA_DV), 0.05),
        'w_out': nrm(ks[9], (L, MIX_W, D_MODEL), MIX_W ** -0.5),
        'norm_xattn_g': 1.0 + nrm(ks[10], (L, D_MODEL), 0.05),
        'norm_mem_g': 1.0 + nrm(ks[11], (L, D_MODEL), 0.05),
        'w_xq': nrm(ks[12], (L, D_MODEL, D_MODEL), D_MODEL ** -0.5),
        'w_xk': nrm(ks[13], (L, D_MODEL, D_MODEL), D_MODEL ** -0.5),
        'w_xv': nrm(ks[14], (L, D_MODEL, D_MODEL), D_MODEL ** -0.5),
        'w_xo': nrm(ks[15], (L, D_MODEL, D_MODEL), D_MODEL ** -0.5),
        'norm_moe_g': 1.0 + nrm(ks[16], (L, D_MODEL), 0.05),
        'w_router': nrm(ks[17], (L, D_MODEL, N_EXPERTS), D_MODEL ** -0.5),
        'b_router': nrm(ks[18], (L, N_EXPERTS), 0.01),
        'w_gu': nrm(ks[19], (L, N_EXPERTS, D_MODEL, 2 * D_FF), D_MODEL ** -0.5),
        'b_gu': nrm(ks[20], (L, N_EXPERTS, 2 * D_FF), 0.01),
        'w_dn': nrm(ks[21], (L, N_EXPERTS, D_FF, D_MODEL), D_FF ** -0.5),
        'b_dn': nrm(ks[22], (L, N_EXPERTS, D_MODEL), 0.01),
        'norm_final_g': 1.0 + nrm(ks[23], (D_MODEL,), 0.05),
    }


def reference(x, mem, norm_mix_g, w_in, w_pool, pool_scale, w_gate_up, b_gate_up,
              gla_norm_g, w_out, norm_xattn_g, norm_mem_g, w_xq, w_xk, w_xv, w_xo,
              norm_moe_g, w_router, b_router, w_gu, b_gu, w_dn, b_dn, norm_final_g):
    for l in range(DEPTH):
        h = rms_norm(x, norm_mix_g[l])
        z = h @ w_in[l]
        u, q, k, v, g_lr, r = jnp.split(z, IN_SPLITS, axis=-1)
        mix = jnp.concatenate(
            [pool_mixer(u, w_pool[l], pool_scale[l]),
             gla_mixer(q, k, v, g_lr, r, w_gate_up[l], b_gate_up[l], gla_norm_g[l])],
            axis=-1)
        x = x + mix @ w_out[l]
        x = x + memory_xattn(rms_norm(x, norm_xattn_g[l]), rms_norm(mem, norm_mem_g[l]),
                             w_xq[l], w_xk[l], w_xv[l], w_xo[l])
        x = x + moe(rms_norm(x, norm_moe_g[l]), w_router[l], b_router[l],
                    w_gu[l], b_gu[l], w_dn[l], b_dn[l])
    return rms_norm(x, norm_final_g)
```

```python
import functools

import jax
import jax.numpy as jnp
from jax import lax
from jax.experimental import pallas as pl
from jax.experimental.pallas import tpu as pltpu

F32 = jnp.float32
BF16 = jnp.bfloat16
I32 = jnp.int32

V7X_LANES = 128
V7X_VMEM_LIMIT_BYTES = 56 * 1024 * 1024

D_MODEL = 1024
POOL_WINDOWS = (2, 4, 8, 16)
POOL_GROUP_W = 128
POOL_W = len(POOL_WINDOWS) * POOL_GROUP_W
GLA_HEADS = 4
GLA_DK = 64
GLA_DV = 128
GLA_KEY = GLA_HEADS * GLA_DK
GLA_VAL = GLA_HEADS * GLA_DV
GATE_RANK = 16
GATE_NORMALIZER = 16.0
CHUNK = 64
XATTN_HEADS = 4
XATTN_HD = D_MODEL // XATTN_HEADS
N_EXPERTS = 32
TOP_K = 4
D_FF = D_MODEL
SWIGLU_LIMIT = 7.0
SWIGLU_ALPHA = 1.702
EPS = 1e-6

_U0 = 0
_Q0 = _U0 + POOL_W
_K0 = _Q0 + GLA_KEY
_V0 = _K0 + GLA_KEY
_R0 = _V0 + GLA_VAL
_G0 = _R0 + GLA_VAL
IN_COLS_PADDED = _G0 + V7X_LANES

SEQ_TILE = 512
POOL_TAIL = 128
PLAN_TILE = 2048
DISPATCH_TILE = 512
EXPERT_ROWS = 256
COMBINE_TILE = 256


def _rms(x, g):
    ms = jnp.mean(x * x, axis=-1, keepdims=True)
    return x * lax.rsqrt(ms + EPS) * g


def _dot(a, b):
    return jnp.dot(a, b, preferred_element_type=F32)


def _dot_nt(a, b):
    return lax.dot_general(a, b, (((1,), (1,)), ((), ())), preferred_element_type=F32)


def _dot_tn(a, b):
    return lax.dot_general(a, b, (((0,), (0,)), ((), ())), preferred_element_type=F32)


def _params(*semantics):
    return pltpu.CompilerParams(dimension_semantics=semantics,
                                vmem_limit_bytes=V7X_VMEM_LIMIT_BYTES)


def _memkv_kernel(mem_ref, g_ref, wk_ref, wv_ref, k_ref, v_ref):
    hm = _rms(mem_ref[0], g_ref[...]).astype(BF16)
    k = _dot(hm, wk_ref[...])
    v = _dot(hm, wv_ref[...])
    k_ref[0] = (k * (XATTN_HD ** -0.5)).astype(BF16)
    v_ref[0] = v.astype(BF16)


def _memkv(mem, g, wk, wv):
    b, m, d = mem.shape
    const = lambda i: (0, 0)
    return pl.pallas_call(
        _memkv_kernel,
        grid=(b,),
        in_specs=[pl.BlockSpec((1, m, d), lambda i: (i, 0, 0)),
                  pl.BlockSpec((1, d), const),
                  pl.BlockSpec((d, d), const),
                  pl.BlockSpec((d, d), const)],
        out_specs=[pl.BlockSpec((1, m, d), lambda i: (i, 0, 0)),
                   pl.BlockSpec((1, m, d), lambda i: (i, 0, 0))],
        out_shape=[jax.ShapeDtypeStruct((b, m, d), BF16)] * 2,
        compiler_params=_params("arbitrary"),
        name="memkv",
    )(mem, g, wk, wv)


def _mix_kernel(x_ref, g_ref, win_ref, wpool_ref, pscale_ref, band_ref, wgate_ref, bgate_ref,
                gnorm_ref, wout_ref, ltri_ref, o_ref, uprev_ref, state_ref, oacc_ref):
    t = x_ref.shape[1]
    j = pl.program_id(1)

    @pl.when(j == 0)
    def _():
        uprev_ref[...] = jnp.zeros_like(uprev_ref)
        state_ref[...] = jnp.zeros_like(state_ref)

    x = x_ref[0]
    h = _rms(x, g_ref[...]).astype(BF16)
    z = _dot(h, win_ref[...])
    u = z[:, _U0:_U0 + POOL_W]
    q = z[:, _Q0:_Q0 + GLA_KEY]
    k = z[:, _K0:_K0 + GLA_KEY]
    v = z[:, _V0:_V0 + GLA_VAL]
    r = z[:, _R0:_R0 + GLA_VAL]
    glr = z[:, _G0:_G0 + V7X_LANES]

    u_ext = jnp.concatenate([uprev_ref[...], u], axis=0).astype(BF16)
    uprev_ref[...] = u[t - POOL_TAIL:, :]
    row = lax.broadcasted_iota(I32, (t, POOL_GROUP_W), 0)
    pos = (j * t + row + 1).astype(F32)
    pooled = []
    for gi, w in enumerate(POOL_WINDOWS):
        cols = slice(gi * POOL_GROUP_W, (gi + 1) * POOL_GROUP_W)
        wsum = _dot(band_ref[gi], u_ext[:, cols])
        p = wsum / jnp.minimum(pos, float(w)) - u[:, cols]
        pooled.append(_dot(p.astype(BF16), wpool_ref[gi]))
    pool = jnp.concatenate(pooled, axis=1) * pscale_ref[...]

    gp = _dot(glr.astype(BF16), wgate_ref[...]) + bgate_ref[...]
    g = jax.nn.log_sigmoid(gp) / GATE_NORMALIZER
    g_hi = g.astype(BF16)
    g_lo = (g - g_hi.astype(F32)).astype(BF16)
    gc = _dot(ltri_ref[...], g_hi) + _dot(ltri_ref[...], g_lo)
    qe = q * (GLA_DK ** -0.5) * jnp.exp(gc)
    ke = k * jnp.exp(-gc)

    lane = lax.broadcasted_iota(I32, (CHUNK, V7X_LANES), 1)
    first_head = lane < GLA_DK
    causal = (lax.broadcasted_iota(I32, (CHUNK, CHUNK), 0)
              >= lax.broadcasted_iota(I32, (CHUNK, CHUNK), 1))
    srow = lax.broadcasted_iota(I32, (2 * GLA_DV, V7X_LANES), 0)
    slane = lax.broadcasted_iota(I32, (2 * GLA_DV, V7X_LANES), 1)
    same_head = (srow < GLA_DV) == (slane < GLA_DK)

    states = [state_ref[0], state_ref[1]]
    for c in range(t // CHUNK):
        rows = slice(c * CHUNK, (c + 1) * CHUNK)
        g_last = gc[c * CHUNK + CHUNK - 1:c * CHUNK + CHUNK, :]
        k2 = k[rows] * jnp.exp(g_last - gc[rows])
        dec = jnp.exp(g_last)
        for p in range(GLA_HEADS // 2):
            lanes = slice(p * V7X_LANES, (p + 1) * V7X_LANES)
            vals = slice(p * 2 * GLA_DV, (p + 1) * 2 * GLA_DV)
            qe_p = qe[rows, lanes]
            ke_p = ke[rows, lanes].astype(BF16)
            v_p = v[rows, vals].astype(BF16)
            s_prev = states[p]
            outs = []
            for a in range(2):
                q_a = jnp.where(first_head if a == 0 else ~first_head, qe_p, 0.0)
                att = jnp.where(causal, _dot_nt(q_a.astype(BF16), ke_p), 0.0)
                outs.append(_dot(att.astype(BF16), v_p[:, a * GLA_DV:(a + 1) * GLA_DV]))
            inter = _dot_nt(qe_p.astype(BF16), s_prev.astype(BF16))
            oacc_ref[rows, vals] = jnp.concatenate(outs, axis=1) + inter
            d_state = _dot_tn(v_p, k2[:, lanes].astype(BF16))
            states[p] = s_prev * dec[:, lanes] + jnp.where(same_head, d_state, 0.0)
    state_ref[0] = states[0]
    state_ref[1] = states[1]

    o = oacc_ref[...]
    gla = []
    for hh in range(GLA_HEADS):
        cols = slice(hh * GLA_DV, (hh + 1) * GLA_DV)
        oh = o[:, cols]
        on = oh * lax.rsqrt(jnp.mean(oh * oh, axis=-1, keepdims=True) + EPS) * gnorm_ref[...]
        gla.append(on * jax.nn.silu(r[:, cols]))
    mix = jnp.concatenate([pool] + gla, axis=1).astype(BF16)
    o_ref[0] = x + _dot(mix, wout_ref[...])


def _mix(x, g, win, wpool, pscale, wgate, bgate, gnorm, wout):
    b, s, d = x.shape
    t = SEQ_TILE
    assert s % t == 0 and t % CHUNK == 0 and t >= POOL_TAIL
    i = jnp.arange(t)[:, None] + POOL_TAIL
    jj = jnp.arange(t + POOL_TAIL)[None, :]
    band = jnp.stack([((jj <= i) & (jj > i - w)) for w in POOL_WINDOWS]).astype(BF16)
    ii = jnp.arange(t)
    ltri = ((ii[:, None] // CHUNK == ii[None, :] // CHUNK)
            & (ii[None, :] <= ii[:, None])).astype(BF16)
    c2 = lambda bi, j: (0, 0)
    c3 = lambda bi, j: (0, 0, 0)
    return pl.pallas_call(
        _mix_kernel,
        grid=(b, s // t),
        in_specs=[pl.BlockSpec((1, t, d), lambda bi, j: (bi, j, 0)),
                  pl.BlockSpec((1, d), c2),
                  pl.BlockSpec((d, IN_COLS_PADDED), c2),
                  pl.BlockSpec((len(POOL_WINDOWS), POOL_GROUP_W, POOL_GROUP_W), c3),
                  pl.BlockSpec((1, POOL_W), c2),
                  pl.BlockSpec((len(POOL_WINDOWS), t, t + POOL_TAIL), c3),
                  pl.BlockSpec((V7X_LANES, GLA_KEY), c2),
                  pl.BlockSpec((1, GLA_KEY), c2),
                  pl.BlockSpec((1, GLA_DV), c2),
                  pl.BlockSpec((POOL_W + GLA_VAL, d), c2),
                  pl.BlockSpec((t, t), c2)],
        out_specs=pl.BlockSpec((1, t, d), lambda bi, j: (bi, j, 0)),
        out_shape=jax.ShapeDtypeStruct((b, s, d), F32),
        scratch_shapes=[pltpu.VMEM((POOL_TAIL, POOL_W), F32),
                        pltpu.VMEM((GLA_HEADS // 2, 2 * GLA_DV, V7X_LANES), F32),
                        pltpu.VMEM((t, GLA_VAL), F32)],
        compiler_params=_params("arbitrary", "arbitrary"),
        name="mix",
    )(x, g, win, wpool, pscale, band, wgate, bgate, gnorm, wout, ltri)


def _xattn_kernel(x_ref, g_ref, wq_ref, k_ref, v_ref, wo_ref, o_ref):
    x = x_ref[0]
    h = _rms(x, g_ref[...]).astype(BF16)
    q = _dot(h, wq_ref[...])
    heads = []
    for hh in range(XATTN_HEADS):
        cols = slice(hh * XATTN_HD, (hh + 1) * XATTN_HD)
        sc = _dot_nt(q[:, cols].astype(BF16), k_ref[0][:, cols])
        e = jnp.exp(sc - jnp.max(sc, axis=-1, keepdims=True))
        p = e / jnp.sum(e, axis=-1, keepdims=True)
        heads.append(_dot(p.astype(BF16), v_ref[0][:, cols]))
    o = jnp.concatenate(heads, axis=1).astype(BF16)
    o_ref[0] = x + _dot(o, wo_ref[...])


def _xattn(x, g, wq, kmem, vmem, wo):
    b, s, d = x.shape
    m = kmem.shape[1]
    t = SEQ_TILE
    c2 = lambda bi, j: (0, 0)
    return pl.pallas_call(
        _xattn_kernel,
        grid=(b, s // t),
        in_specs=[pl.BlockSpec((1, t, d), lambda bi, j: (bi, j, 0)),
                  pl.BlockSpec((1, d), c2),
                  pl.BlockSpec((d, d), c2),
                  pl.BlockSpec((1, m, d), lambda bi, j: (bi, 0, 0)),
                  pl.BlockSpec((1, m, d), lambda bi, j: (bi, 0, 0)),
                  pl.BlockSpec((d, d), c2)],
        out_specs=pl.BlockSpec((1, t, d), lambda bi, j: (bi, j, 0)),
        out_shape=jax.ShapeDtypeStruct((b, s, d), F32),
        compiler_params=_params("arbitrary", "arbitrary"),
        name="xattn",
    )(x, g, wq, kmem, vmem, wo)


def _route_kernel(x_ref, g_ref, wrt_hi_ref, wrt_lo_ref, br_ref, utri_ref,
                  h_ref, eid_ref, rank_ref, gate_ref, cnt_ref, carry_ref):
    t = x_ref.shape[0]

    @pl.when(pl.program_id(0) == 0)
    def _():
        carry_ref[...] = jnp.zeros_like(carry_ref)

    h = _rms(x_ref[...], g_ref[...])
    h_ref[...] = h
    h_hi = h.astype(BF16)
    h_lo = (h - h_hi.astype(F32)).astype(BF16)
    logits = (_dot_nt(wrt_hi_ref[...], h_hi) + _dot_nt(wrt_hi_ref[...], h_lo)
              + _dot_nt(wrt_lo_ref[...], h_hi)) + br_ref[...]

    e_iota = lax.broadcasted_iota(I32, (N_EXPERTS, t), 0)
    e_iota_f = e_iota.astype(F32)
    work = logits
    vals, onehots = [], []
    for _ in range(TOP_K):
        m = jnp.max(work, axis=0, keepdims=True)
        idx = jnp.min(jnp.where(work == m, e_iota_f, float(N_EXPERTS)), axis=0, keepdims=True)
        hit = e_iota_f == idx
        vals.append(m)
        onehots.append(hit)
        work = jnp.where(hit, -jnp.inf, work)

    ex = [jnp.exp(vk - vals[0]) for vk in vals]
    den = ex[0] + ex[1] + ex[2] + ex[3]

    member = jnp.zeros((N_EXPERTS, t), F32)
    for hit in onehots:
        member = member + jnp.where(hit, 1.0, 0.0)
    carry = carry_ref[...]
    before = _dot(member.astype(BF16), utri_ref[...]) + carry[:, 0:1]

    row4 = lax.broadcasted_iota(I32, (TOP_K, t), 0)
    grow = lax.broadcasted_iota(I32, (V7X_LANES, t), 0)
    eid = jnp.zeros((TOP_K, t), I32)
    rank = jnp.zeros((TOP_K, t), I32)
    gates_t = jnp.zeros((V7X_LANES, t), F32)
    for kk in range(TOP_K):
        idx_k = jnp.sum(jnp.where(onehots[kk], e_iota_f, 0.0), axis=0, keepdims=True)
        rank_k = jnp.sum(jnp.where(onehots[kk], before, 0.0), axis=0, keepdims=True)
        eid = jnp.where(row4 == kk, idx_k.astype(I32), eid)
        rank = jnp.where(row4 == kk, rank_k.astype(I32), rank)
        gates_t = jnp.where(grow == kk, ex[kk] / den, gates_t)
    eid_ref[...] = eid
    rank_ref[...] = rank
    gate_ref[...] = gates_t.T

    new_carry = carry + jnp.sum(member, axis=1, keepdims=True)
    carry_ref[...] = new_carry
    cnt_ref[...] = new_carry


def _route(x2d, g, wrt_hi, wrt_lo, br):
    n, d = x2d.shape
    t = SEQ_TILE
    ii = jnp.arange(t)
    utri = (ii[:, None] < ii[None, :]).astype(BF16)
    c2 = lambda i: (0, 0)
    return pl.pallas_call(
        _route_kernel,
        grid=(n // t,),
        in_specs=[pl.BlockSpec((t, d), lambda i: (i, 0)),
                  pl.BlockSpec((1, d), c2),
                  pl.BlockSpec((N_EXPERTS, d), c2),
                  pl.BlockSpec((N_EXPERTS, d), c2),
                  pl.BlockSpec((N_EXPERTS, 1), c2),
                  pl.BlockSpec((t, t), c2)],
        out_specs=[pl.BlockSpec((t, d), lambda i: (i, 0)),
                   pl.BlockSpec((TOP_K, t), lambda i: (0, i)),
                   pl.BlockSpec((TOP_K, t), lambda i: (0, i)),
                   pl.BlockSpec((t, V7X_LANES), lambda i: (i, 0)),
                   pl.BlockSpec((N_EXPERTS, V7X_LANES), c2)],
        out_shape=[jax.ShapeDtypeStruct((n, d), F32),
                   jax.ShapeDtypeStruct((TOP_K, n), I32),
                   jax.ShapeDtypeStruct((TOP_K, n), I32),
                   jax.ShapeDtypeStruct((n, V7X_LANES), F32),
                   jax.ShapeDtypeStruct((N_EXPERTS, V7X_LANES), F32)],
        scratch_shapes=[pltpu.VMEM((N_EXPERTS, V7X_LANES), F32)],
        compiler_params=_params("arbitrary"),
        name="route",
    )(x2d, g, wrt_hi, wrt_lo, br, utri)


def _plan_kernel(start_ref, eid_ref, rank_ref, dest_ref):
    e = eid_ref[...]
    dest = rank_ref[...]
    for ee in range(N_EXPERTS):
        dest = dest + jnp.where(e == ee, start_ref[ee], 0)
    dest_ref[...] = dest


def _plan(starts, eid, rank):
    n = eid.shape[1]
    t = min(PLAN_TILE, n)
    blk = pl.BlockSpec((TOP_K, t), lambda i, s: (0, i))
    return pl.pallas_call(
        _plan_kernel,
        grid_spec=pltpu.PrefetchScalarGridSpec(
            num_scalar_prefetch=1, grid=(n // t,), in_specs=[blk, blk], out_specs=blk),
        out_shape=jax.ShapeDtypeStruct((TOP_K, n), I32),
        compiler_params=_params("arbitrary"),
        name="plan",
    )(starts, eid, rank)


def _dispatch_kernel(dest_ref, h_ref, xs_ref, sem, *, n_tok):
    t = h_ref.shape[0]
    base = pl.program_id(0) * t

    def row_copy(r, kk):
        d = dest_ref[kk * n_tok + base + r]
        return pltpu.make_async_copy(h_ref.at[pl.ds(r, 1), :], xs_ref.at[pl.ds(d, 1), :], sem)

    def issue(r, carry):
        for kk in range(TOP_K):
            row_copy(r, kk).start()
        return carry

    lax.fori_loop(0, t, issue, 0)
    for kk in range(TOP_K):
        pltpu.make_async_copy(h_ref, xs_ref.at[pl.ds(0, t), :], sem).wait()


def _dispatch(dest_flat, h):
    n, d = h.shape
    t = min(DISPATCH_TILE, n)
    return pl.pallas_call(
        functools.partial(_dispatch_kernel, n_tok=n),
        grid_spec=pltpu.PrefetchScalarGridSpec(
            num_scalar_prefetch=1, grid=(n // t,),
            in_specs=[pl.BlockSpec((t, d), lambda i, s: (i, 0))],
            out_specs=pl.BlockSpec(memory_space=pl.ANY),
            scratch_shapes=[pltpu.SemaphoreType.DMA(())]),
        out_shape=jax.ShapeDtypeStruct((TOP_K * n, d), F32),
        compiler_params=_params("arbitrary"),
        name="dispatch",
    )(dest_flat, h)


_ITEM_VALID, _ITEM_FIRST, _ITEM_NEW_EXPERT = 1, 2, 4


def _expert_kernel(blk_ref, exp_ref, lo_ref, hi_ref, flag_ref,
                   xs_ref, wgu_ref, bgu_ref, wdn_ref, bdn_ref, y_ref, wgu_bf, wdn_bf):
    i = pl.program_id(0)
    flags = flag_ref[i]

    @pl.when((flags & _ITEM_NEW_EXPERT) != 0)
    def _():
        wgu_bf[...] = wgu_ref[0].astype(BF16)
        wdn_bf[...] = wdn_ref[0].astype(BF16)

    def ffn():
        gu = _dot(xs_ref[...].astype(BF16), wgu_bf[...]) + bgu_ref[0]
        gate = jnp.minimum(gu[:, :D_FF], SWIGLU_LIMIT)
        up = jnp.clip(gu[:, D_FF:], -SWIGLU_LIMIT, SWIGLU_LIMIT)
        act = gate * jax.nn.sigmoid(SWIGLU_ALPHA * gate) * (up + 1.0)
        yb = _dot(act.astype(BF16), wdn_bf[...]) + bdn_ref[0]
        row = lax.broadcasted_iota(I32, yb.shape, 0)
        return yb, (row >= lo_ref[i]) & (row < hi_ref[i])

    @pl.when((flags & (_ITEM_VALID | _ITEM_FIRST)) == (_ITEM_VALID | _ITEM_FIRST))
    def _():
        yb, mine = ffn()
        y_ref[...] = jnp.where(mine, yb, 0.0)

    @pl.when((flags & (_ITEM_VALID | _ITEM_FIRST)) == _ITEM_VALID)
    def _():
        yb, mine = ffn()
        y_ref[...] = jnp.where(mine, yb, y_ref[...])


def _expert_items(counts, n_rows):
    bm = EXPERT_ROWS
    n_items = n_rows // bm + N_EXPERTS - 1
    ends = jnp.cumsum(counts)
    starts = ends - counts
    blocks_of = jnp.where(counts > 0, (ends - 1) // bm - starts // bm + 1, 0)
    item_end = jnp.cumsum(blocks_of)
    item_start = item_end - blocks_of
    total = item_end[-1]
    idx = jnp.arange(n_items, dtype=I32)
    valid = idx < total
    last = jnp.maximum(total - 1, 0)
    idc = jnp.minimum(idx, last)
    e = jnp.minimum(jnp.sum((item_end[None, :] <= idc[:, None]).astype(I32), axis=1), N_EXPERTS - 1)
    blk = (starts[e] // bm + (idc - item_start[e])).astype(I32)
    lo = jnp.clip(starts[e] - blk * bm, 0, bm).astype(I32)
    hi = jnp.clip(ends[e] - blk * bm, 0, bm).astype(I32)
    prev_blk = jnp.concatenate([jnp.full((1,), -1, I32), blk[:-1]])
    prev_e = jnp.concatenate([jnp.full((1,), -1, I32), e[:-1]])
    flags = jnp.where(valid,
                      _ITEM_VALID
                      + jnp.where(blk != prev_blk, _ITEM_FIRST, 0)
                      + jnp.where(e != prev_e, _ITEM_NEW_EXPERT, 0), 0).astype(I32)
    return blk, e, lo, hi, flags


def _experts(items, xs, wgu, bgu, wdn, bdn):
    n_rows, d = xs.shape
    bm = EXPERT_ROWS
    blk, e, lo, hi, flags = items
    n_items = blk.shape[0]
    by_block = lambda i, blk, e, lo, hi, fl: (blk[i], 0)
    by_expert = lambda i, blk, e, lo, hi, fl: (e[i], 0, 0)
    return pl.pallas_call(
        _expert_kernel,
        grid_spec=pltpu.PrefetchScalarGridSpec(
            num_scalar_prefetch=5, grid=(n_items,),
            in_specs=[pl.BlockSpec((bm, d), by_block),
                      pl.BlockSpec((1, d, 2 * D_FF), by_expert),
                      pl.BlockSpec((1, 1, 2 * D_FF), by_expert),
                      pl.BlockSpec((1, D_FF, d), by_expert),
                      pl.BlockSpec((1, 1, d), by_expert)],
            out_specs=pl.BlockSpec((bm, d), by_block),
            scratch_shapes=[pltpu.VMEM((d, 2 * D_FF), BF16),
                            pltpu.VMEM((D_FF, d), BF16)]),
        out_shape=jax.ShapeDtypeStruct((n_rows, d), F32),
        compiler_params=_params("arbitrary"),
        name="experts",
    )(blk, e, lo, hi, flags, xs, wgu, bgu, wdn, bdn)


def _combine_kernel(dest_ref, x_ref, gate_ref, g_ref, y_ref, o_ref, buf, sem, *, n_tok):
    t = x_ref.shape[0]
    base = pl.program_id(0) * t

    def issue(r, carry):
        for kk in range(TOP_K):
            d = dest_ref[kk * n_tok + base + r]
            pltpu.make_async_copy(y_ref.at[pl.ds(d, 1), :], buf.at[kk, pl.ds(r, 1), :], sem).start()
        return carry

    lax.fori_loop(0, t, issue, 0)
    for kk in range(TOP_K):
        pltpu.make_async_copy(y_ref.at[pl.ds(0, t), :], buf.at[kk], sem).wait()

    gates = gate_ref[...]
    moe = jnp.zeros(x_ref.shape, F32)
    for kk in range(TOP_K):
        moe = moe + buf[kk] * gates[:, kk:kk + 1]
    o_ref[...] = _rms(x_ref[...] + moe, g_ref[...])


def _combine(dest_flat, x2d, gates, g, y):
    n, d = x2d.shape
    t = min(COMBINE_TILE, n)
    return pl.pallas_call(
        functools.partial(_combine_kernel, n_tok=n),
        grid_spec=pltpu.PrefetchScalarGridSpec(
            num_scalar_prefetch=1, grid=(n // t,),
            in_specs=[pl.BlockSpec((t, d), lambda i, s: (i, 0)),
                      pl.BlockSpec((t, V7X_LANES), lambda i, s: (i, 0)),
                      pl.BlockSpec((1, d), lambda i, s: (0, 0)),
                      pl.BlockSpec(memory_space=pl.ANY)],
            out_specs=pl.BlockSpec((t, d), lambda i, s: (i, 0)),
            scratch_shapes=[pltpu.VMEM((TOP_K, t, d), F32),
                            pltpu.SemaphoreType.DMA(())]),
        out_shape=jax.ShapeDtypeStruct((n, d), F32),
        compiler_params=_params("arbitrary"),
        name="combine",
    )(dest_flat, x2d, gates, g, y)


def _layer(x, mem, norm_mix_g, w_in, w_pool, pool_scale, w_gate_up, b_gate_up, gla_norm_g, w_out,
           norm_xattn_g, norm_mem_g, w_xq, w_xk, w_xv, w_xo,
           norm_moe_g, w_router, b_router, w_gu, b_gu, w_dn, b_dn, out_g):
    b, s, d = x.shape
    n = b * s
    row = lambda a: a.reshape(1, -1)

    s0, s1, s2, s3, s4 = (POOL_W, POOL_W + GLA_KEY, POOL_W + 2 * GLA_KEY,
                          POOL_W + 2 * GLA_KEY + GLA_VAL, POOL_W + 2 * GLA_KEY + GLA_VAL + GATE_RANK)
    win = jnp.concatenate(
        [w_in[:, :s3], w_in[:, s4:], w_in[:, s3:s4],
         jnp.zeros((d, V7X_LANES - GATE_RANK), w_in.dtype)], axis=1).astype(BF16)
    wgate = jnp.concatenate(
        [w_gate_up, jnp.zeros((V7X_LANES - GATE_RANK, GLA_KEY), w_gate_up.dtype)], axis=0).astype(BF16)
    wrt = w_router.T
    wrt_hi = wrt.astype(BF16)
    wrt_lo = (wrt - wrt_hi.astype(F32)).astype(BF16)

    kmem, vmem = _memkv(mem, row(norm_mem_g), w_xk.astype(BF16), w_xv.astype(BF16))
    x1 = _mix(x, row(norm_mix_g), win, w_pool.astype(BF16), row(pool_scale), wgate,
              row(b_gate_up), row(gla_norm_g), w_out.astype(BF16))
    x2 = _xattn(x1, row(norm_xattn_g), w_xq.astype(BF16), kmem, vmem, w_xo.astype(BF16))
    x2 = x2.reshape(n, d)

    h, eid, rank, gates, cnt = _route(x2, row(norm_moe_g), wrt_hi, wrt_lo, b_router.reshape(-1, 1))
    counts = cnt[:, 0].astype(I32)
    starts = (jnp.cumsum(counts) - counts).astype(I32)
    dest = _plan(starts, eid, rank).reshape(TOP_K * n)
    xs = _dispatch(dest, h)
    items = _expert_items(counts, TOP_K * n)
    y = _experts(items, xs, w_gu, b_gu.reshape(N_EXPERTS, 1, -1), w_dn, b_dn.reshape(N_EXPERTS, 1, -1))
    out = _combine(dest, x2, gates, row(out_g), y)
    return out.reshape(b, s, d)


def kernel(x, mem, norm_mix_g, w_in, w_pool, pool_scale, w_gate_up, b_gate_up, gla_norm_g, w_out,
           norm_xattn_g, norm_mem_g, w_xq, w_xk, w_xv, w_xo, norm_moe_g, w_router, b_router,
           w_gu, b_gu, w_dn, b_dn, norm_final_g):
    depth = norm_mix_g.shape[0]
    assert depth == 1, "the final rmsnorm is fused into the (single) layer's combine stage"
    return _layer(x, mem, norm_mix_g[0], w_in[0], w_pool[0], pool_scale[0], w_gate_up[0],
                  b_gate_up[0], gla_norm_g[0], w_out[0], norm_xattn_g[0], norm_mem_g[0],
                  w_xq[0], w_xk[0], w_xv[0], w_xo[0], norm_moe_g[0], w_router[0], b_router[0],
                  w_gu[0], b_gu[0], w_dn[0], b_dn[0], norm_final_g)
```

```python
import functools

import jax
import jax.numpy as jnp
from jax import lax
from jax.experimental import pallas as pl
from jax.experimental.pallas import tpu as pltpu

F32 = jnp.float32
BF16 = jnp.bfloat16
I32 = jnp.int32

V7X_LANES = 128
V7X_VMEM_LIMIT_BYTES = 56 * 1024 * 1024

D_MODEL = 1024
POOL_WINDOWS = (2, 4, 8, 16)
POOL_GROUP_W = 128
POOL_W = len(POOL_WINDOWS) * POOL_GROUP_W
GLA_HEADS = 4
GLA_DK = 64
GLA_DV = 128
GLA_KEY = GLA_HEADS * GLA_DK
GLA_VAL = GLA_HEADS * GLA_DV
GATE_RANK = 16
GATE_NORMALIZER = 16.0
CHUNK = 64
XATTN_HEADS = 4
XATTN_HD = D_MODEL // XATTN_HEADS
N_EXPERTS = 32
TOP_K = 4
D_FF = D_MODEL
SWIGLU_LIMIT = 7.0
SWIGLU_ALPHA = 1.702
EPS = 1e-6

_U0 = 0
_Q0 = _U0 + POOL_W
_K0 = _Q0 + GLA_KEY
_V0 = _K0 + GLA_KEY
_R0 = _V0 + GLA_VAL
_G0 = _R0 + GLA_VAL
IN_COLS_PADDED = _G0 + V7X_LANES

SEQ_TILE = 512
POOL_TAIL = 128
PLAN_TILE = 2048
EXPERT_ROWS = 256
COMBINE_TILE = 256


def _rms(x, g):
    ms = jnp.mean(x * x, axis=-1, keepdims=True)
    return x * lax.rsqrt(ms + EPS) * g


def _dot(a, b):
    return jnp.dot(a, b, preferred_element_type=F32)


def _dot_nt(a, b):
    return lax.dot_general(a, b, (((1,), (1,)), ((), ())), preferred_element_type=F32)


def _dot_tn(a, b):
    return lax.dot_general(a, b, (((0,), (0,)), ((), ())), preferred_element_type=F32)


def _params(*semantics):
    return pltpu.CompilerParams(dimension_semantics=semantics,
                                vmem_limit_bytes=V7X_VMEM_LIMIT_BYTES)


def _memkv_kernel(mem_ref, g_ref, wk_ref, wv_ref, k_ref, v_ref):
    hm = _rms(mem_ref[0], g_ref[...]).astype(BF16)
    k = _dot(hm, wk_ref[...])
    v = _dot(hm, wv_ref[...])
    k_ref[0] = (k * (XATTN_HD ** -0.5)).astype(BF16)
    v_ref[0] = v.astype(BF16)


def _memkv(mem, g, wk, wv):
    b, m, d = mem.shape
    const = lambda i: (0, 0)
    return pl.pallas_call(
        _memkv_kernel,
        grid=(b,),
        in_specs=[pl.BlockSpec((1, m, d), lambda i: (i, 0, 0)),
                  pl.BlockSpec((1, d), const),
                  pl.BlockSpec((d, d), const),
                  pl.BlockSpec((d, d), const)],
        out_specs=[pl.BlockSpec((1, m, d), lambda i: (i, 0, 0)),
                   pl.BlockSpec((1, m, d), lambda i: (i, 0, 0))],
        out_shape=[jax.ShapeDtypeStruct((b, m, d), BF16)] * 2,
        compiler_params=_params("arbitrary"),
        name="memkv",
    )(mem, g, wk, wv)


def _mix_kernel(x_ref, g_ref, win_ref, wpool_ref, pscale_ref, band_ref, wgate_ref, bgate_ref,
                gnorm_ref, wout_ref, ltri_ref, o_ref, uprev_ref, state_ref, oacc_ref):
    t = x_ref.shape[1]
    j = pl.program_id(1)

    @pl.when(j == 0)
    def _():
        uprev_ref[...] = jnp.zeros_like(uprev_ref)
        state_ref[...] = jnp.zeros_like(state_ref)

    x = x_ref[0]
    h = _rms(x, g_ref[...]).astype(BF16)
    z = _dot(h, win_ref[...])
    u = z[:, _U0:_U0 + POOL_W]
    q = z[:, _Q0:_Q0 + GLA_KEY]
    k = z[:, _K0:_K0 + GLA_KEY]
    v = z[:, _V0:_V0 + GLA_VAL]
    r = z[:, _R0:_R0 + GLA_VAL]
    glr = z[:, _G0:_G0 + V7X_LANES]

    u_ext = jnp.concatenate([uprev_ref[...], u], axis=0).astype(BF16)
    uprev_ref[...] = u[t - POOL_TAIL:, :]
    row = lax.broadcasted_iota(I32, (t, POOL_GROUP_W), 0)
    pos = (j * t + row + 1).astype(F32)
    pooled = []
    for gi, w in enumerate(POOL_WINDOWS):
        cols = slice(gi * POOL_GROUP_W, (gi + 1) * POOL_GROUP_W)
        wsum = _dot(band_ref[gi], u_ext[:, cols])
        p = wsum / jnp.minimum(pos, float(w)) - u[:, cols]
        pooled.append(_dot(p.astype(BF16), wpool_ref[gi]))
    pool = jnp.concatenate(pooled, axis=1) * pscale_ref[...]

    gp = _dot(glr.astype(BF16), wgate_ref[...]) + bgate_ref[...]
    g = jax.nn.log_sigmoid(gp) / GATE_NORMALIZER
    g_hi = g.astype(BF16)
    g_lo = (g - g_hi.astype(F32)).astype(BF16)
    gc = _dot(ltri_ref[...], g_hi) + _dot(ltri_ref[...], g_lo)
    qe = q * (GLA_DK ** -0.5) * jnp.exp(gc)
    ke = k * jnp.exp(-gc)

    lane = lax.broadcasted_iota(I32, (CHUNK, V7X_LANES), 1)
    first_head = lane < GLA_DK
    causal = (lax.broadcasted_iota(I32, (CHUNK, CHUNK), 0)
              >= lax.broadcasted_iota(I32, (CHUNK, CHUNK), 1))
    srow = lax.broadcasted_iota(I32, (2 * GLA_DV, V7X_LANES), 0)
    slane = lax.broadcasted_iota(I32, (2 * GLA_DV, V7X_LANES), 1)
    same_head = (srow < GLA_DV) == (slane < GLA_DK)

    states = [state_ref[0], state_ref[1]]
    for c in range(t // CHUNK):
        rows = slice(c * CHUNK, (c + 1) * CHUNK)
        g_last = gc[c * CHUNK + CHUNK - 1:c * CHUNK + CHUNK, :]
        k2 = k[rows] * jnp.exp(g_last - gc[rows])
        dec = jnp.exp(g_last)
        for p in range(GLA_HEADS // 2):
            lanes = slice(p * V7X_LANES, (p + 1) * V7X_LANES)
            vals = slice(p * 2 * GLA_DV, (p + 1) * 2 * GLA_DV)
            qe_p = qe[rows, lanes]
            ke_p = ke[rows, lanes].astype(BF16)
            v_p = v[rows, vals].astype(BF16)
            s_prev = states[p]
            outs = []
            for a in range(2):
                q_a = jnp.where(first_head if a == 0 else ~first_head, qe_p, 0.0)
                att = jnp.where(causal, _dot_nt(q_a.astype(BF16), ke_p), 0.0)
                outs.append(_dot(att.astype(BF16), v_p[:, a * GLA_DV:(a + 1) * GLA_DV]))
            inter = _dot_nt(qe_p.astype(BF16), s_prev.astype(BF16))
            oacc_ref[rows, vals] = jnp.concatenate(outs, axis=1) + inter
            d_state = _dot_tn(v_p, k2[:, lanes].astype(BF16))
            states[p] = s_prev * dec[:, lanes] + jnp.where(same_head, d_state, 0.0)
    state_ref[0] = states[0]
    state_ref[1] = states[1]

    o = oacc_ref[...]
    gla = []
    for hh in range(GLA_HEADS):
        cols = slice(hh * GLA_DV, (hh + 1) * GLA_DV)
        oh = o[:, cols]
        on = oh * lax.rsqrt(jnp.mean(oh * oh, axis=-1, keepdims=True) + EPS) * gnorm_ref[...]
        gla.append(on * jax.nn.silu(r[:, cols]))
    mix = jnp.concatenate([pool] + gla, axis=1).astype(BF16)
    o_ref[0] = x + _dot(mix, wout_ref[...])


def _mix(x, g, win, wpool, pscale, wgate, bgate, gnorm, wout):
    b, s, d = x.shape
    t = SEQ_TILE
    assert s % t == 0 and t % CHUNK == 0 and t >= POOL_TAIL
    i = jnp.arange(t)[:, None] + POOL_TAIL
    jj = jnp.arange(t + POOL_TAIL)[None, :]
    band = jnp.stack([((jj <= i) & (jj > i - w)) for w in POOL_WINDOWS]).astype(BF16)
    ii = jnp.arange(t)
    ltri = ((ii[:, None] // CHUNK == ii[None, :] // CHUNK)
            & (ii[None, :] <= ii[:, None])).astype(BF16)
    c2 = lambda bi, j: (0, 0)
    c3 = lambda bi, j: (0, 0, 0)
    return pl.pallas_call(
        _mix_kernel,
        grid=(b, s // t),
        in_specs=[pl.BlockSpec((1, t, d), lambda bi, j: (bi, j, 0)),
                  pl.BlockSpec((1, d), c2),
                  pl.BlockSpec((d, IN_COLS_PADDED), c2),
                  pl.BlockSpec((len(POOL_WINDOWS), POOL_GROUP_W, POOL_GROUP_W), c3),
                  pl.BlockSpec((1, POOL_W), c2),
                  pl.BlockSpec((len(POOL_WINDOWS), t, t + POOL_TAIL), c3),
                  pl.BlockSpec((V7X_LANES, GLA_KEY), c2),
                  pl.BlockSpec((1, GLA_KEY), c2),
                  pl.BlockSpec((1, GLA_DV), c2),
                  pl.BlockSpec((POOL_W + GLA_VAL, d), c2),
                  pl.BlockSpec((t, t), c2)],
        out_specs=pl.BlockSpec((1, t, d), lambda bi, j: (bi, j, 0)),
        out_shape=jax.ShapeDtypeStruct((b, s, d), F32),
        scratch_shapes=[pltpu.VMEM((POOL_TAIL, POOL_W), F32),
                        pltpu.VMEM((GLA_HEADS // 2, 2 * GLA_DV, V7X_LANES), F32),
                        pltpu.VMEM((t, GLA_VAL), F32)],
        compiler_params=_params("arbitrary", "arbitrary"),
        name="mix",
    )(x, g, win, wpool, pscale, band, wgate, bgate, gnorm, wout, ltri)


def _xattn_kernel(x_ref, g_ref, wq_ref, k_ref, v_ref, wo_ref, o_ref):
    x = x_ref[0]
    h = _rms(x, g_ref[...]).astype(BF16)
    q = _dot(h, wq_ref[...])
    heads = []
    for hh in range(XATTN_HEADS):
        cols = slice(hh * XATTN_HD, (hh + 1) * XATTN_HD)
        sc = _dot_nt(q[:, cols].astype(BF16), k_ref[0][:, cols])
        e = jnp.exp(sc - jnp.max(sc, axis=-1, keepdims=True))
        p = e / jnp.sum(e, axis=-1, keepdims=True)
        heads.append(_dot(p.astype(BF16), v_ref[0][:, cols]))
    o = jnp.concatenate(heads, axis=1).astype(BF16)
    o_ref[0] = x + _dot(o, wo_ref[...])


def _xattn(x, g, wq, kmem, vmem, wo):
    b, s, d = x.shape
    m = kmem.shape[1]
    t = SEQ_TILE
    c2 = lambda bi, j: (0, 0)
    return pl.pallas_call(
        _xattn_kernel,
        grid=(b, s // t),
        in_specs=[pl.BlockSpec((1, t, d), lambda bi, j: (bi, j, 0)),
                  pl.BlockSpec((1, d), c2),
                  pl.BlockSpec((d, d), c2),
                  pl.BlockSpec((1, m, d), lambda bi, j: (bi, 0, 0)),
                  pl.BlockSpec((1, m, d), lambda bi, j: (bi, 0, 0)),
                  pl.BlockSpec((d, d), c2)],
        out_specs=pl.BlockSpec((1, t, d), lambda bi, j: (bi, j, 0)),
        out_shape=jax.ShapeDtypeStruct((b, s, d), F32),
        compiler_params=_params("arbitrary", "arbitrary"),
        name="xattn",
    )(x, g, wq, kmem, vmem, wo)


def _route_kernel(x_ref, g_ref, wrt_hi_ref, wrt_lo_ref, br_ref, utri_ref,
                  h_ref, eid_ref, rank_ref, gate_ref, cnt_ref, carry_ref):
    t = x_ref.shape[0]

    @pl.when(pl.program_id(0) == 0)
    def _():
        carry_ref[...] = jnp.zeros_like(carry_ref)

    h = _rms(x_ref[...], g_ref[...])
    h_ref[...] = h
    h_hi = h.astype(BF16)
    h_lo = (h - h_hi.astype(F32)).astype(BF16)
    logits = (_dot_nt(wrt_hi_ref[...], h_hi) + _dot_nt(wrt_hi_ref[...], h_lo)
              + _dot_nt(wrt_lo_ref[...], h_hi)) + br_ref[...]

    e_iota = lax.broadcasted_iota(I32, (N_EXPERTS, t), 0)
    e_iota_f = e_iota.astype(F32)
    work = logits
    vals, onehots = [], []
    for _ in range(TOP_K):
        m = jnp.max(work, axis=0, keepdims=True)
        idx = jnp.min(jnp.where(work == m, e_iota_f, float(N_EXPERTS)), axis=0, keepdims=True)
        hit = e_iota_f == idx
        vals.append(m)
        onehots.append(hit)
        work = jnp.where(hit, -jnp.inf, work)

    ex = [jnp.exp(vk - vals[0]) for vk in vals]
    den = ex[0] + ex[1] + ex[2] + ex[3]

    member = jnp.zeros((N_EXPERTS, t), F32)
    for hit in onehots:
        member = member + jnp.where(hit, 1.0, 0.0)
    carry = carry_ref[...]
    before = _dot(member.astype(BF16), utri_ref[...]) + carry[:, 0:1]

    row4 = lax.broadcasted_iota(I32, (TOP_K, t), 0)
    grow = lax.broadcasted_iota(I32, (V7X_LANES, t), 0)
    eid = jnp.zeros((TOP_K, t), I32)
    rank = jnp.zeros((TOP_K, t), I32)
    gates_t = jnp.zeros((V7X_LANES, t), F32)
    for kk in range(TOP_K):
        idx_k = jnp.sum(jnp.where(onehots[kk], e_iota_f, 0.0), axis=0, keepdims=True)
        rank_k = jnp.sum(jnp.where(onehots[kk], before, 0.0), axis=0, keepdims=True)
        eid = jnp.where(row4 == kk, idx_k.astype(I32), eid)
        rank = jnp.where(row4 == kk, rank_k.astype(I32), rank)
        gates_t = jnp.where(grow == kk, ex[kk] / den, gates_t)
    eid_ref[...] = eid
    rank_ref[...] = rank
    gate_ref[...] = gates_t.T

    new_carry = carry + jnp.sum(member, axis=1, keepdims=True)
    carry_ref[...] = new_carry
    cnt_ref[...] = new_carry


def _route(x2d, g, wrt_hi, wrt_lo, br):
    n, d = x2d.shape
    t = SEQ_TILE
    ii = jnp.arange(t)
    utri = (ii[:, None] < ii[None, :]).astype(BF16)
    c2 = lambda i: (0, 0)
    return pl.pallas_call(
        _route_kernel,
        grid=(n // t,),
        in_specs=[pl.BlockSpec((t, d), lambda i: (i, 0)),
                  pl.BlockSpec((1, d), c2),
                  pl.BlockSpec((N_EXPERTS, d), c2),
                  pl.BlockSpec((N_EXPERTS, d), c2),
                  pl.BlockSpec((N_EXPERTS, 1), c2),
                  pl.BlockSpec((t, t), c2)],
        out_specs=[pl.BlockSpec((t, d), lambda i: (i, 0)),
                   pl.BlockSpec((TOP_K, t), lambda i: (0, i)),
                   pl.BlockSpec((TOP_K, t), lambda i: (0, i)),
                   pl.BlockSpec((t, V7X_LANES), lambda i: (i, 0)),
                   pl.BlockSpec((N_EXPERTS, V7X_LANES), c2)],
        out_shape=[jax.ShapeDtypeStruct((n, d), F32),
                   jax.ShapeDtypeStruct((TOP_K, n), I32),
                   jax.ShapeDtypeStruct((TOP_K, n), I32),
                   jax.ShapeDtypeStruct((n, V7X_LANES), F32),
                   jax.ShapeDtypeStruct((N_EXPERTS, V7X_LANES), F32)],
        scratch_shapes=[pltpu.VMEM((N_EXPERTS, V7X_LANES), F32)],
        compiler_params=_params("arbitrary"),
        name="route",
    )(x2d, g, wrt_hi, wrt_lo, br, utri)


def _plan_kernel(start_ref, eid_ref, rank_ref, dest_ref):
    e = eid_ref[...]
    dest = rank_ref[...]
    for ee in range(N_EXPERTS):
        dest = dest + jnp.where(e == ee, start_ref[ee], 0)
    dest_ref[...] = dest


def _plan(starts, eid, rank):
    n = eid.shape[1]
    t = min(PLAN_TILE, n)
    blk = pl.BlockSpec((TOP_K, t), lambda i, s: (0, i))
    return pl.pallas_call(
        _plan_kernel,
        grid_spec=pltpu.PrefetchScalarGridSpec(
            num_scalar_prefetch=1, grid=(n // t,), in_specs=[blk, blk], out_specs=blk),
        out_shape=jax.ShapeDtypeStruct((TOP_K, n), I32),
        compiler_params=_params("arbitrary"),
        name="plan",
    )(starts, eid, rank)


def _invert_kernel(dest_ref, inv_ref):
    def body(i, carry):
        inv_ref[dest_ref[i]] = i
        return carry

    lax.fori_loop(0, dest_ref.shape[0], body, 0, unroll=8)


def _invert(dest_flat):
    return pl.pallas_call(
        _invert_kernel,
        grid_spec=pltpu.PrefetchScalarGridSpec(
            num_scalar_prefetch=1, grid=(1,), in_specs=[],
            out_specs=pl.BlockSpec(memory_space=pltpu.SMEM)),
        out_shape=jax.ShapeDtypeStruct(dest_flat.shape, I32),
        compiler_params=_params("arbitrary"),
        name="invert",
    )(dest_flat)


_ITEM_VALID, _ITEM_FIRST, _ITEM_NEW_EXPERT = 1, 2, 4


def _expert_kernel(blk_ref, lo_ref, hi_ref, flag_ref, exp_ref, inv_ref,
                   h_ref, wgu_ref, bgu_ref, wdn_ref, bdn_ref, out_ref,
                   wgu_bf, wdn_bf, xs_buf, y_buf, gsem, ssem, *, n_tok, n_blocks):
    bm = xs_buf.shape[1]
    i = pl.program_id(0)
    flags = flag_ref[i]
    blk = blk_ref[i]
    slot = blk & 1

    def gather(b, s):
        return [pltpu.make_async_copy(
            h_ref.at[pl.ds(inv_ref[b * bm + r] & (n_tok - 1), 1), :],
            xs_buf.at[s, pl.ds(r, 1), :], gsem.at[s]) for r in range(bm)]

    def scatter(b, s):
        return [pltpu.make_async_copy(
            y_buf.at[s, pl.ds(r, 1), :],
            out_ref.at[pl.ds(inv_ref[b * bm + r], 1), :], ssem.at[s]) for r in range(bm)]

    def wait_gather(s):
        pltpu.make_async_copy(h_ref.at[pl.ds(0, bm), :], xs_buf.at[s], gsem.at[s]).wait()

    def wait_scatter(s):
        pltpu.make_async_copy(y_buf.at[s], out_ref.at[pl.ds(0, bm), :], ssem.at[s]).wait()

    @pl.when(i == 0)
    def _():
        for c in gather(0, 0):
            c.start()

    @pl.when((flags & _ITEM_NEW_EXPERT) != 0)
    def _():
        wgu_bf[...] = wgu_ref[0].astype(BF16)
        wdn_bf[...] = wdn_ref[0].astype(BF16)

    def ffn():
        gu = _dot(xs_buf[slot].astype(BF16), wgu_bf[...]) + bgu_ref[0]
        gate = jnp.minimum(gu[:, :D_FF], SWIGLU_LIMIT)
        up = jnp.clip(gu[:, D_FF:], -SWIGLU_LIMIT, SWIGLU_LIMIT)
        act = gate * jax.nn.sigmoid(SWIGLU_ALPHA * gate) * (up + 1.0)
        yb = _dot(act.astype(BF16), wdn_bf[...]) + bdn_ref[0]
        row = lax.broadcasted_iota(I32, yb.shape, 0)
        return yb, (row >= lo_ref[i]) & (row < hi_ref[i])

    first = (flags & (_ITEM_VALID | _ITEM_FIRST)) == (_ITEM_VALID | _ITEM_FIRST)

    def first_item(has_prev):
        wait_gather(slot)
        for c in gather(jnp.minimum(blk + 1, n_blocks - 1), 1 - slot):
            c.start()
        if has_prev:
            for c in scatter(blk - 1, 1 - slot):
                c.start()
        yb, mine = ffn()
        y_buf[slot] = jnp.where(mine, yb, 0.0)

    @pl.when(first & (blk == 0))
    def _():
        first_item(False)

    @pl.when(first & (blk == 1))
    def _():
        first_item(True)

    @pl.when(first & (blk >= 2))
    def _():
        wait_scatter(slot)
        first_item(True)

    @pl.when((flags & (_ITEM_VALID | _ITEM_FIRST)) == _ITEM_VALID)
    def _():
        yb, mine = ffn()
        y_buf[slot] = jnp.where(mine, yb, y_buf[slot])

    @pl.when(i == pl.num_programs(0) - 1)
    def _():
        last = (n_blocks - 1) & 1
        for c in scatter(n_blocks - 1, last):
            c.start()
        wait_scatter(1 - last)
        wait_scatter(last)
        wait_gather(1 - last)


def _expert_items(counts, n_rows):
    bm = EXPERT_ROWS
    n_items = n_rows // bm + N_EXPERTS - 1
    ends = jnp.cumsum(counts)
    starts = ends - counts
    blocks_of = jnp.where(counts > 0, (ends - 1) // bm - starts // bm + 1, 0)
    item_end = jnp.cumsum(blocks_of)
    item_start = item_end - blocks_of
    total = item_end[-1]
    idx = jnp.arange(n_items, dtype=I32)
    valid = idx < total
    idc = jnp.minimum(idx, jnp.maximum(total - 1, 0))
    owner = ((item_start[None, :] <= idc[:, None]) & (idc[:, None] < item_end[None, :])).astype(I32)
    pick = lambda v: jnp.sum(owner * v[None, :], axis=1)
    e = pick(jnp.arange(N_EXPERTS, dtype=I32))
    blk = pick(starts // bm) + idc - pick(item_start)
    lo = jnp.clip(pick(starts) - blk * bm, 0, bm)
    hi = jnp.clip(pick(ends) - blk * bm, 0, bm)
    prev_blk = jnp.concatenate([jnp.full((1,), -1, I32), blk[:-1]])
    prev_e = jnp.concatenate([jnp.full((1,), -1, I32), e[:-1]])
    flags = jnp.where(valid,
                      _ITEM_VALID
                      + jnp.where(blk != prev_blk, _ITEM_FIRST, 0)
                      + jnp.where(e != prev_e, _ITEM_NEW_EXPERT, 0), 0)
    as_i32 = lambda v: v.astype(I32)
    return as_i32(blk), as_i32(lo), as_i32(hi), as_i32(flags), as_i32(e)


def _experts(items, inv, h, wgu, bgu, wdn, bdn):
    n, d = h.shape
    bm = EXPERT_ROWS
    n_rows = inv.shape[0]
    n_blocks = n_rows // bm
    assert n_rows % bm == 0 and n_blocks >= 2 and n & (n - 1) == 0
    blk, lo, hi, flags, e = items
    by_expert = lambda i, blk, lo, hi, fl, e, inv: (e[i], 0, 0)
    return pl.pallas_call(
        functools.partial(_expert_kernel, n_tok=n, n_blocks=n_blocks),
        grid_spec=pltpu.PrefetchScalarGridSpec(
            num_scalar_prefetch=6, grid=(blk.shape[0],),
            in_specs=[pl.BlockSpec(memory_space=pl.ANY),
                      pl.BlockSpec((1, d, 2 * D_FF), by_expert),
                      pl.BlockSpec((1, 1, 2 * D_FF), by_expert),
                      pl.BlockSpec((1, D_FF, d), by_expert),
                      pl.BlockSpec((1, 1, d), by_expert)],
            out_specs=pl.BlockSpec(memory_space=pl.ANY),
            scratch_shapes=[pltpu.VMEM((d, 2 * D_FF), BF16),
                            pltpu.VMEM((D_FF, d), BF16),
                            pltpu.VMEM((2, bm, d), F32),
                            pltpu.VMEM((2, bm, d), F32),
                            pltpu.SemaphoreType.DMA((2,)),
                            pltpu.SemaphoreType.DMA((2,))]),
        out_shape=jax.ShapeDtypeStruct((n_rows, d), F32),
        compiler_params=_params("arbitrary"),
        name="experts",
    )(blk, lo, hi, flags, e, inv, h, wgu, bgu, wdn, bdn)


def _combine_kernel(x_ref, gate_ref, g_ref, y_ref, o_ref):
    gates = gate_ref[...]
    moe = jnp.zeros(x_ref.shape, F32)
    for kk in range(TOP_K):
        moe = moe + y_ref[kk] * gates[:, kk:kk + 1]
    o_ref[...] = _rms(x_ref[...] + moe, g_ref[...])


def _combine(x2d, gates, g, y):
    n, d = x2d.shape
    t = min(COMBINE_TILE, n)
    return pl.pallas_call(
        _combine_kernel,
        grid=(n // t,),
        in_specs=[pl.BlockSpec((t, d), lambda i: (i, 0)),
                  pl.BlockSpec((t, V7X_LANES), lambda i: (i, 0)),
                  pl.BlockSpec((1, d), lambda i: (0, 0)),
                  pl.BlockSpec((TOP_K, t, d), lambda i: (0, i, 0))],
        out_specs=pl.BlockSpec((t, d), lambda i: (i, 0)),
        out_shape=jax.ShapeDtypeStruct((n, d), F32),
        compiler_params=_params("arbitrary"),
        name="combine",
    )(x2d, gates, g, y.reshape(TOP_K, n, d))


def _layer(x, mem, norm_mix_g, w_in, w_pool, pool_scale, w_gate_up, b_gate_up, gla_norm_g, w_out,
           norm_xattn_g, norm_mem_g, w_xq, w_xk, w_xv, w_xo,
           norm_moe_g, w_router, b_router, w_gu, b_gu, w_dn, b_dn, out_g):
    b, s, d = x.shape
    n = b * s
    row = lambda a: a.reshape(1, -1)

    s0, s1, s2, s3, s4 = (POOL_W, POOL_W + GLA_KEY, POOL_W + 2 * GLA_KEY,
                          POOL_W + 2 * GLA_KEY + GLA_VAL, POOL_W + 2 * GLA_KEY + GLA_VAL + GATE_RANK)
    win = jnp.concatenate(
        [w_in[:, :s3], w_in[:, s4:], w_in[:, s3:s4],
         jnp.zeros((d, V7X_LANES - GATE_RANK), w_in.dtype)], axis=1).astype(BF16)
    wgate = jnp.concatenate(
        [w_gate_up, jnp.zeros((V7X_LANES - GATE_RANK, GLA_KEY), w_gate_up.dtype)], axis=0).astype(BF16)
    wrt = w_router.T
    wrt_hi = wrt.astype(BF16)
    wrt_lo = (wrt - wrt_hi.astype(F32)).astype(BF16)

    kmem, vmem = _memkv(mem, row(norm_mem_g), w_xk.astype(BF16), w_xv.astype(BF16))
    x1 = _mix(x, row(norm_mix_g), win, w_pool.astype(BF16), row(pool_scale), wgate,
              row(b_gate_up), row(gla_norm_g), w_out.astype(BF16))
    x2 = _xattn(x1, row(norm_xattn_g), w_xq.astype(BF16), kmem, vmem, w_xo.astype(BF16))
    x2 = x2.reshape(n, d)

    h, eid, rank, gates, cnt = _route(x2, row(norm_moe_g), wrt_hi, wrt_lo, b_router.reshape(-1, 1))
    counts = cnt[:, 0].astype(I32)
    starts = (jnp.cumsum(counts) - counts).astype(I32)
    dest = _plan(starts, eid, rank).reshape(TOP_K * n)
    items = _expert_items(counts, TOP_K * n)
    y = _experts(items, _invert(dest), h, w_gu, b_gu.reshape(N_EXPERTS, 1, -1),
                 w_dn, b_dn.reshape(N_EXPERTS, 1, -1))
    out = _combine(x2, gates, row(out_g), y)
    return out.reshape(b, s, d)


def kernel(x, mem, norm_mix_g, w_in, w_pool, pool_scale, w_gate_up, b_gate_up, gla_norm_g, w_out,
           norm_xattn_g, norm_mem_g, w_xq, w_xk, w_xv, w_xo, norm_moe_g, w_router, b_router,
           w_gu, b_gu, w_dn, b_dn, norm_final_g):
    depth = norm_mix_g.shape[0]
    assert depth == 1, "the final rmsnorm is fused into the (single) layer's combine stage"
    return _layer(x, mem, norm_mix_g[0], w_in[0], w_pool[0], pool_scale[0], w_gate_up[0],
                  b_gate_up[0], gla_norm_g[0], w_out[0], norm_xattn_g[0], norm_mem_g[0],
                  w_xq[0], w_xk[0], w_xv[0], w_xo[0], norm_moe_g[0], w_router[0], b_router[0],
                  w_gu[0], b_gu[0], w_dn[0], b_dn[0], norm_final_g)
```

```python
import functools

import jax
import jax.numpy as jnp
from jax import lax
from jax.experimental import pallas as pl
from jax.experimental.pallas import tpu as pltpu

F32 = jnp.float32
BF16 = jnp.bfloat16
I32 = jnp.int32

V7X_LANES = 128
V7X_VMEM_LIMIT_BYTES = 56 * 1024 * 1024

D_MODEL = 1024
POOL_WINDOWS = (2, 4, 8, 16)
POOL_GROUP_W = 128
POOL_W = len(POOL_WINDOWS) * POOL_GROUP_W
GLA_HEADS = 4
GLA_DK = 64
GLA_DV = 128
GLA_KEY = GLA_HEADS * GLA_DK
GLA_VAL = GLA_HEADS * GLA_DV
GATE_RANK = 16
GATE_NORMALIZER = 16.0
CHUNK = 64
XATTN_HEADS = 4
XATTN_HD = D_MODEL // XATTN_HEADS
N_EXPERTS = 32
TOP_K = 4
D_FF = D_MODEL
SWIGLU_LIMIT = 7.0
SWIGLU_ALPHA = 1.702
EPS = 1e-6

_U0 = 0
_Q0 = _U0 + POOL_W
_K0 = _Q0 + GLA_KEY
_V0 = _K0 + GLA_KEY
_R0 = _V0 + GLA_VAL
_G0 = _R0 + GLA_VAL
IN_COLS_PADDED = _G0 + V7X_LANES

SEQ_TILE = 512
POOL_TAIL = 128
PLAN_TILE = 2048
EXPERT_ROWS = 256
COMBINE_TILE = 256


def _rms(x, g):
    ms = jnp.mean(x * x, axis=-1, keepdims=True)
    return x * lax.rsqrt(ms + EPS) * g


def _dot(a, b):
    return jnp.dot(a, b, preferred_element_type=F32)


def _dot_nt(a, b):
    return lax.dot_general(a, b, (((1,), (1,)), ((), ())), preferred_element_type=F32)


def _dot_tn(a, b):
    return lax.dot_general(a, b, (((0,), (0,)), ((), ())), preferred_element_type=F32)


TOKEN_TILE_ROWS = D_MODEL // V7X_LANES


def _load_token_tiles(ref, lead, n_rows, first_row=0):
    start = first_row * TOKEN_TILE_ROWS
    chunks = [ref[lead + (pl.ds(start + c, n_rows, stride=TOKEN_TILE_ROWS), slice(None))]
              for c in range(TOKEN_TILE_ROWS)]
    return jnp.concatenate(chunks, axis=1)


def _store_token_tiles(ref, lead, val, first_row=0):
    start = first_row * TOKEN_TILE_ROWS
    for c in range(TOKEN_TILE_ROWS):
        ref[lead + (pl.ds(start + c, val.shape[0], stride=TOKEN_TILE_ROWS), slice(None))] = (
            val[:, c * V7X_LANES:(c + 1) * V7X_LANES])


def _params(*semantics):
    return pltpu.CompilerParams(dimension_semantics=semantics,
                                vmem_limit_bytes=V7X_VMEM_LIMIT_BYTES)


def _memkv_kernel(mem_ref, g_ref, wk_ref, wv_ref, k_ref, v_ref):
    hm = _rms(mem_ref[0], g_ref[...]).astype(BF16)
    k = _dot(hm, wk_ref[...])
    v = _dot(hm, wv_ref[...])
    k_ref[0] = (k * (XATTN_HD ** -0.5)).astype(BF16)
    v_ref[0] = v.astype(BF16)


def _memkv(mem, g, wk, wv):
    b, m, d = mem.shape
    const = lambda i: (0, 0)
    return pl.pallas_call(
        _memkv_kernel,
        grid=(b,),
        in_specs=[pl.BlockSpec((1, m, d), lambda i: (i, 0, 0)),
                  pl.BlockSpec((1, d), const),
                  pl.BlockSpec((d, d), const),
                  pl.BlockSpec((d, d), const)],
        out_specs=[pl.BlockSpec((1, m, d), lambda i: (i, 0, 0)),
                   pl.BlockSpec((1, m, d), lambda i: (i, 0, 0))],
        out_shape=[jax.ShapeDtypeStruct((b, m, d), BF16)] * 2,
        compiler_params=_params("arbitrary"),
        name="memkv",
    )(mem, g, wk, wv)


def _mix_kernel(x_ref, g_ref, win_ref, wpool_ref, pscale_ref, band_ref, wgate_ref, bgate_ref,
                gnorm_ref, wout_ref, ltri_ref, o_ref, uprev_ref, state_ref, oacc_ref):
    t = x_ref.shape[1]
    j = pl.program_id(1)

    @pl.when(j == 0)
    def _():
        uprev_ref[...] = jnp.zeros_like(uprev_ref)
        state_ref[...] = jnp.zeros_like(state_ref)

    x = x_ref[0]
    h = _rms(x, g_ref[...]).astype(BF16)
    z = _dot(h, win_ref[...])
    u = z[:, _U0:_U0 + POOL_W]
    q = z[:, _Q0:_Q0 + GLA_KEY]
    k = z[:, _K0:_K0 + GLA_KEY]
    v = z[:, _V0:_V0 + GLA_VAL]
    r = z[:, _R0:_R0 + GLA_VAL]
    glr = z[:, _G0:_G0 + V7X_LANES]

    u_ext = jnp.concatenate([uprev_ref[...], u], axis=0).astype(BF16)
    uprev_ref[...] = u[t - POOL_TAIL:, :]
    row = lax.broadcasted_iota(I32, (t, POOL_GROUP_W), 0)
    pos = (j * t + row + 1).astype(F32)
    pooled = []
    for gi, w in enumerate(POOL_WINDOWS):
        cols = slice(gi * POOL_GROUP_W, (gi + 1) * POOL_GROUP_W)
        wsum = _dot(band_ref[gi], u_ext[:, cols])
        p = wsum / jnp.minimum(pos, float(w)) - u[:, cols]
        pooled.append(_dot(p.astype(BF16), wpool_ref[gi]))
    pool = jnp.concatenate(pooled, axis=1) * pscale_ref[...]

    gp = _dot(glr.astype(BF16), wgate_ref[...]) + bgate_ref[...]
    g = jax.nn.log_sigmoid(gp) / GATE_NORMALIZER
    g_hi = g.astype(BF16)
    g_lo = (g - g_hi.astype(F32)).astype(BF16)
    gc = _dot(ltri_ref[...], g_hi) + _dot(ltri_ref[...], g_lo)
    qe = q * (GLA_DK ** -0.5) * jnp.exp(gc)
    ke = k * jnp.exp(-gc)

    lane = lax.broadcasted_iota(I32, (CHUNK, V7X_LANES), 1)
    first_head = lane < GLA_DK
    causal = (lax.broadcasted_iota(I32, (CHUNK, CHUNK), 0)
              >= lax.broadcasted_iota(I32, (CHUNK, CHUNK), 1))
    srow = lax.broadcasted_iota(I32, (2 * GLA_DV, V7X_LANES), 0)
    slane = lax.broadcasted_iota(I32, (2 * GLA_DV, V7X_LANES), 1)
    same_head = (srow < GLA_DV) == (slane < GLA_DK)

    states = [state_ref[0], state_ref[1]]
    for c in range(t // CHUNK):
        rows = slice(c * CHUNK, (c + 1) * CHUNK)
        g_last = gc[c * CHUNK + CHUNK - 1:c * CHUNK + CHUNK, :]
        k2 = k[rows] * jnp.exp(g_last - gc[rows])
        dec = jnp.exp(g_last)
        for p in range(GLA_HEADS // 2):
            lanes = slice(p * V7X_LANES, (p + 1) * V7X_LANES)
            vals = slice(p * 2 * GLA_DV, (p + 1) * 2 * GLA_DV)
            qe_p = qe[rows, lanes]
            ke_p = ke[rows, lanes].astype(BF16)
            v_p = v[rows, vals].astype(BF16)
            s_prev = states[p]
            outs = []
            for a in range(2):
                q_a = jnp.where(first_head if a == 0 else ~first_head, qe_p, 0.0)
                att = jnp.where(causal, _dot_nt(q_a.astype(BF16), ke_p), 0.0)
                outs.append(_dot(att.astype(BF16), v_p[:, a * GLA_DV:(a + 1) * GLA_DV]))
            inter = _dot_nt(qe_p.astype(BF16), s_prev.astype(BF16))
            oacc_ref[rows, vals] = jnp.concatenate(outs, axis=1) + inter
            d_state = _dot_tn(v_p, k2[:, lanes].astype(BF16))
            states[p] = s_prev * dec[:, lanes] + jnp.where(same_head, d_state, 0.0)
    state_ref[0] = states[0]
    state_ref[1] = states[1]

    o = oacc_ref[...]
    gla = []
    for hh in range(GLA_HEADS):
        cols = slice(hh * GLA_DV, (hh + 1) * GLA_DV)
        oh = o[:, cols]
        on = oh * lax.rsqrt(jnp.mean(oh * oh, axis=-1, keepdims=True) + EPS) * gnorm_ref[...]
        gla.append(on * jax.nn.silu(r[:, cols]))
    mix = jnp.concatenate([pool] + gla, axis=1).astype(BF16)
    o_ref[0] = x + _dot(mix, wout_ref[...])


def _mix(x, g, win, wpool, pscale, wgate, bgate, gnorm, wout):
    b, s, d = x.shape
    t = SEQ_TILE
    assert s % t == 0 and t % CHUNK == 0 and t >= POOL_TAIL
    i = jnp.arange(t)[:, None] + POOL_TAIL
    jj = jnp.arange(t + POOL_TAIL)[None, :]
    band = jnp.stack([((jj <= i) & (jj > i - w)) for w in POOL_WINDOWS]).astype(BF16)
    ii = jnp.arange(t)
    ltri = ((ii[:, None] // CHUNK == ii[None, :] // CHUNK)
            & (ii[None, :] <= ii[:, None])).astype(BF16)
    c2 = lambda bi, j: (0, 0)
    c3 = lambda bi, j: (0, 0, 0)
    return pl.pallas_call(
        _mix_kernel,
        grid=(b, s // t),
        in_specs=[pl.BlockSpec((1, t, d), lambda bi, j: (bi, j, 0)),
                  pl.BlockSpec((1, d), c2),
                  pl.BlockSpec((d, IN_COLS_PADDED), c2),
                  pl.BlockSpec((len(POOL_WINDOWS), POOL_GROUP_W, POOL_GROUP_W), c3),
                  pl.BlockSpec((1, POOL_W), c2),
                  pl.BlockSpec((len(POOL_WINDOWS), t, t + POOL_TAIL), c3),
                  pl.BlockSpec((V7X_LANES, GLA_KEY), c2),
                  pl.BlockSpec((1, GLA_KEY), c2),
                  pl.BlockSpec((1, GLA_DV), c2),
                  pl.BlockSpec((POOL_W + GLA_VAL, d), c2),
                  pl.BlockSpec((t, t), c2)],
        out_specs=pl.BlockSpec((1, t, d), lambda bi, j: (bi, j, 0)),
        out_shape=jax.ShapeDtypeStruct((b, s, d), F32),
        scratch_shapes=[pltpu.VMEM((POOL_TAIL, POOL_W), F32),
                        pltpu.VMEM((GLA_HEADS // 2, 2 * GLA_DV, V7X_LANES), F32),
                        pltpu.VMEM((t, GLA_VAL), F32)],
        compiler_params=_params("arbitrary", "arbitrary"),
        name="mix",
    )(x, g, win, wpool, pscale, band, wgate, bgate, gnorm, wout, ltri)


def _xattn_kernel(x_ref, g_ref, wq_ref, k_ref, v_ref, wo_ref, o_ref):
    x = x_ref[0]
    h = _rms(x, g_ref[...]).astype(BF16)
    q = _dot(h, wq_ref[...])
    heads = []
    for hh in range(XATTN_HEADS):
        cols = slice(hh * XATTN_HD, (hh + 1) * XATTN_HD)
        sc = _dot_nt(q[:, cols].astype(BF16), k_ref[0][:, cols])
        e = jnp.exp(sc - jnp.max(sc, axis=-1, keepdims=True))
        p = e / jnp.sum(e, axis=-1, keepdims=True)
        heads.append(_dot(p.astype(BF16), v_ref[0][:, cols]))
    o = jnp.concatenate(heads, axis=1).astype(BF16)
    o_ref[0] = x + _dot(o, wo_ref[...])


def _xattn(x, g, wq, kmem, vmem, wo):
    b, s, d = x.shape
    m = kmem.shape[1]
    t = SEQ_TILE
    c2 = lambda bi, j: (0, 0)
    return pl.pallas_call(
        _xattn_kernel,
        grid=(b, s // t),
        in_specs=[pl.BlockSpec((1, t, d), lambda bi, j: (bi, j, 0)),
                  pl.BlockSpec((1, d), c2),
                  pl.BlockSpec((d, d), c2),
                  pl.BlockSpec((1, m, d), lambda bi, j: (bi, 0, 0)),
                  pl.BlockSpec((1, m, d), lambda bi, j: (bi, 0, 0)),
                  pl.BlockSpec((d, d), c2)],
        out_specs=pl.BlockSpec((1, t, d), lambda bi, j: (bi, j, 0)),
        out_shape=jax.ShapeDtypeStruct((b, s, d), F32),
        compiler_params=_params("arbitrary", "arbitrary"),
        name="xattn",
    )(x, g, wq, kmem, vmem, wo)


def _route_kernel(x_ref, g_ref, wrt_hi_ref, wrt_lo_ref, br_ref, utri_ref,
                  h_ref, eid_ref, rank_ref, gate_ref, cnt_ref, carry_ref):
    t = x_ref.shape[0]

    @pl.when(pl.program_id(0) == 0)
    def _():
        carry_ref[...] = jnp.zeros_like(carry_ref)

    h = _rms(x_ref[...], g_ref[...])
    _store_token_tiles(h_ref, (), h)
    h_hi = h.astype(BF16)
    h_lo = (h - h_hi.astype(F32)).astype(BF16)
    logits = (_dot_nt(wrt_hi_ref[...], h_hi) + _dot_nt(wrt_hi_ref[...], h_lo)
              + _dot_nt(wrt_lo_ref[...], h_hi)) + br_ref[...]

    e_iota = lax.broadcasted_iota(I32, (N_EXPERTS, t), 0)
    e_iota_f = e_iota.astype(F32)
    work = logits
    vals, onehots = [], []
    for _ in range(TOP_K):
        m = jnp.max(work, axis=0, keepdims=True)
        idx = jnp.min(jnp.where(work == m, e_iota_f, float(N_EXPERTS)), axis=0, keepdims=True)
        hit = e_iota_f == idx
        vals.append(m)
        onehots.append(hit)
        work = jnp.where(hit, -jnp.inf, work)

    ex = [jnp.exp(vk - vals[0]) for vk in vals]
    den = ex[0] + ex[1] + ex[2] + ex[3]

    member = jnp.zeros((N_EXPERTS, t), F32)
    for hit in onehots:
        member = member + jnp.where(hit, 1.0, 0.0)
    carry = carry_ref[...]
    before = _dot(member.astype(BF16), utri_ref[...]) + carry[:, 0:1]

    row4 = lax.broadcasted_iota(I32, (TOP_K, t), 0)
    grow = lax.broadcasted_iota(I32, (V7X_LANES, t), 0)
    eid = jnp.zeros((TOP_K, t), I32)
    rank = jnp.zeros((TOP_K, t), I32)
    gates_t = jnp.zeros((V7X_LANES, t), F32)
    for kk in range(TOP_K):
        idx_k = jnp.sum(jnp.where(onehots[kk], e_iota_f, 0.0), axis=0, keepdims=True)
        rank_k = jnp.sum(jnp.where(onehots[kk], before, 0.0), axis=0, keepdims=True)
        eid = jnp.where(row4 == kk, idx_k.astype(I32), eid)
        rank = jnp.where(row4 == kk, rank_k.astype(I32), rank)
        gates_t = jnp.where(grow == kk, ex[kk] / den, gates_t)
    eid_ref[...] = eid
    rank_ref[...] = rank
    gate_ref[...] = gates_t.T

    new_carry = carry + jnp.sum(member, axis=1, keepdims=True)
    carry_ref[...] = new_carry
    cnt_ref[...] = new_carry


def _route(x2d, g, wrt_hi, wrt_lo, br):
    n, d = x2d.shape
    t = SEQ_TILE
    ii = jnp.arange(t)
    utri = (ii[:, None] < ii[None, :]).astype(BF16)
    c2 = lambda i: (0, 0)
    return pl.pallas_call(
        _route_kernel,
        grid=(n // t,),
        in_specs=[pl.BlockSpec((t, d), lambda i: (i, 0)),
                  pl.BlockSpec((1, d), c2),
                  pl.BlockSpec((N_EXPERTS, d), c2),
                  pl.BlockSpec((N_EXPERTS, d), c2),
                  pl.BlockSpec((N_EXPERTS, 1), c2),
                  pl.BlockSpec((t, t), c2)],
        out_specs=[pl.BlockSpec((t * TOKEN_TILE_ROWS, V7X_LANES), lambda i: (i, 0)),
                   pl.BlockSpec((TOP_K, t), lambda i: (0, i)),
                   pl.BlockSpec((TOP_K, t), lambda i: (0, i)),
                   pl.BlockSpec((t, V7X_LANES), lambda i: (i, 0)),
                   pl.BlockSpec((N_EXPERTS, V7X_LANES), c2)],
        out_shape=[jax.ShapeDtypeStruct((n * TOKEN_TILE_ROWS, V7X_LANES), F32),
                   jax.ShapeDtypeStruct((TOP_K, n), I32),
                   jax.ShapeDtypeStruct((TOP_K, n), I32),
                   jax.ShapeDtypeStruct((n, V7X_LANES), F32),
                   jax.ShapeDtypeStruct((N_EXPERTS, V7X_LANES), F32)],
        scratch_shapes=[pltpu.VMEM((N_EXPERTS, V7X_LANES), F32)],
        compiler_params=_params("arbitrary"),
        name="route",
    )(x2d, g, wrt_hi, wrt_lo, br, utri)


def _plan_kernel(start_ref, eid_ref, rank_ref, dest_ref):
    e = eid_ref[...]
    dest = rank_ref[...]
    for ee in range(N_EXPERTS):
        dest = dest + jnp.where(e == ee, start_ref[ee], 0)
    dest_ref[...] = dest


def _plan(starts, eid, rank):
    n = eid.shape[1]
    t = min(PLAN_TILE, n)
    blk = pl.BlockSpec((TOP_K, t), lambda i, s: (0, i))
    return pl.pallas_call(
        _plan_kernel,
        grid_spec=pltpu.PrefetchScalarGridSpec(
            num_scalar_prefetch=1, grid=(n // t,), in_specs=[blk, blk], out_specs=blk),
        out_shape=jax.ShapeDtypeStruct((TOP_K, n), I32),
        compiler_params=_params("arbitrary"),
        name="plan",
    )(starts, eid, rank)


def _invert_kernel(dest_ref, inv_ref):
    def body(i, carry):
        inv_ref[dest_ref[i]] = i
        return carry

    lax.fori_loop(0, dest_ref.shape[0], body, 0, unroll=8)


def _invert(dest_flat):
    return pl.pallas_call(
        _invert_kernel,
        grid_spec=pltpu.PrefetchScalarGridSpec(
            num_scalar_prefetch=1, grid=(1,), in_specs=[],
            out_specs=pl.BlockSpec(memory_space=pltpu.SMEM)),
        out_shape=jax.ShapeDtypeStruct(dest_flat.shape, I32),
        compiler_params=_params("arbitrary"),
        name="invert",
    )(dest_flat)


_ITEM_VALID, _ITEM_FIRST, _ITEM_NEW_EXPERT = 1, 2, 4


def _expert_kernel(blk_ref, lo_ref, hi_ref, flag_ref, exp_ref, inv_ref,
                   h_ref, wgu_ref, bgu_ref, wdn_ref, bdn_ref, out_ref,
                   wgu_bf, wdn_bf, xs_buf, y_buf, gsem, ssem, *, n_tok, n_blocks):
    tr = TOKEN_TILE_ROWS
    bm = xs_buf.shape[1] // tr
    i = pl.program_id(0)
    flags = flag_ref[i]
    blk = blk_ref[i]
    odd = (blk & 1) == 1

    def token_tile(ref, row):
        return ref.at[pl.ds(pl.multiple_of(row * tr, tr), tr), :]

    def gather(b, s):
        return [pltpu.make_async_copy(
            token_tile(h_ref, inv_ref[b * bm + r] & (n_tok - 1)),
            xs_buf.at[s, pl.ds(r * tr, tr), :], gsem.at[s]) for r in range(bm)]

    def scatter(b, s):
        return [pltpu.make_async_copy(
            y_buf.at[s, pl.ds(r * tr, tr), :],
            token_tile(out_ref, inv_ref[b * bm + r]), ssem.at[s]) for r in range(bm)]

    def wait_gather(s):
        pltpu.make_async_copy(h_ref.at[pl.ds(0, bm * tr), :], xs_buf.at[s], gsem.at[s]).wait()

    def wait_scatter(s):
        pltpu.make_async_copy(y_buf.at[s], out_ref.at[pl.ds(0, bm * tr), :], ssem.at[s]).wait()

    @pl.when(i == 0)
    def _():
        for c in gather(0, 0):
            c.start()

    @pl.when((flags & _ITEM_NEW_EXPERT) != 0)
    def _():
        wgu_bf[...] = wgu_ref[0].astype(BF16)
        wdn_bf[...] = wdn_ref[0].astype(BF16)

    dyn_slot = (blk & 1,)
    half = bm // 2

    def load_x(r0, n_rows):
        return _load_token_tiles(xs_buf, dyn_slot, n_rows, r0).astype(BF16)

    def ffn_rows(x, r0, merge):
        n_rows = x.shape[0]
        gu = _dot(x, wgu_bf[...]) + bgu_ref[0]
        gate = jnp.minimum(gu[:, :D_FF], SWIGLU_LIMIT)
        up = jnp.clip(gu[:, D_FF:], -SWIGLU_LIMIT, SWIGLU_LIMIT)
        act = gate * jax.nn.sigmoid(SWIGLU_ALPHA * gate) * (up + 1.0)
        yb = _dot(act.astype(BF16), wdn_bf[...]) + bdn_ref[0]
        row = r0 + lax.broadcasted_iota(I32, yb.shape, 0)
        mine = (row >= lo_ref[i]) & (row < hi_ref[i])
        other = _load_token_tiles(y_buf, dyn_slot, n_rows, r0) if merge else 0.0
        return jnp.where(mine, yb, other)

    def first_item(s, has_prev):
        wait_gather(s)
        x_top = load_x(0, half)
        for c in gather(jnp.minimum(blk + 1, n_blocks - 1), 1 - s):
            c.start()
        _store_token_tiles(y_buf, dyn_slot, ffn_rows(x_top, 0, False), 0)
        if has_prev:
            for c in scatter(blk - 1, 1 - s):
                c.start()
        _store_token_tiles(y_buf, dyn_slot, ffn_rows(load_x(half, half), half, False), half)

    def later_item(s):
        _store_token_tiles(y_buf, dyn_slot, ffn_rows(load_x(0, bm), 0, True), 0)

    first = (flags & (_ITEM_VALID | _ITEM_FIRST)) == (_ITEM_VALID | _ITEM_FIRST)
    later = (flags & (_ITEM_VALID | _ITEM_FIRST)) == _ITEM_VALID

    @pl.when(first & (blk == 0))
    def _():
        first_item(0, False)

    @pl.when(first & odd & (blk >= 3))
    def _():
        wait_scatter(1)

    @pl.when(first & odd)
    def _():
        first_item(1, True)

    @pl.when(first & jnp.logical_not(odd) & (blk >= 2))
    def _():
        wait_scatter(0)
        first_item(0, True)

    @pl.when(later & jnp.logical_not(odd))
    def _():
        later_item(0)

    @pl.when(later & odd)
    def _():
        later_item(1)

    @pl.when(i == pl.num_programs(0) - 1)
    def _():
        last = (n_blocks - 1) & 1
        for c in scatter(n_blocks - 1, last):
            c.start()
        wait_scatter(1 - last)
        wait_scatter(last)
        wait_gather(1 - last)


def _expert_items(counts, n_rows):
    bm = EXPERT_ROWS
    n_items = n_rows // bm + N_EXPERTS - 1
    ends = jnp.cumsum(counts)
    starts = ends - counts
    blocks_of = jnp.where(counts > 0, (ends - 1) // bm - starts // bm + 1, 0)
    item_end = jnp.cumsum(blocks_of)
    item_start = item_end - blocks_of
    total = item_end[-1]
    idx = jnp.arange(n_items, dtype=I32)
    valid = idx < total
    idc = jnp.minimum(idx, jnp.maximum(total - 1, 0))
    owner = ((item_start[None, :] <= idc[:, None]) & (idc[:, None] < item_end[None, :])).astype(I32)
    pick = lambda v: jnp.sum(owner * v[None, :], axis=1)
    e = pick(jnp.arange(N_EXPERTS, dtype=I32))
    blk = pick(starts // bm) + idc - pick(item_start)
    lo = jnp.clip(pick(starts) - blk * bm, 0, bm)
    hi = jnp.clip(pick(ends) - blk * bm, 0, bm)
    prev_blk = jnp.concatenate([jnp.full((1,), -1, I32), blk[:-1]])
    prev_e = jnp.concatenate([jnp.full((1,), -1, I32), e[:-1]])
    flags = jnp.where(valid,
                      _ITEM_VALID
                      + jnp.where(blk != prev_blk, _ITEM_FIRST, 0)
                      + jnp.where(e != prev_e, _ITEM_NEW_EXPERT, 0), 0)
    as_i32 = lambda v: v.astype(I32)
    return as_i32(blk), as_i32(lo), as_i32(hi), as_i32(flags), as_i32(e)


def _experts(items, inv, h, wgu, bgu, wdn, bdn):
    tr, d = TOKEN_TILE_ROWS, D_MODEL
    n = h.shape[0] // tr
    bm = EXPERT_ROWS
    n_rows = inv.shape[0]
    n_blocks = n_rows // bm
    assert n_rows % bm == 0 and n_blocks >= 2 and n & (n - 1) == 0
    blk, lo, hi, flags, e = items
    by_expert = lambda i, blk, lo, hi, fl, e, inv: (e[i], 0, 0)
    return pl.pallas_call(
        functools.partial(_expert_kernel, n_tok=n, n_blocks=n_blocks),
        grid_spec=pltpu.PrefetchScalarGridSpec(
            num_scalar_prefetch=6, grid=(blk.shape[0],),
            in_specs=[pl.BlockSpec(memory_space=pl.ANY),
                      pl.BlockSpec((1, d, 2 * D_FF), by_expert),
                      pl.BlockSpec((1, 1, 2 * D_FF), by_expert),
                      pl.BlockSpec((1, D_FF, d), by_expert),
                      pl.BlockSpec((1, 1, d), by_expert)],
            out_specs=pl.BlockSpec(memory_space=pl.ANY),
            scratch_shapes=[pltpu.VMEM((d, 2 * D_FF), BF16),
                            pltpu.VMEM((D_FF, d), BF16),
                            pltpu.VMEM((2, bm * tr, V7X_LANES), F32),
                            pltpu.VMEM((2, bm * tr, V7X_LANES), F32),
                            pltpu.SemaphoreType.DMA((2,)),
                            pltpu.SemaphoreType.DMA((2,))]),
        out_shape=jax.ShapeDtypeStruct((n_rows * tr, V7X_LANES), F32),
        compiler_params=_params("arbitrary"),
        name="experts",
    )(blk, lo, hi, flags, e, inv, h, wgu, bgu, wdn, bdn)


def _combine_kernel(x_ref, gate_ref, g_ref, y_ref, o_ref):
    gates = gate_ref[...]
    moe = jnp.zeros(x_ref.shape, F32)
    for kk in range(TOP_K):
        moe = moe + _load_token_tiles(y_ref, (kk,), x_ref.shape[0]) * gates[:, kk:kk + 1]
    o_ref[...] = _rms(x_ref[...] + moe, g_ref[...])


def _combine(x2d, gates, g, y):
    n, d = x2d.shape
    t = min(COMBINE_TILE, n)
    return pl.pallas_call(
        _combine_kernel,
        grid=(n // t,),
        in_specs=[pl.BlockSpec((t, d), lambda i: (i, 0)),
                  pl.BlockSpec((t, V7X_LANES), lambda i: (i, 0)),
                  pl.BlockSpec((1, d), lambda i: (0, 0)),
                  pl.BlockSpec((TOP_K, t * TOKEN_TILE_ROWS, V7X_LANES), lambda i: (0, i, 0))],
        out_specs=pl.BlockSpec((t, d), lambda i: (i, 0)),
        out_shape=jax.ShapeDtypeStruct((n, d), F32),
        compiler_params=_params("arbitrary"),
        name="combine",
    )(x2d, gates, g, y.reshape(TOP_K, n * TOKEN_TILE_ROWS, V7X_LANES))


def _layer(x, mem, norm_mix_g, w_in, w_pool, pool_scale, w_gate_up, b_gate_up, gla_norm_g, w_out,
           norm_xattn_g, norm_mem_g, w_xq, w_xk, w_xv, w_xo,
           norm_moe_g, w_router, b_router, w_gu, b_gu, w_dn, b_dn, out_g):
    b, s, d = x.shape
    n = b * s
    row = lambda a: a.reshape(1, -1)

    s0, s1, s2, s3, s4 = (POOL_W, POOL_W + GLA_KEY, POOL_W + 2 * GLA_KEY,
                          POOL_W + 2 * GLA_KEY + GLA_VAL, POOL_W + 2 * GLA_KEY + GLA_VAL + GATE_RANK)
    win = jnp.concatenate(
        [w_in[:, :s3], w_in[:, s4:], w_in[:, s3:s4],
         jnp.zeros((d, V7X_LANES - GATE_RANK), w_in.dtype)], axis=1).astype(BF16)
    wgate = jnp.concatenate(
        [w_gate_up, jnp.zeros((V7X_LANES - GATE_RANK, GLA_KEY), w_gate_up.dtype)], axis=0).astype(BF16)
    wrt = w_router.T
    wrt_hi = wrt.astype(BF16)
    wrt_lo = (wrt - wrt_hi.astype(F32)).astype(BF16)

    kmem, vmem = _memkv(mem, row(norm_mem_g), w_xk.astype(BF16), w_xv.astype(BF16))
    x1 = _mix(x, row(norm_mix_g), win, w_pool.astype(BF16), row(pool_scale), wgate,
              row(b_gate_up), row(gla_norm_g), w_out.astype(BF16))
    x2 = _xattn(x1, row(norm_xattn_g), w_xq.astype(BF16), kmem, vmem, w_xo.astype(BF16))
    x2 = x2.reshape(n, d)

    h, eid, rank, gates, cnt = _route(x2, row(norm_moe_g), wrt_hi, wrt_lo, b_router.reshape(-1, 1))
    counts = cnt[:, 0].astype(I32)
    starts = (jnp.cumsum(counts) - counts).astype(I32)
    dest = _plan(starts, eid, rank).reshape(TOP_K * n)
    items = _expert_items(counts, TOP_K * n)
    y = _experts(items, _invert(dest), h, w_gu, b_gu.reshape(N_EXPERTS, 1, -1),
                 w_dn, b_dn.reshape(N_EXPERTS, 1, -1))
    out = _combine(x2, gates, row(out_g), y)
    return out.reshape(b, s, d)


def kernel(x, mem, norm_mix_g, w_in, w_pool, pool_scale, w_gate_up, b_gate_up, gla_norm_g, w_out,
           norm_xattn_g, norm_mem_g, w_xq, w_xk, w_xv, w_xo, norm_moe_g, w_router, b_router,
           w_gu, b_gu, w_dn, b_dn, norm_final_g):
    depth = norm_mix_g.shape[0]
    assert depth == 1, "the final rmsnorm is fused into the (single) layer's combine stage"
    return _layer(x, mem, norm_mix_g[0], w_in[0], w_pool[0], pool_scale[0], w_gate_up[0],
                  b_gate_up[0], gla_norm_g[0], w_out[0], norm_xattn_g[0], norm_mem_g[0],
                  w_xq[0], w_xk[0], w_xv[0], w_xo[0], norm_moe_g[0], w_router[0], b_router[0],
                  w_gu[0], b_gu[0], w_dn[0], b_dn[0], norm_final_g)
```

```python
import functools

import jax
import jax.numpy as jnp
from jax import lax
from jax.experimental import pallas as pl
from jax.experimental.pallas import tpu as pltpu

F32 = jnp.float32
BF16 = jnp.bfloat16
I32 = jnp.int32

V7X_LANES = 128
V7X_VMEM_LIMIT_BYTES = 56 * 1024 * 1024

D_MODEL = 1024
POOL_WINDOWS = (2, 4, 8, 16)
POOL_GROUP_W = 128
POOL_W = len(POOL_WINDOWS) * POOL_GROUP_W
GLA_HEADS = 4
GLA_DK = 64
GLA_DV = 128
GLA_KEY = GLA_HEADS * GLA_DK
GLA_VAL = GLA_HEADS * GLA_DV
GATE_RANK = 16
GATE_NORMALIZER = 16.0
CHUNK = 64
XATTN_HEADS = 4
XATTN_HD = D_MODEL // XATTN_HEADS
N_EXPERTS = 32
TOP_K = 4
D_FF = D_MODEL
SWIGLU_LIMIT = 7.0
SWIGLU_ALPHA = 1.702
EPS = 1e-6

_U0 = 0
_Q0 = _U0 + POOL_W
_K0 = _Q0 + GLA_KEY
_V0 = _K0 + GLA_KEY
_R0 = _V0 + GLA_VAL
_G0 = _R0 + GLA_VAL
IN_COLS_PADDED = _G0 + V7X_LANES

SEQ_TILE = 512
POOL_TAIL = 128
PLAN_TILE = 2048
EXPERT_ROWS = 256
COMBINE_TILE = 256


def _rms(x, g):
    ms = jnp.mean(x * x, axis=-1, keepdims=True)
    return x * lax.rsqrt(ms + EPS) * g


def _dot(a, b):
    return jnp.dot(a, b, preferred_element_type=F32)


def _dot_nt(a, b):
    return lax.dot_general(a, b, (((1,), (1,)), ((), ())), preferred_element_type=F32)


def _dot_tn(a, b):
    return lax.dot_general(a, b, (((0,), (0,)), ((), ())), preferred_element_type=F32)


TOKEN_TILE_ROWS = D_MODEL // V7X_LANES


def _load_token_tiles(ref, lead, n_rows, first_row=0):
    start = first_row * TOKEN_TILE_ROWS
    chunks = [ref[lead + (pl.ds(start + c, n_rows, stride=TOKEN_TILE_ROWS), slice(None))]
              for c in range(TOKEN_TILE_ROWS)]
    return jnp.concatenate(chunks, axis=1)


def _store_token_tiles(ref, lead, val, first_row=0):
    start = first_row * TOKEN_TILE_ROWS
    for c in range(TOKEN_TILE_ROWS):
        ref[lead + (pl.ds(start + c, val.shape[0], stride=TOKEN_TILE_ROWS), slice(None))] = (
            val[:, c * V7X_LANES:(c + 1) * V7X_LANES])


def _params(*semantics):
    return pltpu.CompilerParams(dimension_semantics=semantics,
                                vmem_limit_bytes=V7X_VMEM_LIMIT_BYTES)


def _memkv_kernel(mem_ref, g_ref, wk_ref, wv_ref, k_ref, v_ref):
    hm = _rms(mem_ref[0], g_ref[...]).astype(BF16)
    k = _dot(hm, wk_ref[...])
    v = _dot(hm, wv_ref[...])
    k_ref[0] = (k * (XATTN_HD ** -0.5)).astype(BF16)
    v_ref[0] = v.astype(BF16)


def _memkv(mem, g, wk, wv):
    b, m, d = mem.shape
    const = lambda i: (0, 0)
    return pl.pallas_call(
        _memkv_kernel,
        grid=(b,),
        in_specs=[pl.BlockSpec((1, m, d), lambda i: (i, 0, 0)),
                  pl.BlockSpec((1, d), const),
                  pl.BlockSpec((d, d), const),
                  pl.BlockSpec((d, d), const)],
        out_specs=[pl.BlockSpec((1, m, d), lambda i: (i, 0, 0)),
                   pl.BlockSpec((1, m, d), lambda i: (i, 0, 0))],
        out_shape=[jax.ShapeDtypeStruct((b, m, d), BF16)] * 2,
        compiler_params=_params("arbitrary"),
        name="memkv",
    )(mem, g, wk, wv)


def _mix_kernel(x_ref, g_ref, win_ref, wpool_ref, pscale_ref, band_ref, wgate_ref, bgate_ref,
                gnorm_ref, wout_ref, ltri_ref, o_ref, uprev_ref, state_ref, oacc_ref):
    t = x_ref.shape[1]
    j = pl.program_id(1)

    @pl.when(j == 0)
    def _():
        uprev_ref[...] = jnp.zeros_like(uprev_ref)
        state_ref[...] = jnp.zeros_like(state_ref)

    x = x_ref[0]
    h = _rms(x, g_ref[...]).astype(BF16)
    z = _dot(h, win_ref[...])
    u = z[:, _U0:_U0 + POOL_W]
    q = z[:, _Q0:_Q0 + GLA_KEY]
    k = z[:, _K0:_K0 + GLA_KEY]
    v = z[:, _V0:_V0 + GLA_VAL]
    r = z[:, _R0:_R0 + GLA_VAL]
    glr = z[:, _G0:_G0 + V7X_LANES]

    u_ext = jnp.concatenate([uprev_ref[...], u], axis=0).astype(BF16)
    uprev_ref[...] = u[t - POOL_TAIL:, :]
    row = lax.broadcasted_iota(I32, (t, POOL_GROUP_W), 0)
    pos = (j * t + row + 1).astype(F32)
    pooled = []
    for gi, w in enumerate(POOL_WINDOWS):
        cols = slice(gi * POOL_GROUP_W, (gi + 1) * POOL_GROUP_W)
        wsum = _dot(band_ref[gi], u_ext[:, cols])
        p = wsum / jnp.minimum(pos, float(w)) - u[:, cols]
        pooled.append(_dot(p.astype(BF16), wpool_ref[gi]))
    pool = jnp.concatenate(pooled, axis=1) * pscale_ref[...]

    gp = _dot(glr.astype(BF16), wgate_ref[...]) + bgate_ref[...]
    g = jax.nn.log_sigmoid(gp) / GATE_NORMALIZER
    g_hi = g.astype(BF16)
    g_lo = (g - g_hi.astype(F32)).astype(BF16)
    gc = _dot(ltri_ref[...], g_hi) + _dot(ltri_ref[...], g_lo)
    qe = q * (GLA_DK ** -0.5) * jnp.exp(gc)
    ke = k * jnp.exp(-gc)

    lane = lax.broadcasted_iota(I32, (CHUNK, V7X_LANES), 1)
    first_head = lane < GLA_DK
    causal = (lax.broadcasted_iota(I32, (CHUNK, CHUNK), 0)
              >= lax.broadcasted_iota(I32, (CHUNK, CHUNK), 1))
    srow = lax.broadcasted_iota(I32, (2 * GLA_DV, V7X_LANES), 0)
    slane = lax.broadcasted_iota(I32, (2 * GLA_DV, V7X_LANES), 1)
    same_head = (srow < GLA_DV) == (slane < GLA_DK)

    states = [state_ref[0], state_ref[1]]
    for c in range(t // CHUNK):
        rows = slice(c * CHUNK, (c + 1) * CHUNK)
        g_last = gc[c * CHUNK + CHUNK - 1:c * CHUNK + CHUNK, :]
        k2 = k[rows] * jnp.exp(g_last - gc[rows])
        dec = jnp.exp(g_last)
        for p in range(GLA_HEADS // 2):
            lanes = slice(p * V7X_LANES, (p + 1) * V7X_LANES)
            vals = slice(p * 2 * GLA_DV, (p + 1) * 2 * GLA_DV)
            qe_p = qe[rows, lanes]
            ke_p = ke[rows, lanes].astype(BF16)
            v_p = v[rows, vals].astype(BF16)
            s_prev = states[p]
            outs = []
            for a in range(2):
                q_a = jnp.where(first_head if a == 0 else ~first_head, qe_p, 0.0)
                att = jnp.where(causal, _dot_nt(q_a.astype(BF16), ke_p), 0.0)
                outs.append(_dot(att.astype(BF16), v_p[:, a * GLA_DV:(a + 1) * GLA_DV]))
            inter = _dot_nt(qe_p.astype(BF16), s_prev.astype(BF16))
            oacc_ref[rows, vals] = jnp.concatenate(outs, axis=1) + inter
            d_state = _dot_tn(v_p, k2[:, lanes].astype(BF16))
            states[p] = s_prev * dec[:, lanes] + jnp.where(same_head, d_state, 0.0)
    state_ref[0] = states[0]
    state_ref[1] = states[1]

    o = oacc_ref[...]
    gla = []
    for hh in range(GLA_HEADS):
        cols = slice(hh * GLA_DV, (hh + 1) * GLA_DV)
        oh = o[:, cols]
        on = oh * lax.rsqrt(jnp.mean(oh * oh, axis=-1, keepdims=True) + EPS) * gnorm_ref[...]
        gla.append(on * jax.nn.silu(r[:, cols]))
    mix = jnp.concatenate([pool] + gla, axis=1).astype(BF16)
    o_ref[0] = x + _dot(mix, wout_ref[...])


def _mix(x, g, win, wpool, pscale, wgate, bgate, gnorm, wout):
    b, s, d = x.shape
    t = SEQ_TILE
    assert s % t == 0 and t % CHUNK == 0 and t >= POOL_TAIL
    i = jnp.arange(t)[:, None] + POOL_TAIL
    jj = jnp.arange(t + POOL_TAIL)[None, :]
    band = jnp.stack([((jj <= i) & (jj > i - w)) for w in POOL_WINDOWS]).astype(BF16)
    ii = jnp.arange(t)
    ltri = ((ii[:, None] // CHUNK == ii[None, :] // CHUNK)
            & (ii[None, :] <= ii[:, None])).astype(BF16)
    c2 = lambda bi, j: (0, 0)
    c3 = lambda bi, j: (0, 0, 0)
    return pl.pallas_call(
        _mix_kernel,
        grid=(b, s // t),
        in_specs=[pl.BlockSpec((1, t, d), lambda bi, j: (bi, j, 0)),
                  pl.BlockSpec((1, d), c2),
                  pl.BlockSpec((d, IN_COLS_PADDED), c2),
                  pl.BlockSpec((len(POOL_WINDOWS), POOL_GROUP_W, POOL_GROUP_W), c3),
                  pl.BlockSpec((1, POOL_W), c2),
                  pl.BlockSpec((len(POOL_WINDOWS), t, t + POOL_TAIL), c3),
                  pl.BlockSpec((V7X_LANES, GLA_KEY), c2),
                  pl.BlockSpec((1, GLA_KEY), c2),
                  pl.BlockSpec((1, GLA_DV), c2),
                  pl.BlockSpec((POOL_W + GLA_VAL, d), c2),
                  pl.BlockSpec((t, t), c2)],
        out_specs=pl.BlockSpec((1, t, d), lambda bi, j: (bi, j, 0)),
        out_shape=jax.ShapeDtypeStruct((b, s, d), F32),
        scratch_shapes=[pltpu.VMEM((POOL_TAIL, POOL_W), F32),
                        pltpu.VMEM((GLA_HEADS // 2, 2 * GLA_DV, V7X_LANES), F32),
                        pltpu.VMEM((t, GLA_VAL), F32)],
        compiler_params=_params("arbitrary", "arbitrary"),
        name="mix",
    )(x, g, win, wpool, pscale, band, wgate, bgate, gnorm, wout, ltri)


def _xattn_kernel(x_ref, g_ref, wq_ref, k_ref, v_ref, wo_ref, o_ref):
    x = x_ref[0]
    h = _rms(x, g_ref[...]).astype(BF16)
    q = _dot(h, wq_ref[...])
    heads = []
    for hh in range(XATTN_HEADS):
        cols = slice(hh * XATTN_HD, (hh + 1) * XATTN_HD)
        sc = _dot_nt(q[:, cols].astype(BF16), k_ref[0][:, cols])
        e = jnp.exp(sc - jnp.max(sc, axis=-1, keepdims=True))
        p = e / jnp.sum(e, axis=-1, keepdims=True)
        heads.append(_dot(p.astype(BF16), v_ref[0][:, cols]))
    o = jnp.concatenate(heads, axis=1).astype(BF16)
    o_ref[0] = x + _dot(o, wo_ref[...])


def _xattn(x, g, wq, kmem, vmem, wo):
    b, s, d = x.shape
    m = kmem.shape[1]
    t = SEQ_TILE
    c2 = lambda bi, j: (0, 0)
    return pl.pallas_call(
        _xattn_kernel,
        grid=(b, s // t),
        in_specs=[pl.BlockSpec((1, t, d), lambda bi, j: (bi, j, 0)),
                  pl.BlockSpec((1, d), c2),
                  pl.BlockSpec((d, d), c2),
                  pl.BlockSpec((1, m, d), lambda bi, j: (bi, 0, 0)),
                  pl.BlockSpec((1, m, d), lambda bi, j: (bi, 0, 0)),
                  pl.BlockSpec((d, d), c2)],
        out_specs=pl.BlockSpec((1, t, d), lambda bi, j: (bi, j, 0)),
        out_shape=jax.ShapeDtypeStruct((b, s, d), F32),
        compiler_params=_params("arbitrary", "arbitrary"),
        name="xattn",
    )(x, g, wq, kmem, vmem, wo)


def _route_kernel(x_ref, g_ref, wrt_hi_ref, wrt_lo_ref, br_ref, utri_ref,
                  h_ref, eid_ref, rank_ref, gate_ref, cnt_ref, carry_ref):
    t = x_ref.shape[0]

    @pl.when(pl.program_id(0) == 0)
    def _():
        carry_ref[...] = jnp.zeros_like(carry_ref)

    h = _rms(x_ref[...], g_ref[...])
    _store_token_tiles(h_ref, (), h)
    h_hi = h.astype(BF16)
    h_lo = (h - h_hi.astype(F32)).astype(BF16)
    logits = (_dot_nt(wrt_hi_ref[...], h_hi) + _dot_nt(wrt_hi_ref[...], h_lo)
              + _dot_nt(wrt_lo_ref[...], h_hi)) + br_ref[...]

    e_iota = lax.broadcasted_iota(I32, (N_EXPERTS, t), 0)
    e_iota_f = e_iota.astype(F32)
    work = logits
    vals, onehots = [], []
    for _ in range(TOP_K):
        m = jnp.max(work, axis=0, keepdims=True)
        idx = jnp.min(jnp.where(work == m, e_iota_f, float(N_EXPERTS)), axis=0, keepdims=True)
        hit = e_iota_f == idx
        vals.append(m)
        onehots.append(hit)
        work = jnp.where(hit, -jnp.inf, work)

    ex = [jnp.exp(vk - vals[0]) for vk in vals]
    den = ex[0] + ex[1] + ex[2] + ex[3]

    member = jnp.zeros((N_EXPERTS, t), F32)
    for hit in onehots:
        member = member + jnp.where(hit, 1.0, 0.0)
    carry = carry_ref[...]
    before = _dot(member.astype(BF16), utri_ref[...]) + carry[:, 0:1]

    row4 = lax.broadcasted_iota(I32, (TOP_K, t), 0)
    grow = lax.broadcasted_iota(I32, (V7X_LANES, t), 0)
    eid = jnp.zeros((TOP_K, t), I32)
    rank = jnp.zeros((TOP_K, t), I32)
    gates_t = jnp.zeros((V7X_LANES, t), F32)
    for kk in range(TOP_K):
        idx_k = jnp.sum(jnp.where(onehots[kk], e_iota_f, 0.0), axis=0, keepdims=True)
        rank_k = jnp.sum(jnp.where(onehots[kk], before, 0.0), axis=0, keepdims=True)
        eid = jnp.where(row4 == kk, idx_k.astype(I32), eid)
        rank = jnp.where(row4 == kk, rank_k.astype(I32), rank)
        gates_t = jnp.where(grow == kk, ex[kk] / den, gates_t)
    eid_ref[...] = eid
    rank_ref[...] = rank
    gate_ref[...] = gates_t.T

    new_carry = carry + jnp.sum(member, axis=1, keepdims=True)
    carry_ref[...] = new_carry
    cnt_ref[...] = new_carry


def _route(x2d, g, wrt_hi, wrt_lo, br):
    n, d = x2d.shape
    t = SEQ_TILE
    ii = jnp.arange(t)
    utri = (ii[:, None] < ii[None, :]).astype(BF16)
    c2 = lambda i: (0, 0)
    return pl.pallas_call(
        _route_kernel,
        grid=(n // t,),
        in_specs=[pl.BlockSpec((t, d), lambda i: (i, 0)),
                  pl.BlockSpec((1, d), c2),
                  pl.BlockSpec((N_EXPERTS, d), c2),
                  pl.BlockSpec((N_EXPERTS, d), c2),
                  pl.BlockSpec((N_EXPERTS, 1), c2),
                  pl.BlockSpec((t, t), c2)],
        out_specs=[pl.BlockSpec((t * TOKEN_TILE_ROWS, V7X_LANES), lambda i: (i, 0)),
                   pl.BlockSpec((TOP_K, t), lambda i: (0, i)),
                   pl.BlockSpec((TOP_K, t), lambda i: (0, i)),
                   pl.BlockSpec((t, V7X_LANES), lambda i: (i, 0)),
                   pl.BlockSpec((N_EXPERTS, V7X_LANES), c2)],
        out_shape=[jax.ShapeDtypeStruct((n * TOKEN_TILE_ROWS, V7X_LANES), F32),
                   jax.ShapeDtypeStruct((TOP_K, n), I32),
                   jax.ShapeDtypeStruct((TOP_K, n), I32),
                   jax.ShapeDtypeStruct((n, V7X_LANES), F32),
                   jax.ShapeDtypeStruct((N_EXPERTS, V7X_LANES), F32)],
        scratch_shapes=[pltpu.VMEM((N_EXPERTS, V7X_LANES), F32)],
        compiler_params=_params("arbitrary"),
        name="route",
    )(x2d, g, wrt_hi, wrt_lo, br, utri)


def _plan_kernel(start_ref, eid_ref, rank_ref, dest_ref):
    e = eid_ref[...]
    dest = rank_ref[...]
    for ee in range(N_EXPERTS):
        dest = dest + jnp.where(e == ee, start_ref[ee], 0)
    dest_ref[...] = dest


def _plan(starts, eid, rank):
    n = eid.shape[1]
    t = min(PLAN_TILE, n)
    blk = pl.BlockSpec((TOP_K, t), lambda i, s: (0, i))
    return pl.pallas_call(
        _plan_kernel,
        grid_spec=pltpu.PrefetchScalarGridSpec(
            num_scalar_prefetch=1, grid=(n // t,), in_specs=[blk, blk], out_specs=blk),
        out_shape=jax.ShapeDtypeStruct((TOP_K, n), I32),
        compiler_params=_params("arbitrary"),
        name="plan",
    )(starts, eid, rank)


def _invert_kernel(dest_ref, inv_ref):
    def body(i, carry):
        inv_ref[dest_ref[i]] = i
        return carry

    lax.fori_loop(0, dest_ref.shape[0], body, 0, unroll=32)


def _invert(dest_flat):
    return pl.pallas_call(
        _invert_kernel,
        grid_spec=pltpu.PrefetchScalarGridSpec(
            num_scalar_prefetch=1, grid=(1,), in_specs=[],
            out_specs=pl.BlockSpec(memory_space=pltpu.SMEM)),
        out_shape=jax.ShapeDtypeStruct(dest_flat.shape, I32),
        compiler_params=_params("arbitrary"),
        name="invert",
    )(dest_flat)


_ITEM_VALID, _ITEM_FIRST, _ITEM_NEW_EXPERT = 1, 2, 4


def _expert_kernel(blk_ref, lo_ref, hi_ref, flag_ref, exp_ref, inv_ref,
                   h_ref, wgu_ref, bgu_ref, wdn_ref, bdn_ref, out_ref,
                   wgu_bf, wdn_bf, xs_buf, y_buf, gsem, ssem, *, n_tok, n_blocks):
    tr = TOKEN_TILE_ROWS
    bm = xs_buf.shape[1] // tr
    i = pl.program_id(0)
    flags = flag_ref[i]
    blk = blk_ref[i]
    odd = (blk & 1) == 1

    def token_tile(ref, row):
        return ref.at[pl.ds(pl.multiple_of(row * tr, tr), tr), :]

    def gather(b, s):
        return [pltpu.make_async_copy(
            token_tile(h_ref, inv_ref[b * bm + r] & (n_tok - 1)),
            xs_buf.at[s, pl.ds(r * tr, tr), :], gsem.at[s]) for r in range(bm)]

    def scatter(b, s):
        return [pltpu.make_async_copy(
            y_buf.at[s, pl.ds(r * tr, tr), :],
            token_tile(out_ref, inv_ref[b * bm + r]), ssem.at[s]) for r in range(bm)]

    def wait_gather(s):
        pltpu.make_async_copy(h_ref.at[pl.ds(0, bm * tr), :], xs_buf.at[s], gsem.at[s]).wait()

    def wait_scatter(s):
        pltpu.make_async_copy(y_buf.at[s], out_ref.at[pl.ds(0, bm * tr), :], ssem.at[s]).wait()

    def start_all(copies):
        for r, c in enumerate(copies):
            c.start(priority=r % 2)

    @pl.when(i == 0)
    def _():
        start_all(gather(0, 0))

    @pl.when((flags & _ITEM_NEW_EXPERT) != 0)
    def _():
        wgu_bf[...] = wgu_ref[0].astype(BF16)
        wdn_bf[...] = wdn_ref[0].astype(BF16)

    dyn_slot = (blk & 1,)
    half = bm // 2

    def load_x(r0, n_rows):
        return _load_token_tiles(xs_buf, dyn_slot, n_rows, r0).astype(BF16)

    def ffn_rows(x, r0, merge):
        n_rows = x.shape[0]
        gu = _dot(x, wgu_bf[...]) + bgu_ref[0]
        gate = jnp.minimum(gu[:, :D_FF], SWIGLU_LIMIT)
        up = jnp.clip(gu[:, D_FF:], -SWIGLU_LIMIT, SWIGLU_LIMIT)
        act = gate * jax.nn.sigmoid(SWIGLU_ALPHA * gate) * (up + 1.0)
        yb = _dot(act.astype(BF16), wdn_bf[...]) + bdn_ref[0]
        row = r0 + lax.broadcasted_iota(I32, yb.shape, 0)
        mine = (row >= lo_ref[i]) & (row < hi_ref[i])
        other = _load_token_tiles(y_buf, dyn_slot, n_rows, r0) if merge else 0.0
        return jnp.where(mine, yb, other)

    def first_item(s, has_prev):
        wait_gather(s)
        x_top = load_x(0, half)
        start_all(gather(jnp.minimum(blk + 1, n_blocks - 1), 1 - s))
        _store_token_tiles(y_buf, dyn_slot, ffn_rows(x_top, 0, False), 0)
        if has_prev:
            start_all(scatter(blk - 1, 1 - s))
        _store_token_tiles(y_buf, dyn_slot, ffn_rows(load_x(half, half), half, False), half)

    def later_item(s):
        _store_token_tiles(y_buf, dyn_slot, ffn_rows(load_x(0, bm), 0, True), 0)

    first = (flags & (_ITEM_VALID | _ITEM_FIRST)) == (_ITEM_VALID | _ITEM_FIRST)
    later = (flags & (_ITEM_VALID | _ITEM_FIRST)) == _ITEM_VALID

    @pl.when(first & (blk == 0))
    def _():
        first_item(0, False)

    @pl.when(first & odd & (blk >= 3))
    def _():
        wait_scatter(1)

    @pl.when(first & odd)
    def _():
        first_item(1, True)

    @pl.when(first & jnp.logical_not(odd) & (blk >= 2))
    def _():
        wait_scatter(0)
        first_item(0, True)

    @pl.when(later & jnp.logical_not(odd))
    def _():
        later_item(0)

    @pl.when(later & odd)
    def _():
        later_item(1)

    @pl.when(i == pl.num_programs(0) - 1)
    def _():
        last = (n_blocks - 1) & 1
        start_all(scatter(n_blocks - 1, last))
        wait_scatter(1 - last)
        wait_scatter(last)
        wait_gather(1 - last)


def _expert_items(counts, n_rows):
    bm = EXPERT_ROWS
    n_items = n_rows // bm + N_EXPERTS - 1
    ends = jnp.cumsum(counts)
    starts = ends - counts
    blocks_of = jnp.where(counts > 0, (ends - 1) // bm - starts // bm + 1, 0)
    item_end = jnp.cumsum(blocks_of)
    item_start = item_end - blocks_of
    total = item_end[-1]
    idx = jnp.arange(n_items, dtype=I32)
    valid = idx < total
    idc = jnp.minimum(idx, jnp.maximum(total - 1, 0))
    owner = ((item_start[None, :] <= idc[:, None]) & (idc[:, None] < item_end[None, :])).astype(I32)
    pick = lambda v: jnp.sum(owner * v[None, :], axis=1)
    e = pick(jnp.arange(N_EXPERTS, dtype=I32))
    blk = pick(starts // bm) + idc - pick(item_start)
    lo = jnp.clip(pick(starts) - blk * bm, 0, bm)
    hi = jnp.clip(pick(ends) - blk * bm, 0, bm)
    prev_blk = jnp.concatenate([jnp.full((1,), -1, I32), blk[:-1]])
    prev_e = jnp.concatenate([jnp.full((1,), -1, I32), e[:-1]])
    flags = jnp.where(valid,
                      _ITEM_VALID
                      + jnp.where(blk != prev_blk, _ITEM_FIRST, 0)
                      + jnp.where(e != prev_e, _ITEM_NEW_EXPERT, 0), 0)
    as_i32 = lambda v: v.astype(I32)
    return as_i32(blk), as_i32(lo), as_i32(hi), as_i32(flags), as_i32(e)


def _experts(items, inv, h, wgu, bgu, wdn, bdn):
    tr, d = TOKEN_TILE_ROWS, D_MODEL
    n = h.shape[0] // tr
    bm = EXPERT_ROWS
    n_rows = inv.shape[0]
    n_blocks = n_rows // bm
    assert n_rows % bm == 0 and n_blocks >= 2 and n & (n - 1) == 0
    blk, lo, hi, flags, e = items
    by_expert = lambda i, blk, lo, hi, fl, e, inv: (e[i], 0, 0)
    return pl.pallas_call(
        functools.partial(_expert_kernel, n_tok=n, n_blocks=n_blocks),
        grid_spec=pltpu.PrefetchScalarGridSpec(
            num_scalar_prefetch=6, grid=(blk.shape[0],),
            in_specs=[pl.BlockSpec(memory_space=pl.ANY),
                      pl.BlockSpec((1, d, 2 * D_FF), by_expert),
                      pl.BlockSpec((1, 1, 2 * D_FF), by_expert),
                      pl.BlockSpec((1, D_FF, d), by_expert),
                      pl.BlockSpec((1, 1, d), by_expert)],
            out_specs=pl.BlockSpec(memory_space=pl.ANY),
            scratch_shapes=[pltpu.VMEM((d, 2 * D_FF), BF16),
                            pltpu.VMEM((D_FF, d), BF16),
                            pltpu.VMEM((2, bm * tr, V7X_LANES), F32),
                            pltpu.VMEM((2, bm * tr, V7X_LANES), F32),
                            pltpu.SemaphoreType.DMA((2,)),
                            pltpu.SemaphoreType.DMA((2,))]),
        out_shape=jax.ShapeDtypeStruct((n_rows * tr, V7X_LANES), F32),
        compiler_params=_params("arbitrary"),
        name="experts",
    )(blk, lo, hi, flags, e, inv, h, wgu, bgu, wdn, bdn)


def _combine_kernel(x_ref, gate_ref, g_ref, y_ref, o_ref):
    gates = gate_ref[...]
    moe = jnp.zeros(x_ref.shape, F32)
    for kk in range(TOP_K):
        moe = moe + _load_token_tiles(y_ref, (kk,), x_ref.shape[0]) * gates[:, kk:kk + 1]
    o_ref[...] = _rms(x_ref[...] + moe, g_ref[...])


def _combine(x2d, gates, g, y):
    n, d = x2d.shape
    t = min(COMBINE_TILE, n)
    return pl.pallas_call(
        _combine_kernel,
        grid=(n // t,),
        in_specs=[pl.BlockSpec((t, d), lambda i: (i, 0)),
                  pl.BlockSpec((t, V7X_LANES), lambda i: (i, 0)),
                  pl.BlockSpec((1, d), lambda i: (0, 0)),
                  pl.BlockSpec((TOP_K, t * TOKEN_TILE_ROWS, V7X_LANES), lambda i: (0, i, 0))],
        out_specs=pl.BlockSpec((t, d), lambda i: (i, 0)),
        out_shape=jax.ShapeDtypeStruct((n, d), F32),
        compiler_params=_params("arbitrary"),
        name="combine",
    )(x2d, gates, g, y.reshape(TOP_K, n * TOKEN_TILE_ROWS, V7X_LANES))


def _layer(x, mem, norm_mix_g, w_in, w_pool, pool_scale, w_gate_up, b_gate_up, gla_norm_g, w_out,
           norm_xattn_g, norm_mem_g, w_xq, w_xk, w_xv, w_xo,
           norm_moe_g, w_router, b_router, w_gu, b_gu, w_dn, b_dn, out_g):
    b, s, d = x.shape
    n = b * s
    row = lambda a: a.reshape(1, -1)

    s0, s1, s2, s3, s4 = (POOL_W, POOL_W + GLA_KEY, POOL_W + 2 * GLA_KEY,
                          POOL_W + 2 * GLA_KEY + GLA_VAL, POOL_W + 2 * GLA_KEY + GLA_VAL + GATE_RANK)
    win = jnp.concatenate(
        [w_in[:, :s3], w_in[:, s4:], w_in[:, s3:s4],
         jnp.zeros((d, V7X_LANES - GATE_RANK), w_in.dtype)], axis=1).astype(BF16)
    wgate = jnp.concatenate(
        [w_gate_up, jnp.zeros((V7X_LANES - GATE_RANK, GLA_KEY), w_gate_up.dtype)], axis=0).astype(BF16)
    wrt = w_router.T
    wrt_hi = wrt.astype(BF16)
    wrt_lo = (wrt - wrt_hi.astype(F32)).astype(BF16)

    kmem, vmem = _memkv(mem, row(norm_mem_g), w_xk.astype(BF16), w_xv.astype(BF16))
    x1 = _mix(x, row(norm_mix_g), win, w_pool.astype(BF16), row(pool_scale), wgate,
              row(b_gate_up), row(gla_norm_g), w_out.astype(BF16))
    x2 = _xattn(x1, row(norm_xattn_g), w_xq.astype(BF16), kmem, vmem, w_xo.astype(BF16))
    x2 = x2.reshape(n, d)

    h, eid, rank, gates, cnt = _route(x2, row(norm_moe_g), wrt_hi, wrt_lo, b_router.reshape(-1, 1))
    counts = cnt[:, 0].astype(I32)
    starts = (jnp.cumsum(counts) - counts).astype(I32)
    dest = _plan(starts, eid, rank).reshape(TOP_K * n)
    items = _expert_items(counts, TOP_K * n)
    y = _experts(items, _invert(dest), h, w_gu, b_gu.reshape(N_EXPERTS, 1, -1),
                 w_dn, b_dn.reshape(N_EXPERTS, 1, -1))
    out = _combine(x2, gates, row(out_g), y)
    return out.reshape(b, s, d)


def kernel(x, mem, norm_mix_g, w_in, w_pool, pool_scale, w_gate_up, b_gate_up, gla_norm_g, w_out,
           norm_xattn_g, norm_mem_g, w_xq, w_xk, w_xv, w_xo, norm_moe_g, w_router, b_router,
           w_gu, b_gu, w_dn, b_dn, norm_final_g):
    depth = norm_mix_g.shape[0]
    assert depth == 1, "the final rmsnorm is fused into the (single) layer's combine stage"
    return _layer(x, mem, norm_mix_g[0], w_in[0], w_pool[0], pool_scale[0], w_gate_up[0],
                  b_gate_up[0], gla_norm_g[0], w_out[0], norm_xattn_g[0], norm_mem_g[0],
                  w_xq[0], w_xk[0], w_xv[0], w_xo[0], norm_moe_g[0], w_router[0], b_router[0],
                  w_gu[0], b_gu[0], w_dn[0], b_dn[0], norm_final_g)
```

```python
import functools

import jax
import jax.numpy as jnp
from jax import lax
from jax.experimental import pallas as pl
from jax.experimental.pallas import tpu as pltpu

F32 = jnp.float32
BF16 = jnp.bfloat16
I32 = jnp.int32

V7X_LANES = 128
V7X_VMEM_LIMIT_BYTES = 56 * 1024 * 1024

D_MODEL = 1024
POOL_WINDOWS = (2, 4, 8, 16)
POOL_GROUP_W = 128
POOL_W = len(POOL_WINDOWS) * POOL_GROUP_W
GLA_HEADS = 4
GLA_DK = 64
GLA_DV = 128
GLA_KEY = GLA_HEADS * GLA_DK
GLA_VAL = GLA_HEADS * GLA_DV
GATE_RANK = 16
GATE_NORMALIZER = 16.0
CHUNK = 64
XATTN_HEADS = 4
XATTN_HD = D_MODEL // XATTN_HEADS
N_EXPERTS = 32
TOP_K = 4
D_FF = D_MODEL
SWIGLU_LIMIT = 7.0
SWIGLU_ALPHA = 1.702
EPS = 1e-6

_U0 = 0
_Q0 = _U0 + POOL_W
_K0 = _Q0 + GLA_KEY
_V0 = _K0 + GLA_KEY
_R0 = _V0 + GLA_VAL
_G0 = _R0 + GLA_VAL
IN_COLS_PADDED = _G0 + V7X_LANES

SEQ_TILE = 512
POOL_TAIL = 128
PLAN_TILE = 2048
EXPERT_ROWS = 256
COMBINE_TILE = 256


def _rms(x, g):
    ms = jnp.mean(x * x, axis=-1, keepdims=True)
    return x * lax.rsqrt(ms + EPS) * g


def _dot(a, b):
    return jnp.dot(a, b, preferred_element_type=F32)


def _dot_nt(a, b):
    return lax.dot_general(a, b, (((1,), (1,)), ((), ())), preferred_element_type=F32)


def _dot_tn(a, b):
    return lax.dot_general(a, b, (((0,), (0,)), ((), ())), preferred_element_type=F32)


TOKEN_TILE_ROWS = D_MODEL // V7X_LANES


def _load_token_tiles(ref, lead, n_rows, first_row=0):
    start = first_row * TOKEN_TILE_ROWS
    chunks = [ref[lead + (pl.ds(start + c, n_rows, stride=TOKEN_TILE_ROWS), slice(None))]
              for c in range(TOKEN_TILE_ROWS)]
    return jnp.concatenate(chunks, axis=1)


def _store_token_tiles(ref, lead, val, first_row=0):
    start = first_row * TOKEN_TILE_ROWS
    for c in range(TOKEN_TILE_ROWS):
        ref[lead + (pl.ds(start + c, val.shape[0], stride=TOKEN_TILE_ROWS), slice(None))] = (
            val[:, c * V7X_LANES:(c + 1) * V7X_LANES])


def _params(*semantics):
    return pltpu.CompilerParams(dimension_semantics=semantics,
                                vmem_limit_bytes=V7X_VMEM_LIMIT_BYTES)


def _memkv_kernel(mem_ref, g_ref, wk_ref, wv_ref, k_ref, v_ref):
    hm = _rms(mem_ref[0], g_ref[...]).astype(BF16)
    k = _dot(hm, wk_ref[...])
    v = _dot(hm, wv_ref[...])
    k_ref[0] = (k * (XATTN_HD ** -0.5)).astype(BF16)
    v_ref[0] = v.astype(BF16)


def _memkv(mem, g, wk, wv):
    b, m, d = mem.shape
    const = lambda i: (0, 0)
    return pl.pallas_call(
        _memkv_kernel,
        grid=(b,),
        in_specs=[pl.BlockSpec((1, m, d), lambda i: (i, 0, 0)),
                  pl.BlockSpec((1, d), const),
                  pl.BlockSpec((d, d), const),
                  pl.BlockSpec((d, d), const)],
        out_specs=[pl.BlockSpec((1, m, d), lambda i: (i, 0, 0)),
                   pl.BlockSpec((1, m, d), lambda i: (i, 0, 0))],
        out_shape=[jax.ShapeDtypeStruct((b, m, d), BF16)] * 2,
        compiler_params=_params("arbitrary"),
        name="memkv",
    )(mem, g, wk, wv)


def _mix_kernel(x_ref, g_ref, win_ref, wpool_ref, pscale_ref, band_ref, wgate_ref, bgate_ref,
                gnorm_ref, wout_ref, ltri_ref, o_ref, uprev_ref, state_ref, oacc_ref):
    t = x_ref.shape[1]
    j = pl.program_id(1)

    @pl.when(j == 0)
    def _():
        uprev_ref[...] = jnp.zeros_like(uprev_ref)
        state_ref[...] = jnp.zeros_like(state_ref)

    x = x_ref[0]
    h = _rms(x, g_ref[...]).astype(BF16)
    z = _dot(h, win_ref[...])
    u = z[:, _U0:_U0 + POOL_W]
    q = z[:, _Q0:_Q0 + GLA_KEY]
    k = z[:, _K0:_K0 + GLA_KEY]
    v = z[:, _V0:_V0 + GLA_VAL]
    r = z[:, _R0:_R0 + GLA_VAL]
    glr = z[:, _G0:_G0 + V7X_LANES]

    u_ext = jnp.concatenate([uprev_ref[...], u], axis=0).astype(BF16)
    uprev_ref[...] = u[t - POOL_TAIL:, :]
    row = lax.broadcasted_iota(I32, (t, POOL_GROUP_W), 0)
    pos = (j * t + row + 1).astype(F32)
    pooled = []
    for gi, w in enumerate(POOL_WINDOWS):
        cols = slice(gi * POOL_GROUP_W, (gi + 1) * POOL_GROUP_W)
        wsum = _dot(band_ref[gi], u_ext[:, cols])
        p = wsum / jnp.minimum(pos, float(w)) - u[:, cols]
        pooled.append(_dot(p.astype(BF16), wpool_ref[gi]))
    pool = jnp.concatenate(pooled, axis=1) * pscale_ref[...]

    gp = _dot(glr.astype(BF16), wgate_ref[...]) + bgate_ref[...]
    g = jax.nn.log_sigmoid(gp) / GATE_NORMALIZER
    g_hi = g.astype(BF16)
    g_lo = (g - g_hi.astype(F32)).astype(BF16)
    gc = _dot(ltri_ref[...], g_hi) + _dot(ltri_ref[...], g_lo)
    qe = q * (GLA_DK ** -0.5) * jnp.exp(gc)
    ke = k * jnp.exp(-gc)

    lane = lax.broadcasted_iota(I32, (CHUNK, V7X_LANES), 1)
    first_head = lane < GLA_DK
    causal = (lax.broadcasted_iota(I32, (CHUNK, CHUNK), 0)
              >= lax.broadcasted_iota(I32, (CHUNK, CHUNK), 1))
    srow = lax.broadcasted_iota(I32, (2 * GLA_DV, V7X_LANES), 0)
    slane = lax.broadcasted_iota(I32, (2 * GLA_DV, V7X_LANES), 1)
    same_head = (srow < GLA_DV) == (slane < GLA_DK)

    states = [state_ref[0], state_ref[1]]
    for c in range(t // CHUNK):
        rows = slice(c * CHUNK, (c + 1) * CHUNK)
        g_last = gc[c * CHUNK + CHUNK - 1:c * CHUNK + CHUNK, :]
        k2 = k[rows] * jnp.exp(g_last - gc[rows])
        dec = jnp.exp(g_last)
        for p in range(GLA_HEADS // 2):
            lanes = slice(p * V7X_LANES, (p + 1) * V7X_LANES)
            vals = slice(p * 2 * GLA_DV, (p + 1) * 2 * GLA_DV)
            qe_p = qe[rows, lanes]
            ke_p = ke[rows, lanes].astype(BF16)
            v_p = v[rows, vals].astype(BF16)
            s_prev = states[p]
            outs = []
            for a in range(2):
                q_a = jnp.where(first_head if a == 0 else ~first_head, qe_p, 0.0)
                att = jnp.where(causal, _dot_nt(q_a.astype(BF16), ke_p), 0.0)
                outs.append(_dot(att.astype(BF16), v_p[:, a * GLA_DV:(a + 1) * GLA_DV]))
            inter = _dot_nt(qe_p.astype(BF16), s_prev.astype(BF16))
            oacc_ref[rows, vals] = jnp.concatenate(outs, axis=1) + inter
            d_state = _dot_tn(v_p, k2[:, lanes].astype(BF16))
            states[p] = s_prev * dec[:, lanes] + jnp.where(same_head, d_state, 0.0)
    state_ref[0] = states[0]
    state_ref[1] = states[1]

    o = oacc_ref[...]
    gla = []
    for hh in range(GLA_HEADS):
        cols = slice(hh * GLA_DV, (hh + 1) * GLA_DV)
        oh = o[:, cols]
        on = oh * lax.rsqrt(jnp.mean(oh * oh, axis=-1, keepdims=True) + EPS) * gnorm_ref[...]
        gla.append(on * jax.nn.silu(r[:, cols]))
    mix = jnp.concatenate([pool] + gla, axis=1).astype(BF16)
    o_ref[0] = x + _dot(mix, wout_ref[...])


def _mix(x, g, win, wpool, pscale, wgate, bgate, gnorm, wout):
    b, s, d = x.shape
    t = SEQ_TILE
    assert s % t == 0 and t % CHUNK == 0 and t >= POOL_TAIL
    i = jnp.arange(t)[:, None] + POOL_TAIL
    jj = jnp.arange(t + POOL_TAIL)[None, :]
    band = jnp.stack([((jj <= i) & (jj > i - w)) for w in POOL_WINDOWS]).astype(BF16)
    ii = jnp.arange(t)
    ltri = ((ii[:, None] // CHUNK == ii[None, :] // CHUNK)
            & (ii[None, :] <= ii[:, None])).astype(BF16)
    c2 = lambda bi, j: (0, 0)
    c3 = lambda bi, j: (0, 0, 0)
    return pl.pallas_call(
        _mix_kernel,
        grid=(b, s // t),
        in_specs=[pl.BlockSpec((1, t, d), lambda bi, j: (bi, j, 0)),
                  pl.BlockSpec((1, d), c2),
                  pl.BlockSpec((d, IN_COLS_PADDED), c2),
                  pl.BlockSpec((len(POOL_WINDOWS), POOL_GROUP_W, POOL_GROUP_W), c3),
                  pl.BlockSpec((1, POOL_W), c2),
                  pl.BlockSpec((len(POOL_WINDOWS), t, t + POOL_TAIL), c3),
                  pl.BlockSpec((V7X_LANES, GLA_KEY), c2),
                  pl.BlockSpec((1, GLA_KEY), c2),
                  pl.BlockSpec((1, GLA_DV), c2),
                  pl.BlockSpec((POOL_W + GLA_VAL, d), c2),
                  pl.BlockSpec((t, t), c2)],
        out_specs=pl.BlockSpec((1, t, d), lambda bi, j: (bi, j, 0)),
        out_shape=jax.ShapeDtypeStruct((b, s, d), F32),
        scratch_shapes=[pltpu.VMEM((POOL_TAIL, POOL_W), F32),
                        pltpu.VMEM((GLA_HEADS // 2, 2 * GLA_DV, V7X_LANES), F32),
                        pltpu.VMEM((t, GLA_VAL), F32)],
        compiler_params=_params("arbitrary", "arbitrary"),
        name="mix",
    )(x, g, win, wpool, pscale, band, wgate, bgate, gnorm, wout, ltri)


def _xattn_kernel(x_ref, g_ref, wq_ref, k_ref, v_ref, wo_ref, o_ref):
    x = x_ref[0]
    h = _rms(x, g_ref[...]).astype(BF16)
    q = _dot(h, wq_ref[...])
    heads = []
    for hh in range(XATTN_HEADS):
        cols = slice(hh * XATTN_HD, (hh + 1) * XATTN_HD)
        sc = _dot_nt(q[:, cols].astype(BF16), k_ref[0][:, cols])
        e = jnp.exp(sc - jnp.max(sc, axis=-1, keepdims=True))
        p = e / jnp.sum(e, axis=-1, keepdims=True)
        heads.append(_dot(p.astype(BF16), v_ref[0][:, cols]))
    o = jnp.concatenate(heads, axis=1).astype(BF16)
    o_ref[0] = x + _dot(o, wo_ref[...])


def _xattn(x, g, wq, kmem, vmem, wo):
    b, s, d = x.shape
    m = kmem.shape[1]
    t = SEQ_TILE
    c2 = lambda bi, j: (0, 0)
    return pl.pallas_call(
        _xattn_kernel,
        grid=(b, s // t),
        in_specs=[pl.BlockSpec((1, t, d), lambda bi, j: (bi, j, 0)),
                  pl.BlockSpec((1, d), c2),
                  pl.BlockSpec((d, d), c2),
                  pl.BlockSpec((1, m, d), lambda bi, j: (bi, 0, 0)),
                  pl.BlockSpec((1, m, d), lambda bi, j: (bi, 0, 0)),
                  pl.BlockSpec((d, d), c2)],
        out_specs=pl.BlockSpec((1, t, d), lambda bi, j: (bi, j, 0)),
        out_shape=jax.ShapeDtypeStruct((b, s, d), F32),
        compiler_params=_params("arbitrary", "arbitrary"),
        name="xattn",
    )(x, g, wq, kmem, vmem, wo)


def _route_kernel(x_ref, g_ref, wrt_hi_ref, wrt_lo_ref, br_ref, utri_ref,
                  h_ref, eid_ref, rank_ref, gate_ref, cnt_ref, carry_ref):
    t = x_ref.shape[0]

    @pl.when(pl.program_id(0) == 0)
    def _():
        carry_ref[...] = jnp.zeros_like(carry_ref)

    h = _rms(x_ref[...], g_ref[...])
    _store_token_tiles(h_ref, (), h)
    h_hi = h.astype(BF16)
    h_lo = (h - h_hi.astype(F32)).astype(BF16)
    logits = (_dot_nt(wrt_hi_ref[...], h_hi) + _dot_nt(wrt_hi_ref[...], h_lo)
              + _dot_nt(wrt_lo_ref[...], h_hi)) + br_ref[...]

    e_iota = lax.broadcasted_iota(I32, (N_EXPERTS, t), 0)
    e_iota_f = e_iota.astype(F32)
    work = logits
    vals, onehots = [], []
    for _ in range(TOP_K):
        m = jnp.max(work, axis=0, keepdims=True)
        idx = jnp.min(jnp.where(work == m, e_iota_f, float(N_EXPERTS)), axis=0, keepdims=True)
        hit = e_iota_f == idx
        vals.append(m)
        onehots.append(hit)
        work = jnp.where(hit, -jnp.inf, work)

    ex = [jnp.exp(vk - vals[0]) for vk in vals]
    den = ex[0] + ex[1] + ex[2] + ex[3]

    member = jnp.zeros((N_EXPERTS, t), F32)
    for hit in onehots:
        member = member + jnp.where(hit, 1.0, 0.0)
    carry = carry_ref[...]
    before = _dot(member.astype(BF16), utri_ref[...]) + carry[:, 0:1]

    row4 = lax.broadcasted_iota(I32, (TOP_K, t), 0)
    grow = lax.broadcasted_iota(I32, (V7X_LANES, t), 0)
    eid = jnp.zeros((TOP_K, t), I32)
    rank = jnp.zeros((TOP_K, t), I32)
    gates_t = jnp.zeros((V7X_LANES, t), F32)
    for kk in range(TOP_K):
        idx_k = jnp.sum(jnp.where(onehots[kk], e_iota_f, 0.0), axis=0, keepdims=True)
        rank_k = jnp.sum(jnp.where(onehots[kk], before, 0.0), axis=0, keepdims=True)
        eid = jnp.where(row4 == kk, idx_k.astype(I32), eid)
        rank = jnp.where(row4 == kk, rank_k.astype(I32), rank)
        gates_t = jnp.where(grow == kk, ex[kk] / den, gates_t)
    eid_ref[...] = eid
    rank_ref[...] = rank
    gate_ref[...] = gates_t.T

    new_carry = carry + jnp.sum(member, axis=1, keepdims=True)
    carry_ref[...] = new_carry
    cnt_ref[...] = new_carry


def _route(x2d, g, wrt_hi, wrt_lo, br):
    n, d = x2d.shape
    t = SEQ_TILE
    ii = jnp.arange(t)
    utri = (ii[:, None] < ii[None, :]).astype(BF16)
    c2 = lambda i: (0, 0)
    return pl.pallas_call(
        _route_kernel,
        grid=(n // t,),
        in_specs=[pl.BlockSpec((t, d), lambda i: (i, 0)),
                  pl.BlockSpec((1, d), c2),
                  pl.BlockSpec((N_EXPERTS, d), c2),
                  pl.BlockSpec((N_EXPERTS, d), c2),
                  pl.BlockSpec((N_EXPERTS, 1), c2),
                  pl.BlockSpec((t, t), c2)],
        out_specs=[pl.BlockSpec((t * TOKEN_TILE_ROWS, V7X_LANES), lambda i: (i, 0)),
                   pl.BlockSpec((TOP_K, t), lambda i: (0, i)),
                   pl.BlockSpec((TOP_K, t), lambda i: (0, i)),
                   pl.BlockSpec((t, V7X_LANES), lambda i: (i, 0)),
                   pl.BlockSpec((N_EXPERTS, V7X_LANES), c2)],
        out_shape=[jax.ShapeDtypeStruct((n * TOKEN_TILE_ROWS, V7X_LANES), F32),
                   jax.ShapeDtypeStruct((TOP_K, n), I32),
                   jax.ShapeDtypeStruct((TOP_K, n), I32),
                   jax.ShapeDtypeStruct((n, V7X_LANES), F32),
                   jax.ShapeDtypeStruct((N_EXPERTS, V7X_LANES), F32)],
        scratch_shapes=[pltpu.VMEM((N_EXPERTS, V7X_LANES), F32)],
        compiler_params=_params("arbitrary"),
        name="route",
    )(x2d, g, wrt_hi, wrt_lo, br, utri)


def _plan_kernel(start_ref, eid_ref, rank_ref, dest_ref):
    e = eid_ref[...]
    dest = rank_ref[...]
    for ee in range(N_EXPERTS):
        dest = dest + jnp.where(e == ee, start_ref[ee], 0)
    dest_ref[...] = dest


def _plan(starts, eid, rank):
    n = eid.shape[1]
    t = min(PLAN_TILE, n)
    blk = pl.BlockSpec((TOP_K, t), lambda i, s: (0, i))
    return pl.pallas_call(
        _plan_kernel,
        grid_spec=pltpu.PrefetchScalarGridSpec(
            num_scalar_prefetch=1, grid=(n // t,), in_specs=[blk, blk], out_specs=blk),
        out_shape=jax.ShapeDtypeStruct((TOP_K, n), I32),
        compiler_params=_params("arbitrary"),
        name="plan",
    )(starts, eid, rank)


def _invert_kernel(dest_ref, inv_ref):
    def body(i, carry):
        inv_ref[dest_ref[i]] = i
        return carry

    lax.fori_loop(0, dest_ref.shape[0], body, 0, unroll=32)


def _invert(dest_flat):
    return pl.pallas_call(
        _invert_kernel,
        grid_spec=pltpu.PrefetchScalarGridSpec(
            num_scalar_prefetch=1, grid=(1,), in_specs=[],
            out_specs=pl.BlockSpec(memory_space=pltpu.SMEM)),
        out_shape=jax.ShapeDtypeStruct(dest_flat.shape, I32),
        compiler_params=_params("arbitrary"),
        name="invert",
    )(dest_flat)


_ITEM_VALID, _ITEM_FIRST, _ITEM_NEW_EXPERT, _ITEM_HAS_NEXT_EXPERT = 1, 2, 4, 8


def _expert_kernel(blk_ref, lo_ref, hi_ref, flag_ref, exp_ref, next_exp_ref, inv_ref,
                   h_ref, wgu_hbm, bgu_ref, wdn_hbm, bdn_ref, out_ref,
                   wgu_bf, wdn_bf, wgu_f32, wdn_f32, xs_buf, y_buf, gsem, ssem, wsem,
                   *, n_tok, n_blocks):
    tr = TOKEN_TILE_ROWS
    bm = xs_buf.shape[1] // tr
    i = pl.program_id(0)
    flags = flag_ref[i]
    blk = blk_ref[i]
    odd = (blk & 1) == 1

    def token_tile(ref, row):
        return ref.at[pl.ds(pl.multiple_of(row * tr, tr), tr), :]

    def gather(b, s):
        return [pltpu.make_async_copy(
            token_tile(h_ref, inv_ref[b * bm + r] & (n_tok - 1)),
            xs_buf.at[s, pl.ds(r * tr, tr), :], gsem.at[s]) for r in range(bm)]

    def scatter(b, s):
        return [pltpu.make_async_copy(
            y_buf.at[s, pl.ds(r * tr, tr), :],
            token_tile(out_ref, inv_ref[b * bm + r]), ssem.at[s]) for r in range(bm)]

    def wait_gather(s):
        pltpu.make_async_copy(h_ref.at[pl.ds(0, bm * tr), :], xs_buf.at[s], gsem.at[s]).wait()

    def wait_scatter(s):
        pltpu.make_async_copy(y_buf.at[s], out_ref.at[pl.ds(0, bm * tr), :], ssem.at[s]).wait()

    def start_all(copies):
        for r, c in enumerate(copies):
            c.start(priority=r % 2)

    @pl.when(i == 0)
    def _():
        start_all(gather(0, 0))

    def weight_copies(e):
        return (pltpu.make_async_copy(wgu_hbm.at[e], wgu_f32, wsem.at[0]),
                pltpu.make_async_copy(wdn_hbm.at[e], wdn_f32, wsem.at[1]))

    @pl.when(i == 0)
    def _():
        for c in weight_copies(exp_ref[0]):
            c.start()

    @pl.when((flags & _ITEM_NEW_EXPERT) != 0)
    def _():
        for c in weight_copies(exp_ref[i]):
            c.wait()
        wgu_bf[...] = wgu_f32[...].astype(BF16)
        wdn_bf[...] = wdn_f32[...].astype(BF16)

        @pl.when((flags & _ITEM_HAS_NEXT_EXPERT) != 0)
        def _():
            for c in weight_copies(next_exp_ref[i]):
                c.start()

    bgu = bgu_ref[exp_ref[i]]
    bdn = bdn_ref[exp_ref[i]]

    dyn_slot = (blk & 1,)
    half = bm // 2

    def load_x(r0, n_rows):
        return _load_token_tiles(xs_buf, dyn_slot, n_rows, r0).astype(BF16)

    def ffn_rows(x, r0, merge):
        n_rows = x.shape[0]
        gu = _dot(x, wgu_bf[...]) + bgu
        gate = jnp.minimum(gu[:, :D_FF], SWIGLU_LIMIT)
        up = jnp.clip(gu[:, D_FF:], -SWIGLU_LIMIT, SWIGLU_LIMIT)
        act = gate * jax.nn.sigmoid(SWIGLU_ALPHA * gate) * (up + 1.0)
        yb = _dot(act.astype(BF16), wdn_bf[...]) + bdn
        row = r0 + lax.broadcasted_iota(I32, yb.shape, 0)
        mine = (row >= lo_ref[i]) & (row < hi_ref[i])
        other = _load_token_tiles(y_buf, dyn_slot, n_rows, r0) if merge else 0.0
        return jnp.where(mine, yb, other)

    def first_item(s, has_prev):
        wait_gather(s)
        x_top = load_x(0, half)
        start_all(gather(jnp.minimum(blk + 1, n_blocks - 1), 1 - s))
        _store_token_tiles(y_buf, dyn_slot, ffn_rows(x_top, 0, False), 0)
        if has_prev:
            start_all(scatter(blk - 1, 1 - s))
        _store_token_tiles(y_buf, dyn_slot, ffn_rows(load_x(half, half), half, False), half)

    def later_item(s):
        _store_token_tiles(y_buf, dyn_slot, ffn_rows(load_x(0, bm), 0, True), 0)

    first = (flags & (_ITEM_VALID | _ITEM_FIRST)) == (_ITEM_VALID | _ITEM_FIRST)
    later = (flags & (_ITEM_VALID | _ITEM_FIRST)) == _ITEM_VALID

    @pl.when(first & (blk == 0))
    def _():
        first_item(0, False)

    @pl.when(first & odd & (blk >= 3))
    def _():
        wait_scatter(1)

    @pl.when(first & odd)
    def _():
        first_item(1, True)

    @pl.when(first & jnp.logical_not(odd) & (blk >= 2))
    def _():
        wait_scatter(0)
        first_item(0, True)

    @pl.when(later & jnp.logical_not(odd))
    def _():
        later_item(0)

    @pl.when(later & odd)
    def _():
        later_item(1)

    @pl.when(i == pl.num_programs(0) - 1)
    def _():
        last = (n_blocks - 1) & 1
        start_all(scatter(n_blocks - 1, last))
        wait_scatter(1 - last)
        wait_scatter(last)
        wait_gather(1 - last)


def _expert_items(counts, n_rows):
    bm = EXPERT_ROWS
    n_items = n_rows // bm + N_EXPERTS - 1
    ends = jnp.cumsum(counts)
    starts = ends - counts
    blocks_of = jnp.where(counts > 0, (ends - 1) // bm - starts // bm + 1, 0)
    item_end = jnp.cumsum(blocks_of)
    item_start = item_end - blocks_of
    total = item_end[-1]
    idx = jnp.arange(n_items, dtype=I32)
    valid = idx < total
    idc = jnp.minimum(idx, jnp.maximum(total - 1, 0))
    owner = ((item_start[None, :] <= idc[:, None]) & (idc[:, None] < item_end[None, :])).astype(I32)
    pick = lambda v: jnp.sum(owner * v[None, :], axis=1)
    e = pick(jnp.arange(N_EXPERTS, dtype=I32))
    blk = pick(starts // bm) + idc - pick(item_start)
    lo = jnp.clip(pick(starts) - blk * bm, 0, bm)
    hi = jnp.clip(pick(ends) - blk * bm, 0, bm)
    prev_blk = jnp.concatenate([jnp.full((1,), -1, I32), blk[:-1]])
    prev_e = jnp.concatenate([jnp.full((1,), -1, I32), e[:-1]])
    ids = jnp.arange(N_EXPERTS, dtype=I32)
    owners = jnp.where(counts > 0, ids, N_EXPERTS)
    next_of = jnp.min(jnp.where(ids[None, :] > ids[:, None], owners[None, :], N_EXPERTS), axis=1)
    next_e = pick(next_of)
    new_expert = e != prev_e
    flags = jnp.where(valid,
                      _ITEM_VALID
                      + jnp.where(blk != prev_blk, _ITEM_FIRST, 0)
                      + jnp.where(new_expert, _ITEM_NEW_EXPERT, 0)
                      + jnp.where(new_expert & (next_e < N_EXPERTS), _ITEM_HAS_NEXT_EXPERT, 0), 0)
    as_i32 = lambda v: v.astype(I32)
    return (as_i32(blk), as_i32(lo), as_i32(hi), as_i32(flags), as_i32(e),
            as_i32(jnp.minimum(next_e, N_EXPERTS - 1)))


def _experts(items, inv, h, wgu, bgu, wdn, bdn):
    tr, d = TOKEN_TILE_ROWS, D_MODEL
    n = h.shape[0] // tr
    bm = EXPERT_ROWS
    n_rows = inv.shape[0]
    n_blocks = n_rows // bm
    assert n_rows % bm == 0 and n_blocks >= 2 and n & (n - 1) == 0
    blk, lo, hi, flags, e, next_e = items
    whole = lambda i, *prefetch: (0, 0, 0)
    return pl.pallas_call(
        functools.partial(_expert_kernel, n_tok=n, n_blocks=n_blocks),
        grid_spec=pltpu.PrefetchScalarGridSpec(
            num_scalar_prefetch=7, grid=(blk.shape[0],),
            in_specs=[pl.BlockSpec(memory_space=pl.ANY),
                      pl.BlockSpec(memory_space=pl.ANY),
                      pl.BlockSpec((N_EXPERTS, 1, 2 * D_FF), whole),
                      pl.BlockSpec(memory_space=pl.ANY),
                      pl.BlockSpec((N_EXPERTS, 1, d), whole)],
            out_specs=pl.BlockSpec(memory_space=pl.ANY),
            scratch_shapes=[pltpu.VMEM((d, 2 * D_FF), BF16),
                            pltpu.VMEM((D_FF, d), BF16),
                            pltpu.VMEM((d, 2 * D_FF), F32),
                            pltpu.VMEM((D_FF, d), F32),
                            pltpu.VMEM((2, bm * tr, V7X_LANES), F32),
                            pltpu.VMEM((2, bm * tr, V7X_LANES), F32),
                            pltpu.SemaphoreType.DMA((2,)),
                            pltpu.SemaphoreType.DMA((2,)),
                            pltpu.SemaphoreType.DMA((2,))]),
        out_shape=jax.ShapeDtypeStruct((n_rows * tr, V7X_LANES), F32),
        compiler_params=_params("arbitrary"),
        name="experts",
    )(blk, lo, hi, flags, e, next_e, inv, h, wgu, bgu, wdn, bdn)


def _combine_kernel(x_ref, gate_ref, g_ref, y_ref, o_ref):
    gates = gate_ref[...]
    moe = jnp.zeros(x_ref.shape, F32)
    for kk in range(TOP_K):
        moe = moe + _load_token_tiles(y_ref, (kk,), x_ref.shape[0]) * gates[:, kk:kk + 1]
    o_ref[...] = _rms(x_ref[...] + moe, g_ref[...])


def _combine(x2d, gates, g, y):
    n, d = x2d.shape
    t = min(COMBINE_TILE, n)
    return pl.pallas_call(
        _combine_kernel,
        grid=(n // t,),
        in_specs=[pl.BlockSpec((t, d), lambda i: (i, 0)),
                  pl.BlockSpec((t, V7X_LANES), lambda i: (i, 0)),
                  pl.BlockSpec((1, d), lambda i: (0, 0)),
                  pl.BlockSpec((TOP_K, t * TOKEN_TILE_ROWS, V7X_LANES), lambda i: (0, i, 0))],
        out_specs=pl.BlockSpec((t, d), lambda i: (i, 0)),
        out_shape=jax.ShapeDtypeStruct((n, d), F32),
        compiler_params=_params("arbitrary"),
        name="combine",
    )(x2d, gates, g, y.reshape(TOP_K, n * TOKEN_TILE_ROWS, V7X_LANES))


def _layer(x, mem, norm_mix_g, w_in, w_pool, pool_scale, w_gate_up, b_gate_up, gla_norm_g, w_out,
           norm_xattn_g, norm_mem_g, w_xq, w_xk, w_xv, w_xo,
           norm_moe_g, w_router, b_router, w_gu, b_gu, w_dn, b_dn, out_g):
    b, s, d = x.shape
    n = b * s
    row = lambda a: a.reshape(1, -1)

    s0, s1, s2, s3, s4 = (POOL_W, POOL_W + GLA_KEY, POOL_W + 2 * GLA_KEY,
                          POOL_W + 2 * GLA_KEY + GLA_VAL, POOL_W + 2 * GLA_KEY + GLA_VAL + GATE_RANK)
    win = jnp.concatenate(
        [w_in[:, :s3], w_in[:, s4:], w_in[:, s3:s4],
         jnp.zeros((d, V7X_LANES - GATE_RANK), w_in.dtype)], axis=1).astype(BF16)
    wgate = jnp.concatenate(
        [w_gate_up, jnp.zeros((V7X_LANES - GATE_RANK, GLA_KEY), w_gate_up.dtype)], axis=0).astype(BF16)
    wrt = w_router.T
    wrt_hi = wrt.astype(BF16)
    wrt_lo = (wrt - wrt_hi.astype(F32)).astype(BF16)

    kmem, vmem = _memkv(mem, row(norm_mem_g), w_xk.astype(BF16), w_xv.astype(BF16))
    x1 = _mix(x, row(norm_mix_g), win, w_pool.astype(BF16), row(pool_scale), wgate,
              row(b_gate_up), row(gla_norm_g), w_out.astype(BF16))
    x2 = _xattn(x1, row(norm_xattn_g), w_xq.astype(BF16), kmem, vmem, w_xo.astype(BF16))
    x2 = x2.reshape(n, d)

    h, eid, rank, gates, cnt = _route(x2, row(norm_moe_g), wrt_hi, wrt_lo, b_router.reshape(-1, 1))
    counts = cnt[:, 0].astype(I32)
    starts = (jnp.cumsum(counts) - counts).astype(I32)
    dest = _plan(starts, eid, rank).reshape(TOP_K * n)
    items = _expert_items(counts, TOP_K * n)
    y = _experts(items, _invert(dest), h, w_gu, b_gu.reshape(N_EXPERTS, 1, -1),
                 w_dn, b_dn.reshape(N_EXPERTS, 1, -1))
    out = _combine(x2, gates, row(out_g), y)
    return out.reshape(b, s, d)


def kernel(x, mem, norm_mix_g, w_in, w_pool, pool_scale, w_gate_up, b_gate_up, gla_norm_g, w_out,
           norm_xattn_g, norm_mem_g, w_xq, w_xk, w_xv, w_xo, norm_moe_g, w_router, b_router,
           w_gu, b_gu, w_dn, b_dn, norm_final_g):
    depth = norm_mix_g.shape[0]
    assert depth == 1, "the final rmsnorm is fused into the (single) layer's combine stage"
    return _layer(x, mem, norm_mix_g[0], w_in[0], w_pool[0], pool_scale[0], w_gate_up[0],
                  b_gate_up[0], gla_norm_g[0], w_out[0], norm_xattn_g[0], norm_mem_g[0],
                  w_xq[0], w_xk[0], w_xv[0], w_xo[0], norm_moe_g[0], w_router[0], b_router[0],
                  w_gu[0], b_gu[0], w_dn[0], b_dn[0], norm_final_g)
```

```python
import functools

import jax
import jax.numpy as jnp
from jax import lax
from jax.experimental import pallas as pl
from jax.experimental.pallas import tpu as pltpu

F32 = jnp.float32
BF16 = jnp.bfloat16
I32 = jnp.int32

V7X_LANES = 128
V7X_VMEM_LIMIT_BYTES = 56 * 1024 * 1024

D_MODEL = 1024
POOL_WINDOWS = (2, 4, 8, 16)
POOL_GROUP_W = 128
POOL_W = len(POOL_WINDOWS) * POOL_GROUP_W
GLA_HEADS = 4
GLA_DK = 64
GLA_DV = 128
GLA_KEY = GLA_HEADS * GLA_DK
GLA_VAL = GLA_HEADS * GLA_DV
GATE_RANK = 16
GATE_NORMALIZER = 16.0
CHUNK = 64
XATTN_HEADS = 4
XATTN_HD = D_MODEL // XATTN_HEADS
N_EXPERTS = 32
TOP_K = 4
D_FF = D_MODEL
SWIGLU_LIMIT = 7.0
SWIGLU_ALPHA = 1.702
EPS = 1e-6

_U0 = 0
_Q0 = _U0 + POOL_W
_K0 = _Q0 + GLA_KEY
_V0 = _K0 + GLA_KEY
_R0 = _V0 + GLA_VAL
_G0 = _R0 + GLA_VAL
IN_COLS_PADDED = _G0 + V7X_LANES

SEQ_TILE = 512
POOL_TAIL = 128
PLAN_TILE = 2048
EXPERT_ROWS = 256
EXPERT_SLOTS = 3
COMBINE_TILE = 256


def _rms(x, g):
    ms = jnp.mean(x * x, axis=-1, keepdims=True)
    return x * lax.rsqrt(ms + EPS) * g


def _dot(a, b):
    return jnp.dot(a, b, preferred_element_type=F32)


def _dot_nt(a, b):
    return lax.dot_general(a, b, (((1,), (1,)), ((), ())), preferred_element_type=F32)


def _dot_tn(a, b):
    return lax.dot_general(a, b, (((0,), (0,)), ((), ())), preferred_element_type=F32)


TOKEN_TILE_ROWS = D_MODEL // V7X_LANES


def _load_token_tiles(ref, lead, n_rows, first_row=0):
    start = first_row * TOKEN_TILE_ROWS
    chunks = [ref[lead + (pl.ds(start + c, n_rows, stride=TOKEN_TILE_ROWS), slice(None))]
              for c in range(TOKEN_TILE_ROWS)]
    return jnp.concatenate(chunks, axis=1)


def _store_token_tiles(ref, lead, val, first_row=0):
    start = first_row * TOKEN_TILE_ROWS
    for c in range(TOKEN_TILE_ROWS):
        ref[lead + (pl.ds(start + c, val.shape[0], stride=TOKEN_TILE_ROWS), slice(None))] = (
            val[:, c * V7X_LANES:(c + 1) * V7X_LANES])


def _params(*semantics):
    return pltpu.CompilerParams(dimension_semantics=semantics,
                                vmem_limit_bytes=V7X_VMEM_LIMIT_BYTES)


def _memkv_kernel(mem_ref, g_ref, wk_ref, wv_ref, k_ref, v_ref):
    hm = _rms(mem_ref[0], g_ref[...]).astype(BF16)
    k = _dot(hm, wk_ref[...])
    v = _dot(hm, wv_ref[...])
    k_ref[0] = (k * (XATTN_HD ** -0.5)).astype(BF16)
    v_ref[0] = v.astype(BF16)


def _memkv(mem, g, wk, wv):
    b, m, d = mem.shape
    const = lambda i: (0, 0)
    return pl.pallas_call(
        _memkv_kernel,
        grid=(b,),
        in_specs=[pl.BlockSpec((1, m, d), lambda i: (i, 0, 0)),
                  pl.BlockSpec((1, d), const),
                  pl.BlockSpec((d, d), const),
                  pl.BlockSpec((d, d), const)],
        out_specs=[pl.BlockSpec((1, m, d), lambda i: (i, 0, 0)),
                   pl.BlockSpec((1, m, d), lambda i: (i, 0, 0))],
        out_shape=[jax.ShapeDtypeStruct((b, m, d), BF16)] * 2,
        compiler_params=_params("arbitrary"),
        name="memkv",
    )(mem, g, wk, wv)


def _mix_kernel(x_ref, g_ref, win_ref, wpool_ref, pscale_ref, band_ref, wgate_ref, bgate_ref,
                gnorm_ref, wout_ref, ltri_ref, o_ref, uprev_ref, state_ref, oacc_ref):
    t = x_ref.shape[1]
    j = pl.program_id(1)

    @pl.when(j == 0)
    def _():
        uprev_ref[...] = jnp.zeros_like(uprev_ref)
        state_ref[...] = jnp.zeros_like(state_ref)

    x = x_ref[0]
    h = _rms(x, g_ref[...]).astype(BF16)
    z = _dot(h, win_ref[...])
    u = z[:, _U0:_U0 + POOL_W]
    q = z[:, _Q0:_Q0 + GLA_KEY]
    k = z[:, _K0:_K0 + GLA_KEY]
    v = z[:, _V0:_V0 + GLA_VAL]
    r = z[:, _R0:_R0 + GLA_VAL]
    glr = z[:, _G0:_G0 + V7X_LANES]

    u_ext = jnp.concatenate([uprev_ref[...], u], axis=0).astype(BF16)
    uprev_ref[...] = u[t - POOL_TAIL:, :]
    row = lax.broadcasted_iota(I32, (t, POOL_GROUP_W), 0)
    pos = (j * t + row + 1).astype(F32)
    pooled = []
    for gi, w in enumerate(POOL_WINDOWS):
        cols = slice(gi * POOL_GROUP_W, (gi + 1) * POOL_GROUP_W)
        wsum = _dot(band_ref[gi], u_ext[:, cols])
        p = wsum / jnp.minimum(pos, float(w)) - u[:, cols]
        pooled.append(_dot(p.astype(BF16), wpool_ref[gi]))
    pool = jnp.concatenate(pooled, axis=1) * pscale_ref[...]

    gp = _dot(glr.astype(BF16), wgate_ref[...]) + bgate_ref[...]
    g = jax.nn.log_sigmoid(gp) / GATE_NORMALIZER
    g_hi = g.astype(BF16)
    g_lo = (g - g_hi.astype(F32)).astype(BF16)
    gc = _dot(ltri_ref[...], g_hi) + _dot(ltri_ref[...], g_lo)
    qe = q * (GLA_DK ** -0.5) * jnp.exp(gc)
    ke = k * jnp.exp(-gc)

    lane = lax.broadcasted_iota(I32, (CHUNK, V7X_LANES), 1)
    first_head = lane < GLA_DK
    causal = (lax.broadcasted_iota(I32, (CHUNK, CHUNK), 0)
              >= lax.broadcasted_iota(I32, (CHUNK, CHUNK), 1))
    srow = lax.broadcasted_iota(I32, (2 * GLA_DV, V7X_LANES), 0)
    slane = lax.broadcasted_iota(I32, (2 * GLA_DV, V7X_LANES), 1)
    same_head = (srow < GLA_DV) == (slane < GLA_DK)

    states = [state_ref[0], state_ref[1]]
    for c in range(t // CHUNK):
        rows = slice(c * CHUNK, (c + 1) * CHUNK)
        g_last = gc[c * CHUNK + CHUNK - 1:c * CHUNK + CHUNK, :]
        k2 = k[rows] * jnp.exp(g_last - gc[rows])
        dec = jnp.exp(g_last)
        for p in range(GLA_HEADS // 2):
            lanes = slice(p * V7X_LANES, (p + 1) * V7X_LANES)
            vals = slice(p * 2 * GLA_DV, (p + 1) * 2 * GLA_DV)
            qe_p = qe[rows, lanes]
            ke_p = ke[rows, lanes].astype(BF16)
            v_p = v[rows, vals].astype(BF16)
            s_prev = states[p]
            outs = []
            for a in range(2):
                q_a = jnp.where(first_head if a == 0 else ~first_head, qe_p, 0.0)
                att = jnp.where(causal, _dot_nt(q_a.astype(BF16), ke_p), 0.0)
                outs.append(_dot(att.astype(BF16), v_p[:, a * GLA_DV:(a + 1) * GLA_DV]))
            inter = _dot_nt(qe_p.astype(BF16), s_prev.astype(BF16))
            oacc_ref[rows, vals] = jnp.concatenate(outs, axis=1) + inter
            d_state = _dot_tn(v_p, k2[:, lanes].astype(BF16))
            states[p] = s_prev * dec[:, lanes] + jnp.where(same_head, d_state, 0.0)
    state_ref[0] = states[0]
    state_ref[1] = states[1]

    o = oacc_ref[...]
    gla = []
    for hh in range(GLA_HEADS):
        cols = slice(hh * GLA_DV, (hh + 1) * GLA_DV)
        oh = o[:, cols]
        on = oh * lax.rsqrt(jnp.mean(oh * oh, axis=-1, keepdims=True) + EPS) * gnorm_ref[...]
        gla.append(on * jax.nn.silu(r[:, cols]))
    mix = jnp.concatenate([pool] + gla, axis=1).astype(BF16)
    o_ref[0] = x + _dot(mix, wout_ref[...])


def _mix(x, g, win, wpool, pscale, wgate, bgate, gnorm, wout):
    b, s, d = x.shape
    t = SEQ_TILE
    assert s % t == 0 and t % CHUNK == 0 and t >= POOL_TAIL
    i = jnp.arange(t)[:, None] + POOL_TAIL
    jj = jnp.arange(t + POOL_TAIL)[None, :]
    band = jnp.stack([((jj <= i) & (jj > i - w)) for w in POOL_WINDOWS]).astype(BF16)
    ii = jnp.arange(t)
    ltri = ((ii[:, None] // CHUNK == ii[None, :] // CHUNK)
            & (ii[None, :] <= ii[:, None])).astype(BF16)
    c2 = lambda bi, j: (0, 0)
    c3 = lambda bi, j: (0, 0, 0)
    return pl.pallas_call(
        _mix_kernel,
        grid=(b, s // t),
        in_specs=[pl.BlockSpec((1, t, d), lambda bi, j: (bi, j, 0)),
                  pl.BlockSpec((1, d), c2),
                  pl.BlockSpec((d, IN_COLS_PADDED), c2),
                  pl.BlockSpec((len(POOL_WINDOWS), POOL_GROUP_W, POOL_GROUP_W), c3),
                  pl.BlockSpec((1, POOL_W), c2),
                  pl.BlockSpec((len(POOL_WINDOWS), t, t + POOL_TAIL), c3),
                  pl.BlockSpec((V7X_LANES, GLA_KEY), c2),
                  pl.BlockSpec((1, GLA_KEY), c2),
                  pl.BlockSpec((1, GLA_DV), c2),
                  pl.BlockSpec((POOL_W + GLA_VAL, d), c2),
                  pl.BlockSpec((t, t), c2)],
        out_specs=pl.BlockSpec((1, t, d), lambda bi, j: (bi, j, 0)),
        out_shape=jax.ShapeDtypeStruct((b, s, d), F32),
        scratch_shapes=[pltpu.VMEM((POOL_TAIL, POOL_W), F32),
                        pltpu.VMEM((GLA_HEADS // 2, 2 * GLA_DV, V7X_LANES), F32),
                        pltpu.VMEM((t, GLA_VAL), F32)],
        compiler_params=_params("arbitrary", "arbitrary"),
        name="mix",
    )(x, g, win, wpool, pscale, band, wgate, bgate, gnorm, wout, ltri)


def _xattn_kernel(x_ref, g_ref, wq_ref, k_ref, v_ref, wo_ref, o_ref):
    x = x_ref[0]
    h = _rms(x, g_ref[...]).astype(BF16)
    q = _dot(h, wq_ref[...])
    heads = []
    for hh in range(XATTN_HEADS):
        cols = slice(hh * XATTN_HD, (hh + 1) * XATTN_HD)
        sc = _dot_nt(q[:, cols].astype(BF16), k_ref[0][:, cols])
        e = jnp.exp(sc - jnp.max(sc, axis=-1, keepdims=True))
        p = e / jnp.sum(e, axis=-1, keepdims=True)
        heads.append(_dot(p.astype(BF16), v_ref[0][:, cols]))
    o = jnp.concatenate(heads, axis=1).astype(BF16)
    o_ref[0] = x + _dot(o, wo_ref[...])


def _xattn(x, g, wq, kmem, vmem, wo):
    b, s, d = x.shape
    m = kmem.shape[1]
    t = SEQ_TILE
    c2 = lambda bi, j: (0, 0)
    return pl.pallas_call(
        _xattn_kernel,
        grid=(b, s // t),
        in_specs=[pl.BlockSpec((1, t, d), lambda bi, j: (bi, j, 0)),
                  pl.BlockSpec((1, d), c2),
                  pl.BlockSpec((d, d), c2),
                  pl.BlockSpec((1, m, d), lambda bi, j: (bi, 0, 0)),
                  pl.BlockSpec((1, m, d), lambda bi, j: (bi, 0, 0)),
                  pl.BlockSpec((d, d), c2)],
        out_specs=pl.BlockSpec((1, t, d), lambda bi, j: (bi, j, 0)),
        out_shape=jax.ShapeDtypeStruct((b, s, d), F32),
        compiler_params=_params("arbitrary", "arbitrary"),
        name="xattn",
    )(x, g, wq, kmem, vmem, wo)


def _route_kernel(x_ref, g_ref, wrt_hi_ref, wrt_lo_ref, br_ref, utri_ref,
                  h_ref, eid_ref, rank_ref, gate_ref, cnt_ref, carry_ref):
    t = x_ref.shape[0]

    @pl.when(pl.program_id(0) == 0)
    def _():
        carry_ref[...] = jnp.zeros_like(carry_ref)

    h = _rms(x_ref[...], g_ref[...])
    _store_token_tiles(h_ref, (), h)
    h_hi = h.astype(BF16)
    h_lo = (h - h_hi.astype(F32)).astype(BF16)
    logits = (_dot_nt(wrt_hi_ref[...], h_hi) + _dot_nt(wrt_hi_ref[...], h_lo)
              + _dot_nt(wrt_lo_ref[...], h_hi)) + br_ref[...]

    e_iota = lax.broadcasted_iota(I32, (N_EXPERTS, t), 0)
    e_iota_f = e_iota.astype(F32)
    work = logits
    vals, onehots = [], []
    for _ in range(TOP_K):
        m = jnp.max(work, axis=0, keepdims=True)
        idx = jnp.min(jnp.where(work == m, e_iota_f, float(N_EXPERTS)), axis=0, keepdims=True)
        hit = e_iota_f == idx
        vals.append(m)
        onehots.append(hit)
        work = jnp.where(hit, -jnp.inf, work)

    ex = [jnp.exp(vk - vals[0]) for vk in vals]
    den = ex[0] + ex[1] + ex[2] + ex[3]

    member = jnp.zeros((N_EXPERTS, t), F32)
    for hit in onehots:
        member = member + jnp.where(hit, 1.0, 0.0)
    carry = carry_ref[...]
    before = _dot(member.astype(BF16), utri_ref[...]) + carry[:, 0:1]

    row4 = lax.broadcasted_iota(I32, (TOP_K, t), 0)
    grow = lax.broadcasted_iota(I32, (V7X_LANES, t), 0)
    eid = jnp.zeros((TOP_K, t), I32)
    rank = jnp.zeros((TOP_K, t), I32)
    gates_t = jnp.zeros((V7X_LANES, t), F32)
    for kk in range(TOP_K):
        idx_k = jnp.sum(jnp.where(onehots[kk], e_iota_f, 0.0), axis=0, keepdims=True)
        rank_k = jnp.sum(jnp.where(onehots[kk], before, 0.0), axis=0, keepdims=True)
        eid = jnp.where(row4 == kk, idx_k.astype(I32), eid)
        rank = jnp.where(row4 == kk, rank_k.astype(I32), rank)
        gates_t = jnp.where(grow == kk, ex[kk] / den, gates_t)
    eid_ref[...] = eid
    rank_ref[...] = rank
    gate_ref[...] = gates_t.T

    new_carry = carry + jnp.sum(member, axis=1, keepdims=True)
    carry_ref[...] = new_carry
    cnt_ref[...] = new_carry


def _route(x2d, g, wrt_hi, wrt_lo, br):
    n, d = x2d.shape
    t = SEQ_TILE
    ii = jnp.arange(t)
    utri = (ii[:, None] < ii[None, :]).astype(BF16)
    c2 = lambda i: (0, 0)
    return pl.pallas_call(
        _route_kernel,
        grid=(n // t,),
        in_specs=[pl.BlockSpec((t, d), lambda i: (i, 0)),
                  pl.BlockSpec((1, d), c2),
                  pl.BlockSpec((N_EXPERTS, d), c2),
                  pl.BlockSpec((N_EXPERTS, d), c2),
                  pl.BlockSpec((N_EXPERTS, 1), c2),
                  pl.BlockSpec((t, t), c2)],
        out_specs=[pl.BlockSpec((t * TOKEN_TILE_ROWS, V7X_LANES), lambda i: (i, 0)),
                   pl.BlockSpec((TOP_K, t), lambda i: (0, i)),
                   pl.BlockSpec((TOP_K, t), lambda i: (0, i)),
                   pl.BlockSpec((t, V7X_LANES), lambda i: (i, 0)),
                   pl.BlockSpec((N_EXPERTS, V7X_LANES), c2)],
        out_shape=[jax.ShapeDtypeStruct((n * TOKEN_TILE_ROWS, V7X_LANES), F32),
                   jax.ShapeDtypeStruct((TOP_K, n), I32),
                   jax.ShapeDtypeStruct((TOP_K, n), I32),
                   jax.ShapeDtypeStruct((n, V7X_LANES), F32),
                   jax.ShapeDtypeStruct((N_EXPERTS, V7X_LANES), F32)],
        scratch_shapes=[pltpu.VMEM((N_EXPERTS, V7X_LANES), F32)],
        compiler_params=_params("arbitrary"),
        name="route",
    )(x2d, g, wrt_hi, wrt_lo, br, utri)


def _plan_kernel(start_ref, eid_ref, rank_ref, dest_ref):
    e = eid_ref[...]
    dest = rank_ref[...]
    for ee in range(N_EXPERTS):
        dest = dest + jnp.where(e == ee, start_ref[ee], 0)
    dest_ref[...] = dest


def _plan(starts, eid, rank):
    n = eid.shape[1]
    t = min(PLAN_TILE, n)
    blk = pl.BlockSpec((TOP_K, t), lambda i, s: (0, i))
    return pl.pallas_call(
        _plan_kernel,
        grid_spec=pltpu.PrefetchScalarGridSpec(
            num_scalar_prefetch=1, grid=(n // t,), in_specs=[blk, blk], out_specs=blk),
        out_shape=jax.ShapeDtypeStruct((TOP_K, n), I32),
        compiler_params=_params("arbitrary"),
        name="plan",
    )(starts, eid, rank)


def _invert_kernel(dest_ref, inv_ref):
    def body(i, carry):
        inv_ref[dest_ref[i]] = i
        return carry

    lax.fori_loop(0, dest_ref.shape[0], body, 0, unroll=32)


def _invert(dest_flat):
    return pl.pallas_call(
        _invert_kernel,
        grid_spec=pltpu.PrefetchScalarGridSpec(
            num_scalar_prefetch=1, grid=(1,), in_specs=[],
            out_specs=pl.BlockSpec(memory_space=pltpu.SMEM)),
        out_shape=jax.ShapeDtypeStruct(dest_flat.shape, I32),
        compiler_params=_params("arbitrary"),
        name="invert",
    )(dest_flat)


_ITEM_VALID, _ITEM_FIRST, _ITEM_NEW_EXPERT, _ITEM_HAS_NEXT_EXPERT = 1, 2, 4, 8


def _expert_kernel(blk_ref, lo_ref, hi_ref, flag_ref, exp_ref, next_exp_ref, inv_ref,
                   h_ref, wgu_hbm, bgu_ref, wdn_hbm, bdn_ref, out_ref,
                   wgu_bf, wdn_bf, wgu_f32, wdn_f32, xs_buf, y_buf, gsem, ssem, wsem,
                   *, n_tok, n_blocks):
    tr = TOKEN_TILE_ROWS
    bm = xs_buf.shape[1] // tr
    i = pl.program_id(0)
    flags = flag_ref[i]
    blk = blk_ref[i]
    slots = xs_buf.shape[0]
    slot = lax.rem(blk, slots)

    def token_tile(ref, row):
        return ref.at[pl.ds(pl.multiple_of(row * tr, tr), tr), :]

    def gather(b, s):
        return [pltpu.make_async_copy(
            token_tile(h_ref, inv_ref[b * bm + r] & (n_tok - 1)),
            xs_buf.at[s, pl.ds(r * tr, tr), :], gsem.at[s]) for r in range(bm)]

    def scatter(b, s):
        return [pltpu.make_async_copy(
            y_buf.at[s, pl.ds(r * tr, tr), :],
            token_tile(out_ref, inv_ref[b * bm + r]), ssem.at[s]) for r in range(bm)]

    def wait_gather(s):
        pltpu.make_async_copy(h_ref.at[pl.ds(0, bm * tr), :], xs_buf.at[s], gsem.at[s]).wait()

    def wait_scatter(s):
        pltpu.make_async_copy(y_buf.at[s], out_ref.at[pl.ds(0, bm * tr), :], ssem.at[s]).wait()

    def start_all(copies):
        for r, c in enumerate(copies):
            c.start(priority=r % 2)

    @pl.when(i == 0)
    def _():
        for b in range(slots - 1):
            start_all(gather(b, b))

    def weight_copies(e):
        return (pltpu.make_async_copy(wgu_hbm.at[e], wgu_f32, wsem.at[0]),
                pltpu.make_async_copy(wdn_hbm.at[e], wdn_f32, wsem.at[1]))

    @pl.when(i == 0)
    def _():
        for c in weight_copies(exp_ref[0]):
            c.start()

    @pl.when((flags & _ITEM_NEW_EXPERT) != 0)
    def _():
        for c in weight_copies(exp_ref[i]):
            c.wait()
        wgu_bf[...] = wgu_f32[...].astype(BF16)
        wdn_bf[...] = wdn_f32[...].astype(BF16)

        @pl.when((flags & _ITEM_HAS_NEXT_EXPERT) != 0)
        def _():
            for c in weight_copies(next_exp_ref[i]):
                c.start()

    bgu = bgu_ref[exp_ref[i]]
    bdn = bdn_ref[exp_ref[i]]

    dyn_slot = (slot,)
    half = bm // 2

    def load_x(r0, n_rows):
        return _load_token_tiles(xs_buf, dyn_slot, n_rows, r0).astype(BF16)

    def ffn_rows(x, r0, merge):
        n_rows = x.shape[0]
        gu = _dot(x, wgu_bf[...]) + bgu
        gate = jnp.minimum(gu[:, :D_FF], SWIGLU_LIMIT)
        up = jnp.clip(gu[:, D_FF:], -SWIGLU_LIMIT, SWIGLU_LIMIT)
        act = gate * jax.nn.sigmoid(SWIGLU_ALPHA * gate) * (up + 1.0)
        yb = _dot(act.astype(BF16), wdn_bf[...]) + bdn
        row = r0 + lax.broadcasted_iota(I32, yb.shape, 0)
        mine = (row >= lo_ref[i]) & (row < hi_ref[i])
        other = _load_token_tiles(y_buf, dyn_slot, n_rows, r0) if merge else 0.0
        return jnp.where(mine, yb, other)

    def first_item(s, has_prev):
        wait_gather(s)
        x_top = load_x(0, half)
        ahead = slots - 1
        start_all(gather(jnp.minimum(blk + ahead, n_blocks - 1), (s + ahead) % slots))
        _store_token_tiles(y_buf, dyn_slot, ffn_rows(x_top, 0, False), 0)
        if has_prev:
            start_all(scatter(blk - 1, (s - 1) % slots))
        _store_token_tiles(y_buf, dyn_slot, ffn_rows(load_x(half, half), half, False), half)

    first = (flags & (_ITEM_VALID | _ITEM_FIRST)) == (_ITEM_VALID | _ITEM_FIRST)
    later = (flags & (_ITEM_VALID | _ITEM_FIRST)) == _ITEM_VALID

    @pl.when(first & (blk == 0))
    def _():
        first_item(0, False)

    @pl.when(first & (blk >= slots))
    def _():
        for s in range(slots):
            @pl.when(slot == s)
            def _():
                wait_scatter(s)

    for s in range(slots):
        @pl.when(first & (slot == s) & (blk > 0))
        def _():
            first_item(s, True)

    @pl.when(later)
    def _():
        _store_token_tiles(y_buf, dyn_slot, ffn_rows(load_x(0, bm), 0, True), 0)

    @pl.when(i == pl.num_programs(0) - 1)
    def _():
        start_all(scatter(n_blocks - 1, (n_blocks - 1) % slots))
        for s in range(slots):
            wait_scatter(s)
        for ahead in range(1, slots):
            wait_gather((n_blocks - 1 + ahead) % slots)


def _expert_items(counts, n_rows):
    bm = EXPERT_ROWS
    n_items = n_rows // bm + N_EXPERTS - 1
    ends = jnp.cumsum(counts)
    starts = ends - counts
    blocks_of = jnp.where(counts > 0, (ends - 1) // bm - starts // bm + 1, 0)
    item_end = jnp.cumsum(blocks_of)
    item_start = item_end - blocks_of
    total = item_end[-1]
    idx = jnp.arange(n_items, dtype=I32)
    valid = idx < total
    idc = jnp.minimum(idx, jnp.maximum(total - 1, 0))
    owner = ((item_start[None, :] <= idc[:, None]) & (idc[:, None] < item_end[None, :])).astype(I32)
    pick = lambda v: jnp.sum(owner * v[None, :], axis=1)
    e = pick(jnp.arange(N_EXPERTS, dtype=I32))
    blk = pick(starts // bm) + idc - pick(item_start)
    lo = jnp.clip(pick(starts) - blk * bm, 0, bm)
    hi = jnp.clip(pick(ends) - blk * bm, 0, bm)
    prev_blk = jnp.concatenate([jnp.full((1,), -1, I32), blk[:-1]])
    prev_e = jnp.concatenate([jnp.full((1,), -1, I32), e[:-1]])
    ids = jnp.arange(N_EXPERTS, dtype=I32)
    owners = jnp.where(counts > 0, ids, N_EXPERTS)
    next_of = jnp.min(jnp.where(ids[None, :] > ids[:, None], owners[None, :], N_EXPERTS), axis=1)
    next_e = pick(next_of)
    new_expert = e != prev_e
    flags = jnp.where(valid,
                      _ITEM_VALID
                      + jnp.where(blk != prev_blk, _ITEM_FIRST, 0)
                      + jnp.where(new_expert, _ITEM_NEW_EXPERT, 0)
                      + jnp.where(new_expert & (next_e < N_EXPERTS), _ITEM_HAS_NEXT_EXPERT, 0), 0)
    as_i32 = lambda v: v.astype(I32)
    return (as_i32(blk), as_i32(lo), as_i32(hi), as_i32(flags), as_i32(e),
            as_i32(jnp.minimum(next_e, N_EXPERTS - 1)))


def _experts(items, inv, h, wgu, bgu, wdn, bdn):
    tr, d = TOKEN_TILE_ROWS, D_MODEL
    n = h.shape[0] // tr
    bm = EXPERT_ROWS
    n_rows = inv.shape[0]
    n_blocks = n_rows // bm
    assert n_rows % bm == 0 and n_blocks >= EXPERT_SLOTS and n & (n - 1) == 0
    blk, lo, hi, flags, e, next_e = items
    whole = lambda i, *prefetch: (0, 0, 0)
    return pl.pallas_call(
        functools.partial(_expert_kernel, n_tok=n, n_blocks=n_blocks),
        grid_spec=pltpu.PrefetchScalarGridSpec(
            num_scalar_prefetch=7, grid=(blk.shape[0],),
            in_specs=[pl.BlockSpec(memory_space=pl.ANY),
                      pl.BlockSpec(memory_space=pl.ANY),
                      pl.BlockSpec((N_EXPERTS, 1, 2 * D_FF), whole),
                      pl.BlockSpec(memory_space=pl.ANY),
                      pl.BlockSpec((N_EXPERTS, 1, d), whole)],
            out_specs=pl.BlockSpec(memory_space=pl.ANY),
            scratch_shapes=[pltpu.VMEM((d, 2 * D_FF), BF16),
                            pltpu.VMEM((D_FF, d), BF16),
                            pltpu.VMEM((d, 2 * D_FF), F32),
                            pltpu.VMEM((D_FF, d), F32),
                            pltpu.VMEM((EXPERT_SLOTS, bm * tr, V7X_LANES), F32),
                            pltpu.VMEM((EXPERT_SLOTS, bm * tr, V7X_LANES), F32),
                            pltpu.SemaphoreType.DMA((EXPERT_SLOTS,)),
                            pltpu.SemaphoreType.DMA((EXPERT_SLOTS,)),
                            pltpu.SemaphoreType.DMA((2,))]),
        out_shape=jax.ShapeDtypeStruct((n_rows * tr, V7X_LANES), F32),
        compiler_params=_params("arbitrary"),
        name="experts",
    )(blk, lo, hi, flags, e, next_e, inv, h, wgu, bgu, wdn, bdn)


def _combine_kernel(x_ref, gate_ref, g_ref, y_ref, o_ref):
    gates = gate_ref[...]
    moe = jnp.zeros(x_ref.shape, F32)
    for kk in range(TOP_K):
        moe = moe + _load_token_tiles(y_ref, (kk,), x_ref.shape[0]) * gates[:, kk:kk + 1]
    o_ref[...] = _rms(x_ref[...] + moe, g_ref[...])


def _combine(x2d, gates, g, y):
    n, d = x2d.shape
    t = min(COMBINE_TILE, n)
    return pl.pallas_call(
        _combine_kernel,
        grid=(n // t,),
        in_specs=[pl.BlockSpec((t, d), lambda i: (i, 0)),
                  pl.BlockSpec((t, V7X_LANES), lambda i: (i, 0)),
                  pl.BlockSpec((1, d), lambda i: (0, 0)),
                  pl.BlockSpec((TOP_K, t * TOKEN_TILE_ROWS, V7X_LANES), lambda i: (0, i, 0))],
        out_specs=pl.BlockSpec((t, d), lambda i: (i, 0)),
        out_shape=jax.ShapeDtypeStruct((n, d), F32),
        compiler_params=_params("arbitrary"),
        name="combine",
    )(x2d, gates, g, y.reshape(TOP_K, n * TOKEN_TILE_ROWS, V7X_LANES))


def _layer(x, mem, norm_mix_g, w_in, w_pool, pool_scale, w_gate_up, b_gate_up, gla_norm_g, w_out,
           norm_xattn_g, norm_mem_g, w_xq, w_xk, w_xv, w_xo,
           norm_moe_g, w_router, b_router, w_gu, b_gu, w_dn, b_dn, out_g):
    b, s, d = x.shape
    n = b * s
    row = lambda a: a.reshape(1, -1)

    s0, s1, s2, s3, s4 = (POOL_W, POOL_W + GLA_KEY, POOL_W + 2 * GLA_KEY,
                          POOL_W + 2 * GLA_KEY + GLA_VAL, POOL_W + 2 * GLA_KEY + GLA_VAL + GATE_RANK)
    win = jnp.concatenate(
        [w_in[:, :s3], w_in[:, s4:], w_in[:, s3:s4],
         jnp.zeros((d, V7X_LANES - GATE_RANK), w_in.dtype)], axis=1).astype(BF16)
    wgate = jnp.concatenate(
        [w_gate_up, jnp.zeros((V7X_LANES - GATE_RANK, GLA_KEY), w_gate_up.dtype)], axis=0).astype(BF16)
    wrt = w_router.T
    wrt_hi = wrt.astype(BF16)
    wrt_lo = (wrt - wrt_hi.astype(F32)).astype(BF16)

    kmem, vmem = _memkv(mem, row(norm_mem_g), w_xk.astype(BF16), w_xv.astype(BF16))
    x1 = _mix(x, row(norm_mix_g), win, w_pool.astype(BF16), row(pool_scale), wgate,
              row(b_gate_up), row(gla_norm_g), w_out.astype(BF16))
    x2 = _xattn(x1, row(norm_xattn_g), w_xq.astype(BF16), kmem, vmem, w_xo.astype(BF16))
    x2 = x2.reshape(n, d)

    h, eid, rank, gates, cnt = _route(x2, row(norm_moe_g), wrt_hi, wrt_lo, b_router.reshape(-1, 1))
    counts = cnt[:, 0].astype(I32)
    starts = (jnp.cumsum(counts) - counts).astype(I32)
    dest = _plan(starts, eid, rank).reshape(TOP_K * n)
    items = _expert_items(counts, TOP_K * n)
    y = _experts(items, _invert(dest), h, w_gu, b_gu.reshape(N_EXPERTS, 1, -1),
                 w_dn, b_dn.reshape(N_EXPERTS, 1, -1))
    out = _combine(x2, gates, row(out_g), y)
    return out.reshape(b, s, d)


def kernel(x, mem, norm_mix_g, w_in, w_pool, pool_scale, w_gate_up, b_gate_up, gla_norm_g, w_out,
           norm_xattn_g, norm_mem_g, w_xq, w_xk, w_xv, w_xo, norm_moe_g, w_router, b_router,
           w_gu, b_gu, w_dn, b_dn, norm_final_g):
    depth = norm_mix_g.shape[0]
    assert depth == 1, "the final rmsnorm is fused into the (single) layer's combine stage"
    return _layer(x, mem, norm_mix_g[0], w_in[0], w_pool[0], pool_scale[0], w_gate_up[0],
                  b_gate_up[0], gla_norm_g[0], w_out[0], norm_xattn_g[0], norm_mem_g[0],
                  w_xq[0], w_xk[0], w_xv[0], w_xo[0], norm_moe_g[0], w_router[0], b_router[0],
                  w_gu[0], b_gu[0], w_dn[0], b_dn[0], norm_final_g)
```

```python
import functools

import jax
import jax.numpy as jnp
from jax import lax
from jax.experimental import pallas as pl
from jax.experimental.pallas import tpu as pltpu

F32 = jnp.float32
BF16 = jnp.bfloat16
I32 = jnp.int32

V7X_LANES = 128
V7X_VMEM_LIMIT_BYTES = 56 * 1024 * 1024

D_MODEL = 1024
POOL_WINDOWS = (2, 4, 8, 16)
POOL_GROUP_W = 128
POOL_W = len(POOL_WINDOWS) * POOL_GROUP_W
GLA_HEADS = 4
GLA_DK = 64
GLA_DV = 128
GLA_KEY = GLA_HEADS * GLA_DK
GLA_VAL = GLA_HEADS * GLA_DV
GATE_RANK = 16
GATE_NORMALIZER = 16.0
CHUNK = 64
XATTN_HEADS = 4
XATTN_HD = D_MODEL // XATTN_HEADS
N_EXPERTS = 32
TOP_K = 4
D_FF = D_MODEL
SWIGLU_LIMIT = 7.0
SWIGLU_ALPHA = 1.702
EPS = 1e-6

_U0 = 0
_Q0 = _U0 + POOL_W
_K0 = _Q0 + GLA_KEY
_V0 = _K0 + GLA_KEY
_R0 = _V0 + GLA_VAL
_G0 = _R0 + GLA_VAL
IN_COLS_PADDED = _G0 + V7X_LANES

SEQ_TILE = 512
POOL_TAIL = 16
PLAN_TILE = 2048
EXPERT_ROWS = 256
EXPERT_SLOTS = 4
COMBINE_TILE = 256


def _rms(x, g):
    ms = jnp.mean(x * x, axis=-1, keepdims=True)
    return x * lax.rsqrt(ms + EPS) * g


def _dot(a, b):
    return jnp.dot(a, b, preferred_element_type=F32)


def _dot_nt(a, b):
    return lax.dot_general(a, b, (((1,), (1,)), ((), ())), preferred_element_type=F32)


def _dot_tn(a, b):
    return lax.dot_general(a, b, (((0,), (0,)), ((), ())), preferred_element_type=F32)


TOKEN_TILE_ROWS = D_MODEL // V7X_LANES


def _load_token_tiles(ref, lead, n_rows, first_row=0):
    start = first_row * TOKEN_TILE_ROWS
    chunks = [ref[lead + (pl.ds(start + c, n_rows, stride=TOKEN_TILE_ROWS), slice(None))]
              for c in range(TOKEN_TILE_ROWS)]
    return jnp.concatenate(chunks, axis=1)


def _store_token_tiles(ref, lead, val, first_row=0):
    start = first_row * TOKEN_TILE_ROWS
    for c in range(TOKEN_TILE_ROWS):
        ref[lead + (pl.ds(start + c, val.shape[0], stride=TOKEN_TILE_ROWS), slice(None))] = (
            val[:, c * V7X_LANES:(c + 1) * V7X_LANES])


def _params(*semantics):
    return pltpu.CompilerParams(dimension_semantics=semantics,
                                vmem_limit_bytes=V7X_VMEM_LIMIT_BYTES)


def _memkv_kernel(mem_ref, g_ref, wk_ref, wv_ref, k_ref, v_ref):
    hm = _rms(mem_ref[0], g_ref[...]).astype(BF16)
    k = _dot(hm, wk_ref[...])
    v = _dot(hm, wv_ref[...])
    k_ref[0] = (k * (XATTN_HD ** -0.5)).astype(BF16)
    v_ref[0] = v.astype(BF16)


def _memkv(mem, g, wk, wv):
    b, m, d = mem.shape
    const = lambda i: (0, 0)
    return pl.pallas_call(
        _memkv_kernel,
        grid=(b,),
        in_specs=[pl.BlockSpec((1, m, d), lambda i: (i, 0, 0)),
                  pl.BlockSpec((1, d), const),
                  pl.BlockSpec((d, d), const),
                  pl.BlockSpec((d, d), const)],
        out_specs=[pl.BlockSpec((1, m, d), lambda i: (i, 0, 0)),
                   pl.BlockSpec((1, m, d), lambda i: (i, 0, 0))],
        out_shape=[jax.ShapeDtypeStruct((b, m, d), BF16)] * 2,
        compiler_params=_params("arbitrary"),
        name="memkv",
    )(mem, g, wk, wv)


def _mix_kernel(x_ref, g_ref, win_ref, wpool_ref, pscale_ref, wgate_ref, bgate_ref,
                gnorm_ref, wout_ref, ltri_ref, o_ref, uprev_ref, state_ref, oacc_ref):
    t = x_ref.shape[1]
    j = pl.program_id(1)

    @pl.when(j == 0)
    def _():
        uprev_ref[...] = jnp.zeros_like(uprev_ref)
        state_ref[...] = jnp.zeros_like(state_ref)

    x = x_ref[0]
    h = _rms(x, g_ref[...]).astype(BF16)
    z = _dot(h, win_ref[...])
    u = z[:, _U0:_U0 + POOL_W]
    q = z[:, _Q0:_Q0 + GLA_KEY]
    k = z[:, _K0:_K0 + GLA_KEY]
    v = z[:, _V0:_V0 + GLA_VAL]
    r = z[:, _R0:_R0 + GLA_VAL]
    glr = z[:, _G0:_G0 + V7X_LANES]

    u_ext = jnp.concatenate([uprev_ref[...], u], axis=0)
    uprev_ref[...] = u[t - POOL_TAIL:, :]
    row = lax.broadcasted_iota(I32, (t, POOL_GROUP_W), 0)
    pos = (j * t + row + 1).astype(F32)
    pooled = []
    for gi, w in enumerate(POOL_WINDOWS):
        cols = slice(gi * POOL_GROUP_W, (gi + 1) * POOL_GROUP_W)
        wsum = u_ext[:, cols]
        shift = 1
        while shift < w:
            wsum = wsum + pltpu.roll(wsum, shift, axis=0)
            shift *= 2
        p = wsum[POOL_TAIL:, :] / jnp.minimum(pos, float(w)) - u[:, cols]
        pooled.append(_dot(p.astype(BF16), wpool_ref[gi]))
    pool = jnp.concatenate(pooled, axis=1) * pscale_ref[...]

    gp = _dot(glr.astype(BF16), wgate_ref[...]) + bgate_ref[...]
    g = jax.nn.log_sigmoid(gp) / GATE_NORMALIZER
    g_hi = g.astype(BF16)
    g_lo = (g - g_hi.astype(F32)).astype(BF16)
    gc = _dot(ltri_ref[...], g_hi) + _dot(ltri_ref[...], g_lo)
    qe = q * (GLA_DK ** -0.5) * jnp.exp(gc)
    ke = k * jnp.exp(-gc)

    lane = lax.broadcasted_iota(I32, (CHUNK, V7X_LANES), 1)
    first_head = lane < GLA_DK
    causal = (lax.broadcasted_iota(I32, (CHUNK, CHUNK), 0)
              >= lax.broadcasted_iota(I32, (CHUNK, CHUNK), 1))
    srow = lax.broadcasted_iota(I32, (2 * GLA_DV, V7X_LANES), 0)
    slane = lax.broadcasted_iota(I32, (2 * GLA_DV, V7X_LANES), 1)
    same_head = (srow < GLA_DV) == (slane < GLA_DK)

    states = [state_ref[0], state_ref[1]]
    for c in range(t // CHUNK):
        rows = slice(c * CHUNK, (c + 1) * CHUNK)
        g_last = gc[c * CHUNK + CHUNK - 1:c * CHUNK + CHUNK, :]
        k2 = k[rows] * jnp.exp(g_last - gc[rows])
        dec = jnp.exp(g_last)
        for p in range(GLA_HEADS // 2):
            lanes = slice(p * V7X_LANES, (p + 1) * V7X_LANES)
            vals = slice(p * 2 * GLA_DV, (p + 1) * 2 * GLA_DV)
            qe_p = qe[rows, lanes]
            ke_p = ke[rows, lanes].astype(BF16)
            v_p = v[rows, vals].astype(BF16)
            s_prev = states[p]
            outs = []
            for a in range(2):
                q_a = jnp.where(first_head if a == 0 else ~first_head, qe_p, 0.0)
                att = jnp.where(causal, _dot_nt(q_a.astype(BF16), ke_p), 0.0)
                outs.append(_dot(att.astype(BF16), v_p[:, a * GLA_DV:(a + 1) * GLA_DV]))
            inter = _dot_nt(qe_p.astype(BF16), s_prev.astype(BF16))
            oacc_ref[rows, vals] = jnp.concatenate(outs, axis=1) + inter
            d_state = _dot_tn(v_p, k2[:, lanes].astype(BF16))
            states[p] = s_prev * dec[:, lanes] + jnp.where(same_head, d_state, 0.0)
    state_ref[0] = states[0]
    state_ref[1] = states[1]

    o = oacc_ref[...]
    gla = []
    for hh in range(GLA_HEADS):
        cols = slice(hh * GLA_DV, (hh + 1) * GLA_DV)
        oh = o[:, cols]
        on = oh * lax.rsqrt(jnp.mean(oh * oh, axis=-1, keepdims=True) + EPS) * gnorm_ref[...]
        gla.append(on * jax.nn.silu(r[:, cols]))
    mix = jnp.concatenate([pool] + gla, axis=1).astype(BF16)
    o_ref[0] = x + _dot(mix, wout_ref[...])


def _mix(x, g, win, wpool, pscale, wgate, bgate, gnorm, wout):
    b, s, d = x.shape
    t = SEQ_TILE
    assert s % t == 0 and t % CHUNK == 0 and t >= POOL_TAIL >= max(POOL_WINDOWS) - 1
    ii = jnp.arange(t)
    ltri = ((ii[:, None] // CHUNK == ii[None, :] // CHUNK)
            & (ii[None, :] <= ii[:, None])).astype(BF16)
    c2 = lambda bi, j: (0, 0)
    c3 = lambda bi, j: (0, 0, 0)
    return pl.pallas_call(
        _mix_kernel,
        grid=(b, s // t),
        in_specs=[pl.BlockSpec((1, t, d), lambda bi, j: (bi, j, 0)),
                  pl.BlockSpec((1, d), c2),
                  pl.BlockSpec((d, IN_COLS_PADDED), c2),
                  pl.BlockSpec((len(POOL_WINDOWS), POOL_GROUP_W, POOL_GROUP_W), c3),
                  pl.BlockSpec((1, POOL_W), c2),
                  pl.BlockSpec((V7X_LANES, GLA_KEY), c2),
                  pl.BlockSpec((1, GLA_KEY), c2),
                  pl.BlockSpec((1, GLA_DV), c2),
                  pl.BlockSpec((POOL_W + GLA_VAL, d), c2),
                  pl.BlockSpec((t, t), c2)],
        out_specs=pl.BlockSpec((1, t, d), lambda bi, j: (bi, j, 0)),
        out_shape=jax.ShapeDtypeStruct((b, s, d), F32),
        scratch_shapes=[pltpu.VMEM((POOL_TAIL, POOL_W), F32),
                        pltpu.VMEM((GLA_HEADS // 2, 2 * GLA_DV, V7X_LANES), F32),
                        pltpu.VMEM((t, GLA_VAL), F32)],
        compiler_params=_params("arbitrary", "arbitrary"),
        name="mix",
    )(x, g, win, wpool, pscale, wgate, bgate, gnorm, wout, ltri)


def _xattn_kernel(x_ref, g_ref, wq_ref, k_ref, v_ref, wo_ref, o_ref):
    x = x_ref[0]
    h = _rms(x, g_ref[...]).astype(BF16)
    q = _dot(h, wq_ref[...])
    heads = []
    for hh in range(XATTN_HEADS):
        cols = slice(hh * XATTN_HD, (hh + 1) * XATTN_HD)
        sc = _dot_nt(q[:, cols].astype(BF16), k_ref[0][:, cols])
        e = jnp.exp(sc - jnp.max(sc, axis=-1, keepdims=True))
        p = e / jnp.sum(e, axis=-1, keepdims=True)
        heads.append(_dot(p.astype(BF16), v_ref[0][:, cols]))
    o = jnp.concatenate(heads, axis=1).astype(BF16)
    o_ref[0] = x + _dot(o, wo_ref[...])


def _xattn(x, g, wq, kmem, vmem, wo):
    b, s, d = x.shape
    m = kmem.shape[1]
    t = SEQ_TILE
    c2 = lambda bi, j: (0, 0)
    return pl.pallas_call(
        _xattn_kernel,
        grid=(b, s // t),
        in_specs=[pl.BlockSpec((1, t, d), lambda bi, j: (bi, j, 0)),
                  pl.BlockSpec((1, d), c2),
                  pl.BlockSpec((d, d), c2),
                  pl.BlockSpec((1, m, d), lambda bi, j: (bi, 0, 0)),
                  pl.BlockSpec((1, m, d), lambda bi, j: (bi, 0, 0)),
                  pl.BlockSpec((d, d), c2)],
        out_specs=pl.BlockSpec((1, t, d), lambda bi, j: (bi, j, 0)),
        out_shape=jax.ShapeDtypeStruct((b, s, d), F32),
        compiler_params=_params("arbitrary", "arbitrary"),
        name="xattn",
    )(x, g, wq, kmem, vmem, wo)


def _route_kernel(x_ref, g_ref, wrt_hi_ref, wrt_lo_ref, br_ref, utri_ref,
                  h_ref, eid_ref, rank_ref, gate_ref, cnt_ref, carry_ref):
    t = x_ref.shape[0]

    @pl.when(pl.program_id(0) == 0)
    def _():
        carry_ref[...] = jnp.zeros_like(carry_ref)

    h = _rms(x_ref[...], g_ref[...])
    _store_token_tiles(h_ref, (), h)
    h_hi = h.astype(BF16)
    h_lo = (h - h_hi.astype(F32)).astype(BF16)
    logits = (_dot_nt(wrt_hi_ref[...], h_hi) + _dot_nt(wrt_hi_ref[...], h_lo)
              + _dot_nt(wrt_lo_ref[...], h_hi)) + br_ref[...]

    e_iota = lax.broadcasted_iota(I32, (N_EXPERTS, t), 0)
    e_iota_f = e_iota.astype(F32)
    work = logits
    vals, onehots = [], []
    for _ in range(TOP_K):
        m = jnp.max(work, axis=0, keepdims=True)
        idx = jnp.min(jnp.where(work == m, e_iota_f, float(N_EXPERTS)), axis=0, keepdims=True)
        hit = e_iota_f == idx
        vals.append(m)
        onehots.append(hit)
        work = jnp.where(hit, -jnp.inf, work)

    ex = [jnp.exp(vk - vals[0]) for vk in vals]
    den = ex[0] + ex[1] + ex[2] + ex[3]

    member = jnp.zeros((N_EXPERTS, t), F32)
    for hit in onehots:
        member = member + jnp.where(hit, 1.0, 0.0)
    carry = carry_ref[...]
    before = _dot(member.astype(BF16), utri_ref[...]) + carry[:, 0:1]

    row4 = lax.broadcasted_iota(I32, (TOP_K, t), 0)
    grow = lax.broadcasted_iota(I32, (V7X_LANES, t), 0)
    eid = jnp.zeros((TOP_K, t), I32)
    rank = jnp.zeros((TOP_K, t), I32)
    gates_t = jnp.zeros((V7X_LANES, t), F32)
    for kk in range(TOP_K):
        idx_k = jnp.sum(jnp.where(onehots[kk], e_iota_f, 0.0), axis=0, keepdims=True)
        rank_k = jnp.sum(jnp.where(onehots[kk], before, 0.0), axis=0, keepdims=True)
        eid = jnp.where(row4 == kk, idx_k.astype(I32), eid)
        rank = jnp.where(row4 == kk, rank_k.astype(I32), rank)
        gates_t = jnp.where(grow == kk, ex[kk] / den, gates_t)
    eid_ref[...] = eid
    rank_ref[...] = rank
    gate_ref[...] = gates_t.T

    new_carry = carry + jnp.sum(member, axis=1, keepdims=True)
    carry_ref[...] = new_carry
    cnt_ref[...] = new_carry


def _route(x2d, g, wrt_hi, wrt_lo, br):
    n, d = x2d.shape
    t = SEQ_TILE
    ii = jnp.arange(t)
    utri = (ii[:, None] < ii[None, :]).astype(BF16)
    c2 = lambda i: (0, 0)
    return pl.pallas_call(
        _route_kernel,
        grid=(n // t,),
        in_specs=[pl.BlockSpec((t, d), lambda i: (i, 0)),
                  pl.BlockSpec((1, d), c2),
                  pl.BlockSpec((N_EXPERTS, d), c2),
                  pl.BlockSpec((N_EXPERTS, d), c2),
                  pl.BlockSpec((N_EXPERTS, 1), c2),
                  pl.BlockSpec((t, t), c2)],
        out_specs=[pl.BlockSpec((t * TOKEN_TILE_ROWS, V7X_LANES), lambda i: (i, 0)),
                   pl.BlockSpec((TOP_K, t), lambda i: (0, i)),
                   pl.BlockSpec((TOP_K, t), lambda i: (0, i)),
                   pl.BlockSpec((t, V7X_LANES), lambda i: (i, 0)),
                   pl.BlockSpec((N_EXPERTS, V7X_LANES), c2)],
        out_shape=[jax.ShapeDtypeStruct((n * TOKEN_TILE_ROWS, V7X_LANES), F32),
                   jax.ShapeDtypeStruct((TOP_K, n), I32),
                   jax.ShapeDtypeStruct((TOP_K, n), I32),
                   jax.ShapeDtypeStruct((n, V7X_LANES), F32),
                   jax.ShapeDtypeStruct((N_EXPERTS, V7X_LANES), F32)],
        scratch_shapes=[pltpu.VMEM((N_EXPERTS, V7X_LANES), F32)],
        compiler_params=_params("arbitrary"),
        name="route",
    )(x2d, g, wrt_hi, wrt_lo, br, utri)


def _plan_kernel(start_ref, eid_ref, rank_ref, dest_ref):
    e = eid_ref[...]
    dest = rank_ref[...]
    for ee in range(N_EXPERTS):
        dest = dest + jnp.where(e == ee, start_ref[ee], 0)
    dest_ref[...] = dest


def _plan(starts, eid, rank):
    n = eid.shape[1]
    t = min(PLAN_TILE, n)
    blk = pl.BlockSpec((TOP_K, t), lambda i, s: (0, i))
    return pl.pallas_call(
        _plan_kernel,
        grid_spec=pltpu.PrefetchScalarGridSpec(
            num_scalar_prefetch=1, grid=(n // t,), in_specs=[blk, blk], out_specs=blk),
        out_shape=jax.ShapeDtypeStruct((TOP_K, n), I32),
        compiler_params=_params("arbitrary"),
        name="plan",
    )(starts, eid, rank)


def _invert_kernel(dest_ref, inv_ref):
    def body(i, carry):
        inv_ref[dest_ref[i]] = i
        return carry

    lax.fori_loop(0, dest_ref.shape[0], body, 0, unroll=32)


def _invert(dest_flat):
    return pl.pallas_call(
        _invert_kernel,
        grid_spec=pltpu.PrefetchScalarGridSpec(
            num_scalar_prefetch=1, grid=(1,), in_specs=[],
            out_specs=pl.BlockSpec(memory_space=pltpu.SMEM)),
        out_shape=jax.ShapeDtypeStruct(dest_flat.shape, I32),
        compiler_params=_params("arbitrary"),
        name="invert",
    )(dest_flat)


_ITEM_VALID, _ITEM_FIRST, _ITEM_NEW_EXPERT, _ITEM_HAS_NEXT_EXPERT = 1, 2, 4, 8


def _expert_kernel(blk_ref, lo_ref, hi_ref, flag_ref, exp_ref, next_exp_ref, inv_ref,
                   h_ref, wgu_hbm, bgu_ref, wdn_hbm, bdn_ref, out_ref,
                   wgu_bf, wdn_bf, wgu_f32, wdn_f32, xs_buf, y_buf, gsem, ssem, wsem,
                   *, n_tok, n_blocks):
    tr = TOKEN_TILE_ROWS
    bm = xs_buf.shape[1] // tr
    i = pl.program_id(0)
    flags = flag_ref[i]
    blk = blk_ref[i]
    slots = xs_buf.shape[0]
    slot = lax.rem(blk, slots)

    def token_tile(ref, row):
        return ref.at[pl.ds(pl.multiple_of(row * tr, tr), tr), :]

    def gather(b, s):
        return [pltpu.make_async_copy(
            token_tile(h_ref, inv_ref[b * bm + r] & (n_tok - 1)),
            xs_buf.at[s, pl.ds(r * tr, tr), :], gsem.at[s]) for r in range(bm)]

    def scatter(b, s):
        return [pltpu.make_async_copy(
            y_buf.at[s, pl.ds(r * tr, tr), :],
            token_tile(out_ref, inv_ref[b * bm + r]), ssem.at[s]) for r in range(bm)]

    def wait_gather(s):
        pltpu.make_async_copy(h_ref.at[pl.ds(0, bm * tr), :], xs_buf.at[s], gsem.at[s]).wait()

    def wait_scatter(s):
        pltpu.make_async_copy(y_buf.at[s], out_ref.at[pl.ds(0, bm * tr), :], ssem.at[s]).wait()

    def start_all(copies):
        for r, c in enumerate(copies):
            c.start(priority=r % 2)

    @pl.when(i == 0)
    def _():
        for b in range(slots - 1):
            start_all(gather(b, b))

    def weight_copies(e):
        return (pltpu.make_async_copy(wgu_hbm.at[e], wgu_f32, wsem.at[0]),
                pltpu.make_async_copy(wdn_hbm.at[e], wdn_f32, wsem.at[1]))

    @pl.when(i == 0)
    def _():
        for c in weight_copies(exp_ref[0]):
            c.start()

    @pl.when((flags & _ITEM_NEW_EXPERT) != 0)
    def _():
        for c in weight_copies(exp_ref[i]):
            c.wait()
        wgu_bf[...] = wgu_f32[...].astype(BF16)
        wdn_bf[...] = wdn_f32[...].astype(BF16)

        @pl.when((flags & _ITEM_HAS_NEXT_EXPERT) != 0)
        def _():
            for c in weight_copies(next_exp_ref[i]):
                c.start()

    bgu = bgu_ref[exp_ref[i]]
    bdn = bdn_ref[exp_ref[i]]

    dyn_slot = (slot,)
    half = bm // 2

    def load_x(r0, n_rows):
        return _load_token_tiles(xs_buf, dyn_slot, n_rows, r0).astype(BF16)

    def ffn_rows(x, r0, merge):
        n_rows = x.shape[0]
        gu = _dot(x, wgu_bf[...]) + bgu
        gate = jnp.minimum(gu[:, :D_FF], SWIGLU_LIMIT)
        up = jnp.clip(gu[:, D_FF:], -SWIGLU_LIMIT, SWIGLU_LIMIT)
        act = gate * jax.nn.sigmoid(SWIGLU_ALPHA * gate) * (up + 1.0)
        yb = _dot(act.astype(BF16), wdn_bf[...]) + bdn
        row = r0 + lax.broadcasted_iota(I32, yb.shape, 0)
        mine = (row >= lo_ref[i]) & (row < hi_ref[i])
        other = _load_token_tiles(y_buf, dyn_slot, n_rows, r0) if merge else 0.0
        return jnp.where(mine, yb, other)

    def first_item(s, has_prev):
        wait_gather(s)
        x_top = load_x(0, half)
        ahead = slots - 1
        start_all(gather(jnp.minimum(blk + ahead, n_blocks - 1), (s + ahead) % slots))
        _store_token_tiles(y_buf, dyn_slot, ffn_rows(x_top, 0, False), 0)
        if has_prev:
            start_all(scatter(blk - 1, (s - 1) % slots))
        _store_token_tiles(y_buf, dyn_slot, ffn_rows(load_x(half, half), half, False), half)

    first = (flags & (_ITEM_VALID | _ITEM_FIRST)) == (_ITEM_VALID | _ITEM_FIRST)
    later = (flags & (_ITEM_VALID | _ITEM_FIRST)) == _ITEM_VALID

    @pl.when(first & (blk == 0))
    def _():
        first_item(0, False)

    @pl.when(first & (blk >= slots))
    def _():
        for s in range(slots):
            @pl.when(slot == s)
            def _():
                wait_scatter(s)

    for s in range(slots):
        @pl.when(first & (slot == s) & (blk > 0))
        def _():
            first_item(s, True)

    @pl.when(later)
    def _():
        _store_token_tiles(y_buf, dyn_slot, ffn_rows(load_x(0, bm), 0, True), 0)

    @pl.when(i == pl.num_programs(0) - 1)
    def _():
        start_all(scatter(n_blocks - 1, (n_blocks - 1) % slots))
        for s in range(slots):
            wait_scatter(s)
        for ahead in range(1, slots):
            wait_gather((n_blocks - 1 + ahead) % slots)


def _expert_items(counts, n_rows):
    bm = EXPERT_ROWS
    n_items = n_rows // bm + N_EXPERTS - 1
    ends = jnp.cumsum(counts)
    starts = ends - counts
    blocks_of = jnp.where(counts > 0, (ends - 1) // bm - starts // bm + 1, 0)
    item_end = jnp.cumsum(blocks_of)
    item_start = item_end - blocks_of
    total = item_end[-1]
    idx = jnp.arange(n_items, dtype=I32)
    valid = idx < total
    idc = jnp.minimum(idx, jnp.maximum(total - 1, 0))
    owner = ((item_start[None, :] <= idc[:, None]) & (idc[:, None] < item_end[None, :])).astype(I32)
    pick = lambda v: jnp.sum(owner * v[None, :], axis=1)
    e = pick(jnp.arange(N_EXPERTS, dtype=I32))
    blk = pick(starts // bm) + idc - pick(item_start)
    lo = jnp.clip(pick(starts) - blk * bm, 0, bm)
    hi = jnp.clip(pick(ends) - blk * bm, 0, bm)
    prev_blk = jnp.concatenate([jnp.full((1,), -1, I32), blk[:-1]])
    prev_e = jnp.concatenate([jnp.full((1,), -1, I32), e[:-1]])
    ids = jnp.arange(N_EXPERTS, dtype=I32)
    owners = jnp.where(counts > 0, ids, N_EXPERTS)
    next_of = jnp.min(jnp.where(ids[None, :] > ids[:, None], owners[None, :], N_EXPERTS), axis=1)
    next_e = pick(next_of)
    new_expert = e != prev_e
    flags = jnp.where(valid,
                      _ITEM_VALID
                      + jnp.where(blk != prev_blk, _ITEM_FIRST, 0)
                      + jnp.where(new_expert, _ITEM_NEW_EXPERT, 0)
                      + jnp.where(new_expert & (next_e < N_EXPERTS), _ITEM_HAS_NEXT_EXPERT, 0), 0)
    as_i32 = lambda v: v.astype(I32)
    return (as_i32(blk), as_i32(lo), as_i32(hi), as_i32(flags), as_i32(e),
            as_i32(jnp.minimum(next_e, N_EXPERTS - 1)))


def _experts(items, inv, h, wgu, bgu, wdn, bdn):
    tr, d = TOKEN_TILE_ROWS, D_MODEL
    n = h.shape[0] // tr
    bm = EXPERT_ROWS
    n_rows = inv.shape[0]
    n_blocks = n_rows // bm
    assert n_rows % bm == 0 and n_blocks >= EXPERT_SLOTS and n & (n - 1) == 0
    blk, lo, hi, flags, e, next_e = items
    whole = lambda i, *prefetch: (0, 0, 0)
    return pl.pallas_call(
        functools.partial(_expert_kernel, n_tok=n, n_blocks=n_blocks),
        grid_spec=pltpu.PrefetchScalarGridSpec(
            num_scalar_prefetch=7, grid=(blk.shape[0],),
            in_specs=[pl.BlockSpec(memory_space=pl.ANY),
                      pl.BlockSpec(memory_space=pl.ANY),
                      pl.BlockSpec((N_EXPERTS, 1, 2 * D_FF), whole),
                      pl.BlockSpec(memory_space=pl.ANY),
                      pl.BlockSpec((N_EXPERTS, 1, d), whole)],
            out_specs=pl.BlockSpec(memory_space=pl.ANY),
            scratch_shapes=[pltpu.VMEM((d, 2 * D_FF), BF16),
                            pltpu.VMEM((D_FF, d), BF16),
                            pltpu.VMEM((d, 2 * D_FF), F32),
                            pltpu.VMEM((D_FF, d), F32),
                            pltpu.VMEM((EXPERT_SLOTS, bm * tr, V7X_LANES), F32),
                            pltpu.VMEM((EXPERT_SLOTS, bm * tr, V7X_LANES), F32),
                            pltpu.SemaphoreType.DMA((EXPERT_SLOTS,)),
                            pltpu.SemaphoreType.DMA((EXPERT_SLOTS,)),
                            pltpu.SemaphoreType.DMA((2,))]),
        out_shape=jax.ShapeDtypeStruct((n_rows * tr, V7X_LANES), F32),
        compiler_params=_params("arbitrary"),
        name="experts",
    )(blk, lo, hi, flags, e, next_e, inv, h, wgu, bgu, wdn, bdn)


def _combine_kernel(x_ref, gate_ref, g_ref, y_ref, o_ref):
    gates = gate_ref[...]
    moe = jnp.zeros(x_ref.shape, F32)
    for kk in range(TOP_K):
        moe = moe + _load_token_tiles(y_ref, (kk,), x_ref.shape[0]) * gates[:, kk:kk + 1]
    o_ref[...] = _rms(x_ref[...] + moe, g_ref[...])


def _combine(x2d, gates, g, y):
    n, d = x2d.shape
    t = min(COMBINE_TILE, n)
    return pl.pallas_call(
        _combine_kernel,
        grid=(n // t,),
        in_specs=[pl.BlockSpec((t, d), lambda i: (i, 0)),
                  pl.BlockSpec((t, V7X_LANES), lambda i: (i, 0)),
                  pl.BlockSpec((1, d), lambda i: (0, 0)),
                  pl.BlockSpec((TOP_K, t * TOKEN_TILE_ROWS, V7X_LANES), lambda i: (0, i, 0))],
        out_specs=pl.BlockSpec((t, d), lambda i: (i, 0)),
        out_shape=jax.ShapeDtypeStruct((n, d), F32),
        compiler_params=_params("arbitrary"),
        name="combine",
    )(x2d, gates, g, y.reshape(TOP_K, n * TOKEN_TILE_ROWS, V7X_LANES))


def _layer(x, mem, norm_mix_g, w_in, w_pool, pool_scale, w_gate_up, b_gate_up, gla_norm_g, w_out,
           norm_xattn_g, norm_mem_g, w_xq, w_xk, w_xv, w_xo,
           norm_moe_g, w_router, b_router, w_gu, b_gu, w_dn, b_dn, out_g):
    b, s, d = x.shape
    n = b * s
    row = lambda a: a.reshape(1, -1)

    s0, s1, s2, s3, s4 = (POOL_W, POOL_W + GLA_KEY, POOL_W + 2 * GLA_KEY,
                          POOL_W + 2 * GLA_KEY + GLA_VAL, POOL_W + 2 * GLA_KEY + GLA_VAL + GATE_RANK)
    win = jnp.concatenate(
        [w_in[:, :s3], w_in[:, s4:], w_in[:, s3:s4],
         jnp.zeros((d, V7X_LANES - GATE_RANK), w_in.dtype)], axis=1).astype(BF16)
    wgate = jnp.concatenate(
        [w_gate_up, jnp.zeros((V7X_LANES - GATE_RANK, GLA_KEY), w_gate_up.dtype)], axis=0).astype(BF16)
    wrt = w_router.T
    wrt_hi = wrt.astype(BF16)
    wrt_lo = (wrt - wrt_hi.astype(F32)).astype(BF16)

    kmem, vmem = _memkv(mem, row(norm_mem_g), w_xk.astype(BF16), w_xv.astype(BF16))
    x1 = _mix(x, row(norm_mix_g), win, w_pool.astype(BF16), row(pool_scale), wgate,
              row(b_gate_up), row(gla_norm_g), w_out.astype(BF16))
    x2 = _xattn(x1, row(norm_xattn_g), w_xq.astype(BF16), kmem, vmem, w_xo.astype(BF16))
    x2 = x2.reshape(n, d)

    h, eid, rank, gates, cnt = _route(x2, row(norm_moe_g), wrt_hi, wrt_lo, b_router.reshape(-1, 1))
    counts = cnt[:, 0].astype(I32)
    starts = (jnp.cumsum(counts) - counts).astype(I32)
    dest = _plan(starts, eid, rank).reshape(TOP_K * n)
    items = _expert_items(counts, TOP_K * n)
    y = _experts(items, _invert(dest), h, w_gu, b_gu.reshape(N_EXPERTS, 1, -1),
                 w_dn, b_dn.reshape(N_EXPERTS, 1, -1))
    out = _combine(x2, gates, row(out_g), y)
    return out.reshape(b, s, d)


def kernel(x, mem, norm_mix_g, w_in, w_pool, pool_scale, w_gate_up, b_gate_up, gla_norm_g, w_out,
           norm_xattn_g, norm_mem_g, w_xq, w_xk, w_xv, w_xo, norm_moe_g, w_router, b_router,
           w_gu, b_gu, w_dn, b_dn, norm_final_g):
    depth = norm_mix_g.shape[0]
    assert depth == 1, "the final rmsnorm is fused into the (single) layer's combine stage"
    return _layer(x, mem, norm_mix_g[0], w_in[0], w_pool[0], pool_scale[0], w_gate_up[0],
                  b_gate_up[0], gla_norm_g[0], w_out[0], norm_xattn_g[0], norm_mem_g[0],
                  w_xq[0], w_xk[0], w_xv[0], w_xo[0], norm_moe_g[0], w_router[0], b_router[0],
                  w_gu[0], b_gu[0], w_dn[0], b_dn[0], norm_final_g)
```

```python
import functools

import jax
import jax.numpy as jnp
from jax import lax
from jax.experimental import pallas as pl
from jax.experimental.pallas import tpu as pltpu

F32 = jnp.float32
BF16 = jnp.bfloat16
I32 = jnp.int32

V7X_LANES = 128
V7X_VMEM_LIMIT_BYTES = 56 * 1024 * 1024

D_MODEL = 1024
POOL_WINDOWS = (2, 4, 8, 16)
POOL_GROUP_W = 128
POOL_W = len(POOL_WINDOWS) * POOL_GROUP_W
GLA_HEADS = 4
GLA_DK = 64
GLA_DV = 128
GLA_KEY = GLA_HEADS * GLA_DK
GLA_VAL = GLA_HEADS * GLA_DV
GATE_RANK = 16
GATE_NORMALIZER = 16.0
CHUNK = 64
XATTN_HEADS = 4
XATTN_HD = D_MODEL // XATTN_HEADS
N_EXPERTS = 32
TOP_K = 4
D_FF = D_MODEL
SWIGLU_LIMIT = 7.0
SWIGLU_ALPHA = 1.702
EPS = 1e-6

_U0 = 0
_Q0 = _U0 + POOL_W
_K0 = _Q0 + GLA_KEY
_V0 = _K0 + GLA_KEY
_R0 = _V0 + GLA_VAL
_G0 = _R0 + GLA_VAL
IN_COLS_PADDED = _G0 + V7X_LANES

SEQ_TILE = 512
POOL_TAIL = 16
PLAN_TILE = 2048
EXPERT_ROWS = 256
EXPERT_SLOTS = 3
COMBINE_TILE = 256


def _rms(x, g):
    ms = jnp.mean(x * x, axis=-1, keepdims=True)
    return x * lax.rsqrt(ms + EPS) * g


def _dot(a, b):
    return jnp.dot(a, b, preferred_element_type=F32)


def _dot_nt(a, b):
    return lax.dot_general(a, b, (((1,), (1,)), ((), ())), preferred_element_type=F32)


def _dot_tn(a, b):
    return lax.dot_general(a, b, (((0,), (0,)), ((), ())), preferred_element_type=F32)


TOKEN_TILE_ROWS = D_MODEL // V7X_LANES


def _load_token_tiles(ref, lead, n_rows, first_row=0):
    start = first_row * TOKEN_TILE_ROWS
    chunks = [ref[lead + (pl.ds(start + c, n_rows, stride=TOKEN_TILE_ROWS), slice(None))]
              for c in range(TOKEN_TILE_ROWS)]
    return jnp.concatenate(chunks, axis=1)


def _store_token_tiles(ref, lead, val, first_row=0):
    start = first_row * TOKEN_TILE_ROWS
    for c in range(TOKEN_TILE_ROWS):
        ref[lead + (pl.ds(start + c, val.shape[0], stride=TOKEN_TILE_ROWS), slice(None))] = (
            val[:, c * V7X_LANES:(c + 1) * V7X_LANES])


def _params(*semantics):
    return pltpu.CompilerParams(dimension_semantics=semantics,
                                vmem_limit_bytes=V7X_VMEM_LIMIT_BYTES)


def _memkv_kernel(mem_ref, g_ref, wk_ref, wv_ref, k_ref, v_ref):
    hm = _rms(mem_ref[0], g_ref[...]).astype(BF16)
    k = _dot(hm, wk_ref[...])
    v = _dot(hm, wv_ref[...])
    k_ref[0] = (k * (XATTN_HD ** -0.5)).astype(BF16)
    v_ref[0] = v.astype(BF16)


def _memkv(mem, g, wk, wv):
    b, m, d = mem.shape
    const = lambda i: (0, 0)
    return pl.pallas_call(
        _memkv_kernel,
        grid=(b,),
        in_specs=[pl.BlockSpec((1, m, d), lambda i: (i, 0, 0)),
                  pl.BlockSpec((1, d), const),
                  pl.BlockSpec((d, d), const),
                  pl.BlockSpec((d, d), const)],
        out_specs=[pl.BlockSpec((1, m, d), lambda i: (i, 0, 0)),
                   pl.BlockSpec((1, m, d), lambda i: (i, 0, 0))],
        out_shape=[jax.ShapeDtypeStruct((b, m, d), BF16)] * 2,
        compiler_params=_params("arbitrary"),
        name="memkv",
    )(mem, g, wk, wv)


def _mix_kernel(x_ref, g_ref, win_ref, wpool_ref, pscale_ref, wgate_ref, bgate_ref,
                gnorm_ref, wout_ref, ltri_ref, o_ref, uprev_ref, state_ref, oacc_ref):
    t = x_ref.shape[1]
    j = pl.program_id(1)

    @pl.when(j == 0)
    def _():
        uprev_ref[...] = jnp.zeros_like(uprev_ref)
        state_ref[...] = jnp.zeros_like(state_ref)

    x = x_ref[0]
    h = _rms(x, g_ref[...]).astype(BF16)
    def project(c0, width):
        return _dot(h, win_ref[:, c0:c0 + width])

    glr = project(_G0, V7X_LANES)
    u = project(_U0, POOL_W)
    gp = _dot(glr.astype(BF16), wgate_ref[...]) + bgate_ref[...]
    q = project(_Q0, GLA_KEY)
    k = project(_K0, GLA_KEY)
    g = jax.nn.log_sigmoid(gp) / GATE_NORMALIZER
    g_hi = g.astype(BF16)
    g_lo = (g - g_hi.astype(F32)).astype(BF16)
    gc = _dot(ltri_ref[...], g_hi) + _dot(ltri_ref[...], g_lo)
    v = project(_V0, GLA_VAL)
    r = project(_R0, GLA_VAL)

    u_ext = jnp.concatenate([uprev_ref[...], u], axis=0)
    uprev_ref[...] = u[t - POOL_TAIL:, :]
    row = lax.broadcasted_iota(I32, (t, POOL_GROUP_W), 0)
    pos = (j * t + row + 1).astype(F32)
    pooled = []
    for gi, w in enumerate(POOL_WINDOWS):
        cols = slice(gi * POOL_GROUP_W, (gi + 1) * POOL_GROUP_W)
        wsum = u_ext[:, cols]
        shift = 1
        while shift < w:
            wsum = wsum + pltpu.roll(wsum, shift, axis=0)
            shift *= 2
        p = wsum[POOL_TAIL:, :] / jnp.minimum(pos, float(w)) - u[:, cols]
        pooled.append(_dot(p.astype(BF16), wpool_ref[gi]))
    pool = jnp.concatenate(pooled, axis=1) * pscale_ref[...]

    qe = q * (GLA_DK ** -0.5) * jnp.exp(gc)
    ke = k * jnp.exp(-gc)

    lane = lax.broadcasted_iota(I32, (CHUNK, V7X_LANES), 1)
    first_head = lane < GLA_DK
    causal = (lax.broadcasted_iota(I32, (CHUNK, CHUNK), 0)
              >= lax.broadcasted_iota(I32, (CHUNK, CHUNK), 1))
    srow = lax.broadcasted_iota(I32, (2 * GLA_DV, V7X_LANES), 0)
    slane = lax.broadcasted_iota(I32, (2 * GLA_DV, V7X_LANES), 1)
    same_head = (srow < GLA_DV) == (slane < GLA_DK)

    states = [state_ref[0], state_ref[1]]
    for c in range(t // CHUNK):
        rows = slice(c * CHUNK, (c + 1) * CHUNK)
        g_last = gc[c * CHUNK + CHUNK - 1:c * CHUNK + CHUNK, :]
        k2 = k[rows] * jnp.exp(g_last - gc[rows])
        dec = jnp.exp(g_last)
        for p in range(GLA_HEADS // 2):
            lanes = slice(p * V7X_LANES, (p + 1) * V7X_LANES)
            vals = slice(p * 2 * GLA_DV, (p + 1) * 2 * GLA_DV)
            qe_p = qe[rows, lanes]
            ke_p = ke[rows, lanes].astype(BF16)
            v_p = v[rows, vals].astype(BF16)
            s_prev = states[p]
            outs = []
            for a in range(2):
                q_a = jnp.where(first_head if a == 0 else ~first_head, qe_p, 0.0)
                att = jnp.where(causal, _dot_nt(q_a.astype(BF16), ke_p), 0.0)
                outs.append(_dot(att.astype(BF16), v_p[:, a * GLA_DV:(a + 1) * GLA_DV]))
            inter = _dot_nt(qe_p.astype(BF16), s_prev.astype(BF16))
            oacc_ref[rows, vals] = jnp.concatenate(outs, axis=1) + inter
            d_state = _dot_tn(v_p, k2[:, lanes].astype(BF16))
            states[p] = s_prev * dec[:, lanes] + jnp.where(same_head, d_state, 0.0)
    state_ref[0] = states[0]
    state_ref[1] = states[1]

    o = oacc_ref[...]
    gla = []
    for hh in range(GLA_HEADS):
        cols = slice(hh * GLA_DV, (hh + 1) * GLA_DV)
        oh = o[:, cols]
        on = oh * lax.rsqrt(jnp.mean(oh * oh, axis=-1, keepdims=True) + EPS) * gnorm_ref[...]
        gla.append(on * jax.nn.silu(r[:, cols]))
    mix = jnp.concatenate([pool] + gla, axis=1).astype(BF16)
    o_ref[0] = x + _dot(mix, wout_ref[...])


def _mix(x, g, win, wpool, pscale, wgate, bgate, gnorm, wout):
    b, s, d = x.shape
    t = SEQ_TILE
    assert s % t == 0 and t % CHUNK == 0 and t >= POOL_TAIL >= max(POOL_WINDOWS) - 1
    ii = jnp.arange(t)
    ltri = ((ii[:, None] // CHUNK == ii[None, :] // CHUNK)
            & (ii[None, :] <= ii[:, None])).astype(BF16)
    c2 = lambda bi, j: (0, 0)
    c3 = lambda bi, j: (0, 0, 0)
    return pl.pallas_call(
        _mix_kernel,
        grid=(b, s // t),
        in_specs=[pl.BlockSpec((1, t, d), lambda bi, j: (bi, j, 0)),
                  pl.BlockSpec((1, d), c2),
                  pl.BlockSpec((d, IN_COLS_PADDED), c2),
                  pl.BlockSpec((len(POOL_WINDOWS), POOL_GROUP_W, POOL_GROUP_W), c3),
                  pl.BlockSpec((1, POOL_W), c2),
                  pl.BlockSpec((V7X_LANES, GLA_KEY), c2),
                  pl.BlockSpec((1, GLA_KEY), c2),
                  pl.BlockSpec((1, GLA_DV), c2),
                  pl.BlockSpec((POOL_W + GLA_VAL, d), c2),
                  pl.BlockSpec((t, t), c2)],
        out_specs=pl.BlockSpec((1, t, d), lambda bi, j: (bi, j, 0)),
        out_shape=jax.ShapeDtypeStruct((b, s, d), F32),
        scratch_shapes=[pltpu.VMEM((POOL_TAIL, POOL_W), F32),
                        pltpu.VMEM((GLA_HEADS // 2, 2 * GLA_DV, V7X_LANES), F32),
                        pltpu.VMEM((t, GLA_VAL), F32)],
        compiler_params=_params("arbitrary", "arbitrary"),
        name="mix",
    )(x, g, win, wpool, pscale, wgate, bgate, gnorm, wout, ltri)


def _xattn_kernel(x_ref, g_ref, wq_ref, k_ref, v_ref, wo_ref, o_ref):
    x = x_ref[0]
    h = _rms(x, g_ref[...]).astype(BF16)
    q = _dot(h, wq_ref[...])
    heads = []
    for hh in range(XATTN_HEADS):
        cols = slice(hh * XATTN_HD, (hh + 1) * XATTN_HD)
        sc = _dot_nt(q[:, cols].astype(BF16), k_ref[0][:, cols])
        e = jnp.exp(sc - jnp.max(sc, axis=-1, keepdims=True))
        p = e / jnp.sum(e, axis=-1, keepdims=True)
        heads.append(_dot(p.astype(BF16), v_ref[0][:, cols]))
    o = jnp.concatenate(heads, axis=1).astype(BF16)
    o_ref[0] = x + _dot(o, wo_ref[...])


def _xattn(x, g, wq, kmem, vmem, wo):
    b, s, d = x.shape
    m = kmem.shape[1]
    t = SEQ_TILE
    c2 = lambda bi, j: (0, 0)
    return pl.pallas_call(
        _xattn_kernel,
        grid=(b, s // t),
        in_specs=[pl.BlockSpec((1, t, d), lambda bi, j: (bi, j, 0)),
                  pl.BlockSpec((1, d), c2),
                  pl.BlockSpec((d, d), c2),
                  pl.BlockSpec((1, m, d), lambda bi, j: (bi, 0, 0)),
                  pl.BlockSpec((1, m, d), lambda bi, j: (bi, 0, 0)),
                  pl.BlockSpec((d, d), c2)],
        out_specs=pl.BlockSpec((1, t, d), lambda bi, j: (bi, j, 0)),
        out_shape=jax.ShapeDtypeStruct((b, s, d), F32),
        compiler_params=_params("arbitrary", "arbitrary"),
        name="xattn",
    )(x, g, wq, kmem, vmem, wo)


def _route_kernel(x_ref, g_ref, wrt_hi_ref, wrt_lo_ref, br_ref, utri_ref,
                  h_ref, eid_ref, rank_ref, gate_ref, cnt_ref, carry_ref):
    t = x_ref.shape[0]

    @pl.when(pl.program_id(0) == 0)
    def _():
        carry_ref[...] = jnp.zeros_like(carry_ref)

    h = _rms(x_ref[...], g_ref[...])
    _store_token_tiles(h_ref, (), h)
    h_hi = h.astype(BF16)
    h_lo = (h - h_hi.astype(F32)).astype(BF16)
    logits = (_dot_nt(wrt_hi_ref[...], h_hi) + _dot_nt(wrt_hi_ref[...], h_lo)
              + _dot_nt(wrt_lo_ref[...], h_hi)) + br_ref[...]

    e_iota = lax.broadcasted_iota(I32, (N_EXPERTS, t), 0)
    e_iota_f = e_iota.astype(F32)
    work = logits
    vals, onehots = [], []
    for _ in range(TOP_K):
        m = jnp.max(work, axis=0, keepdims=True)
        idx = jnp.min(jnp.where(work == m, e_iota_f, float(N_EXPERTS)), axis=0, keepdims=True)
        hit = e_iota_f == idx
        vals.append(m)
        onehots.append(hit)
        work = jnp.where(hit, -jnp.inf, work)

    ex = [jnp.exp(vk - vals[0]) for vk in vals]
    den = ex[0] + ex[1] + ex[2] + ex[3]

    member = jnp.zeros((N_EXPERTS, t), F32)
    for hit in onehots:
        member = member + jnp.where(hit, 1.0, 0.0)
    carry = carry_ref[...]
    before = _dot(member.astype(BF16), utri_ref[...]) + carry[:, 0:1]

    row4 = lax.broadcasted_iota(I32, (TOP_K, t), 0)
    grow = lax.broadcasted_iota(I32, (V7X_LANES, t), 0)
    eid = jnp.zeros((TOP_K, t), I32)
    rank = jnp.zeros((TOP_K, t), I32)
    gates_t = jnp.zeros((V7X_LANES, t), F32)
    for kk in range(TOP_K):
        idx_k = jnp.sum(jnp.where(onehots[kk], e_iota_f, 0.0), axis=0, keepdims=True)
        rank_k = jnp.sum(jnp.where(onehots[kk], before, 0.0), axis=0, keepdims=True)
        eid = jnp.where(row4 == kk, idx_k.astype(I32), eid)
        rank = jnp.where(row4 == kk, rank_k.astype(I32), rank)
        gates_t = jnp.where(grow == kk, ex[kk] / den, gates_t)
    eid_ref[...] = eid
    rank_ref[...] = rank
    gate_ref[...] = gates_t.T

    new_carry = carry + jnp.sum(member, axis=1, keepdims=True)
    carry_ref[...] = new_carry
    cnt_ref[...] = new_carry


def _route(x2d, g, wrt_hi, wrt_lo, br):
    n, d = x2d.shape
    t = SEQ_TILE
    ii = jnp.arange(t)
    utri = (ii[:, None] < ii[None, :]).astype(BF16)
    c2 = lambda i: (0, 0)
    return pl.pallas_call(
        _route_kernel,
        grid=(n // t,),
        in_specs=[pl.BlockSpec((t, d), lambda i: (i, 0)),
                  pl.BlockSpec((1, d), c2),
                  pl.BlockSpec((N_EXPERTS, d), c2),
                  pl.BlockSpec((N_EXPERTS, d), c2),
                  pl.BlockSpec((N_EXPERTS, 1), c2),
                  pl.BlockSpec((t, t), c2)],
        out_specs=[pl.BlockSpec((t * TOKEN_TILE_ROWS, V7X_LANES), lambda i: (i, 0)),
                   pl.BlockSpec((TOP_K, t), lambda i: (0, i)),
                   pl.BlockSpec((TOP_K, t), lambda i: (0, i)),
                   pl.BlockSpec((t, V7X_LANES), lambda i: (i, 0)),
                   pl.BlockSpec((N_EXPERTS, V7X_LANES), c2)],
        out_shape=[jax.ShapeDtypeStruct((n * TOKEN_TILE_ROWS, V7X_LANES), F32),
                   jax.ShapeDtypeStruct((TOP_K, n), I32),
                   jax.ShapeDtypeStruct((TOP_K, n), I32),
                   jax.ShapeDtypeStruct((n, V7X_LANES), F32),
                   jax.ShapeDtypeStruct((N_EXPERTS, V7X_LANES), F32)],
        scratch_shapes=[pltpu.VMEM((N_EXPERTS, V7X_LANES), F32)],
        compiler_params=_params("arbitrary"),
        name="route",
    )(x2d, g, wrt_hi, wrt_lo, br, utri)


def _plan_kernel(start_ref, eid_ref, rank_ref, dest_ref):
    e = eid_ref[...]
    dest = rank_ref[...]
    for ee in range(N_EXPERTS):
        dest = dest + jnp.where(e == ee, start_ref[ee], 0)
    dest_ref[...] = dest


def _plan(starts, eid, rank):
    n = eid.shape[1]
    t = min(PLAN_TILE, n)
    blk = pl.BlockSpec((TOP_K, t), lambda i, s: (0, i))
    return pl.pallas_call(
        _plan_kernel,
        grid_spec=pltpu.PrefetchScalarGridSpec(
            num_scalar_prefetch=1, grid=(n // t,), in_specs=[blk, blk], out_specs=blk),
        out_shape=jax.ShapeDtypeStruct((TOP_K, n), I32),
        compiler_params=_params("arbitrary"),
        name="plan",
    )(starts, eid, rank)


def _invert_kernel(dest_ref, inv_ref):
    def body(i, carry):
        inv_ref[dest_ref[i]] = i * TOKEN_TILE_ROWS
        return carry

    lax.fori_loop(0, dest_ref.shape[0], body, 0, unroll=32)


def _invert(dest_flat):
    return pl.pallas_call(
        _invert_kernel,
        grid_spec=pltpu.PrefetchScalarGridSpec(
            num_scalar_prefetch=1, grid=(1,), in_specs=[],
            out_specs=pl.BlockSpec(memory_space=pltpu.SMEM)),
        out_shape=jax.ShapeDtypeStruct(dest_flat.shape, I32),
        compiler_params=_params("arbitrary"),
        name="invert",
    )(dest_flat)


_ITEM_VALID, _ITEM_FIRST, _ITEM_NEW_EXPERT, _ITEM_HAS_NEXT_EXPERT = 1, 2, 4, 8


def _expert_kernel(blk_ref, lo_ref, hi_ref, flag_ref, exp_ref, next_exp_ref, inv_ref,
                   h_ref, wgu_hbm, bgu_ref, wdn_hbm, bdn_ref, out_ref,
                   wgu_bf, wdn_bf, wgu_f32, wdn_f32, xs_buf, y_buf, gsem, ssem, wsem,
                   *, n_tok, n_blocks):
    tr = TOKEN_TILE_ROWS
    bm = xs_buf.shape[1] // tr
    i = pl.program_id(0)
    flags = flag_ref[i]
    blk = blk_ref[i]
    slots = xs_buf.shape[0]
    slot = lax.rem(blk, slots)

    def token_tile(ref, first_row):
        return ref.at[pl.ds(pl.multiple_of(first_row, tr), tr), :]

    def gather(b, s):
        return [pltpu.make_async_copy(
            token_tile(h_ref, inv_ref[b * bm + r] & (n_tok * tr - 1)),
            xs_buf.at[s, pl.ds(r * tr, tr), :], gsem.at[s]) for r in range(bm)]

    def scatter(b, s):
        return [pltpu.make_async_copy(
            y_buf.at[s, pl.ds(r * tr, tr), :],
            token_tile(out_ref, inv_ref[b * bm + r]), ssem.at[s]) for r in range(bm)]

    def wait_gather(s):
        pltpu.make_async_copy(h_ref.at[pl.ds(0, bm * tr), :], xs_buf.at[s], gsem.at[s]).wait()

    def wait_scatter(s):
        pltpu.make_async_copy(y_buf.at[s], out_ref.at[pl.ds(0, bm * tr), :], ssem.at[s]).wait()

    def start_all(copies):
        for r, c in enumerate(copies):
            c.start(priority=r % 2)

    @pl.when(i == 0)
    def _():
        for b in range(slots - 1):
            start_all(gather(b, b))

    def weight_copies(e):
        return (pltpu.make_async_copy(wgu_hbm.at[e], wgu_f32, wsem.at[0]),
                pltpu.make_async_copy(wdn_hbm.at[e], wdn_f32, wsem.at[1]))

    @pl.when(i == 0)
    def _():
        for c in weight_copies(exp_ref[0]):
            c.start()

    @pl.when((flags & _ITEM_NEW_EXPERT) != 0)
    def _():
        for c in weight_copies(exp_ref[i]):
            c.wait()
        wgu_bf[...] = wgu_f32[...].astype(BF16)
        wdn_bf[...] = wdn_f32[...].astype(BF16)

        @pl.when((flags & _ITEM_HAS_NEXT_EXPERT) != 0)
        def _():
            for c in weight_copies(next_exp_ref[i]):
                c.start()

    bgu = bgu_ref[exp_ref[i]]
    bdn = bdn_ref[exp_ref[i]]

    dyn_slot = (slot,)
    half = bm // 2

    def load_x(r0, n_rows):
        return _load_token_tiles(xs_buf, dyn_slot, n_rows, r0).astype(BF16)

    def ffn_rows(x, r0, merge):
        n_rows = x.shape[0]
        gu = _dot(x, wgu_bf[...]) + bgu
        gate = jnp.minimum(gu[:, :D_FF], SWIGLU_LIMIT)
        up = jnp.clip(gu[:, D_FF:], -SWIGLU_LIMIT, SWIGLU_LIMIT)
        act = gate * jax.nn.sigmoid(SWIGLU_ALPHA * gate) * (up + 1.0)
        yb = _dot(act.astype(BF16), wdn_bf[...]) + bdn
        row = r0 + lax.broadcasted_iota(I32, yb.shape, 0)
        mine = (row >= lo_ref[i]) & (row < hi_ref[i])
        other = _load_token_tiles(y_buf, dyn_slot, n_rows, r0) if merge else 0.0
        return jnp.where(mine, yb, other)

    def first_item(s, has_prev):
        wait_gather(s)
        x_top = load_x(0, half)
        ahead = slots - 1
        start_all(gather(jnp.minimum(blk + ahead, n_blocks - 1), (s + ahead) % slots))
        _store_token_tiles(y_buf, dyn_slot, ffn_rows(x_top, 0, False), 0)
        if has_prev:
            start_all(scatter(blk - 1, (s - 1) % slots))
        _store_token_tiles(y_buf, dyn_slot, ffn_rows(load_x(half, half), half, False), half)

    first = (flags & (_ITEM_VALID | _ITEM_FIRST)) == (_ITEM_VALID | _ITEM_FIRST)
    later = (flags & (_ITEM_VALID | _ITEM_FIRST)) == _ITEM_VALID

    @pl.when(first & (blk == 0))
    def _():
        first_item(0, False)

    @pl.when(first & (blk >= slots))
    def _():
        for s in range(slots):
            @pl.when(slot == s)
            def _():
                wait_scatter(s)

    for s in range(slots):
        @pl.when(first & (slot == s) & (blk > 0))
        def _():
            first_item(s, True)

    for r0 in (0, half):
        @pl.when(later & (lo_ref[i] < r0 + half) & (hi_ref[i] > r0))
        def _():
            _store_token_tiles(y_buf, dyn_slot, ffn_rows(load_x(r0, half), r0, True), r0)

    @pl.when(i == pl.num_programs(0) - 1)
    def _():
        start_all(scatter(n_blocks - 1, (n_blocks - 1) % slots))
        for s in range(slots):
            wait_scatter(s)
        for ahead in range(1, slots):
            wait_gather((n_blocks - 1 + ahead) % slots)


def _expert_items(counts, n_rows):
    bm = EXPERT_ROWS
    n_items = n_rows // bm + N_EXPERTS - 1
    ends = jnp.cumsum(counts)
    starts = ends - counts
    blocks_of = jnp.where(counts > 0, (ends - 1) // bm - starts // bm + 1, 0)
    item_end = jnp.cumsum(blocks_of)
    item_start = item_end - blocks_of
    total = item_end[-1]
    idx = jnp.arange(n_items, dtype=I32)
    valid = idx < total
    idc = jnp.minimum(idx, jnp.maximum(total - 1, 0))
    owner = ((item_start[None, :] <= idc[:, None]) & (idc[:, None] < item_end[None, :])).astype(I32)
    pick = lambda v: jnp.sum(owner * v[None, :], axis=1)
    e = pick(jnp.arange(N_EXPERTS, dtype=I32))
    blk = pick(starts // bm) + idc - pick(item_start)
    lo = jnp.clip(pick(starts) - blk * bm, 0, bm)
    hi = jnp.clip(pick(ends) - blk * bm, 0, bm)
    prev_blk = jnp.concatenate([jnp.full((1,), -1, I32), blk[:-1]])
    prev_e = jnp.concatenate([jnp.full((1,), -1, I32), e[:-1]])
    ids = jnp.arange(N_EXPERTS, dtype=I32)
    owners = jnp.where(counts > 0, ids, N_EXPERTS)
    next_of = jnp.min(jnp.where(ids[None, :] > ids[:, None], owners[None, :], N_EXPERTS), axis=1)
    next_e = pick(next_of)
    new_expert = e != prev_e
    flags = jnp.where(valid,
                      _ITEM_VALID
                      + jnp.where(blk != prev_blk, _ITEM_FIRST, 0)
                      + jnp.where(new_expert, _ITEM_NEW_EXPERT, 0)
                      + jnp.where(new_expert & (next_e < N_EXPERTS), _ITEM_HAS_NEXT_EXPERT, 0), 0)
    as_i32 = lambda v: v.astype(I32)
    return (as_i32(blk), as_i32(lo), as_i32(hi), as_i32(flags), as_i32(e),
            as_i32(jnp.minimum(next_e, N_EXPERTS - 1)))


def _experts(items, inv, h, wgu, bgu, wdn, bdn):
    tr, d = TOKEN_TILE_ROWS, D_MODEL
    n = h.shape[0] // tr
    bm = EXPERT_ROWS
    n_rows = inv.shape[0]
    n_blocks = n_rows // bm
    assert n_rows % bm == 0 and n_blocks >= EXPERT_SLOTS and n & (n - 1) == 0
    blk, lo, hi, flags, e, next_e = items
    whole = lambda i, *prefetch: (0, 0, 0)
    return pl.pallas_call(
        functools.partial(_expert_kernel, n_tok=n, n_blocks=n_blocks),
        grid_spec=pltpu.PrefetchScalarGridSpec(
            num_scalar_prefetch=7, grid=(blk.shape[0],),
            in_specs=[pl.BlockSpec(memory_space=pl.ANY),
                      pl.BlockSpec(memory_space=pl.ANY),
                      pl.BlockSpec((N_EXPERTS, 1, 2 * D_FF), whole),
                      pl.BlockSpec(memory_space=pl.ANY),
                      pl.BlockSpec((N_EXPERTS, 1, d), whole)],
            out_specs=pl.BlockSpec(memory_space=pl.ANY),
            scratch_shapes=[pltpu.VMEM((d, 2 * D_FF), BF16),
                            pltpu.VMEM((D_FF, d), BF16),
                            pltpu.VMEM((d, 2 * D_FF), F32),
                            pltpu.VMEM((D_FF, d), F32),
                            pltpu.VMEM((EXPERT_SLOTS, bm * tr, V7X_LANES), F32),
                            pltpu.VMEM((EXPERT_SLOTS, bm * tr, V7X_LANES), F32),
                            pltpu.SemaphoreType.DMA((EXPERT_SLOTS,)),
                            pltpu.SemaphoreType.DMA((EXPERT_SLOTS,)),
                            pltpu.SemaphoreType.DMA((2,))]),
        out_shape=jax.ShapeDtypeStruct((n_rows * tr, V7X_LANES), F32),
        compiler_params=_params("arbitrary"),
        name="experts",
    )(blk, lo, hi, flags, e, next_e, inv, h, wgu, bgu, wdn, bdn)


def _combine_kernel(x_ref, gate_ref, g_ref, y_ref, o_ref):
    gates = gate_ref[...]
    moe = jnp.zeros(x_ref.shape, F32)
    for kk in range(TOP_K):
        moe = moe + _load_token_tiles(y_ref, (kk,), x_ref.shape[0]) * gates[:, kk:kk + 1]
    o_ref[...] = _rms(x_ref[...] + moe, g_ref[...])


def _combine(x2d, gates, g, y):
    n, d = x2d.shape
    t = min(COMBINE_TILE, n)
    return pl.pallas_call(
        _combine_kernel,
        grid=(n // t,),
        in_specs=[pl.BlockSpec((t, d), lambda i: (i, 0)),
                  pl.BlockSpec((t, V7X_LANES), lambda i: (i, 0)),
                  pl.BlockSpec((1, d), lambda i: (0, 0)),
                  pl.BlockSpec((TOP_K, t * TOKEN_TILE_ROWS, V7X_LANES), lambda i: (0, i, 0))],
        out_specs=pl.BlockSpec((t, d), lambda i: (i, 0)),
        out_shape=jax.ShapeDtypeStruct((n, d), F32),
        compiler_params=_params("arbitrary"),
        name="combine",
    )(x2d, gates, g, y.reshape(TOP_K, n * TOKEN_TILE_ROWS, V7X_LANES))


def _layer(x, mem, norm_mix_g, w_in, w_pool, pool_scale, w_gate_up, b_gate_up, gla_norm_g, w_out,
           norm_xattn_g, norm_mem_g, w_xq, w_xk, w_xv, w_xo,
           norm_moe_g, w_router, b_router, w_gu, b_gu, w_dn, b_dn, out_g):
    b, s, d = x.shape
    n = b * s
    row = lambda a: a.reshape(1, -1)

    s0, s1, s2, s3, s4 = (POOL_W, POOL_W + GLA_KEY, POOL_W + 2 * GLA_KEY,
                          POOL_W + 2 * GLA_KEY + GLA_VAL, POOL_W + 2 * GLA_KEY + GLA_VAL + GATE_RANK)
    win = jnp.concatenate(
        [w_in[:, :s3], w_in[:, s4:], w_in[:, s3:s4],
         jnp.zeros((d, V7X_LANES - GATE_RANK), w_in.dtype)], axis=1).astype(BF16)
    wgate = jnp.concatenate(
        [w_gate_up, jnp.zeros((V7X_LANES - GATE_RANK, GLA_KEY), w_gate_up.dtype)], axis=0).astype(BF16)
    wrt = w_router.T
    wrt_hi = wrt.astype(BF16)
    wrt_lo = (wrt - wrt_hi.astype(F32)).astype(BF16)

    kmem, vmem = _memkv(mem, row(norm_mem_g), w_xk.astype(BF16), w_xv.astype(BF16))
    x1 = _mix(x, row(norm_mix_g), win, w_pool.astype(BF16), row(pool_scale), wgate,
              row(b_gate_up), row(gla_norm_g), w_out.astype(BF16))
    x2 = _xattn(x1, row(norm_xattn_g), w_xq.astype(BF16), kmem, vmem, w_xo.astype(BF16))
    x2 = x2.reshape(n, d)

    h, eid, rank, gates, cnt = _route(x2, row(norm_moe_g), wrt_hi, wrt_lo, b_router.reshape(-1, 1))
    counts = cnt[:, 0].astype(I32)
    starts = (jnp.cumsum(counts) - counts).astype(I32)
    dest = _plan(starts, eid, rank).reshape(TOP_K * n)
    items = _expert_items(counts, TOP_K * n)
    y = _experts(items, _invert(dest), h, w_gu, b_gu.reshape(N_EXPERTS, 1, -1),
                 w_dn, b_dn.reshape(N_EXPERTS, 1, -1))
    out = _combine(x2, gates, row(out_g), y)
    return out.reshape(b, s, d)


def kernel(x, mem, norm_mix_g, w_in, w_pool, pool_scale, w_gate_up, b_gate_up, gla_norm_g, w_out,
           norm_xattn_g, norm_mem_g, w_xq, w_xk, w_xv, w_xo, norm_moe_g, w_router, b_router,
           w_gu, b_gu, w_dn, b_dn, norm_final_g):
    depth = norm_mix_g.shape[0]
    assert depth == 1, "the final rmsnorm is fused into the (single) layer's combine stage"
    return _layer(x, mem, norm_mix_g[0], w_in[0], w_pool[0], pool_scale[0], w_gate_up[0],
                  b_gate_up[0], gla_norm_g[0], w_out[0], norm_xattn_g[0], norm_mem_g[0],
                  w_xq[0], w_xk[0], w_xv[0], w_xo[0], norm_moe_g[0], w_router[0], b_router[0],
                  w_gu[0], b_gu[0], w_dn[0], b_dn[0], norm_final_g)
```

```python
import functools

import jax
import jax.numpy as jnp
from jax import lax
from jax.experimental import pallas as pl
from jax.experimental.pallas import tpu as pltpu

F32 = jnp.float32
BF16 = jnp.bfloat16
I32 = jnp.int32

V7X_LANES = 128
V7X_VMEM_LIMIT_BYTES = 56 * 1024 * 1024

D_MODEL = 1024
POOL_WINDOWS = (2, 4, 8, 16)
POOL_GROUP_W = 128
POOL_W = len(POOL_WINDOWS) * POOL_GROUP_W
GLA_HEADS = 4
GLA_DK = 64
GLA_DV = 128
GLA_KEY = GLA_HEADS * GLA_DK
GLA_VAL = GLA_HEADS * GLA_DV
GATE_RANK = 16
GATE_NORMALIZER = 16.0
CHUNK = 64
XATTN_HEADS = 4
XATTN_HD = D_MODEL // XATTN_HEADS
N_EXPERTS = 32
TOP_K = 4
D_FF = D_MODEL
SWIGLU_LIMIT = 7.0
SWIGLU_ALPHA = 1.702
EPS = 1e-6

_U0 = 0
_Q0 = _U0 + POOL_W
_K0 = _Q0 + GLA_KEY
_V0 = _K0 + GLA_KEY
_R0 = _V0 + GLA_VAL
_G0 = _R0 + GLA_VAL
IN_COLS_PADDED = _G0 + V7X_LANES

SEQ_TILE = 512
POOL_TAIL = 16
PLAN_TILE = 2048
EXPERT_ROWS = 256
EXPERT_SLOTS = 3
COMBINE_TILE = 256


def _rms(x, g):
    ms = jnp.mean(x * x, axis=-1, keepdims=True)
    return x * lax.rsqrt(ms + EPS) * g


def _dot(a, b):
    return jnp.dot(a, b, preferred_element_type=F32)


def _dot_nt(a, b):
    return lax.dot_general(a, b, (((1,), (1,)), ((), ())), preferred_element_type=F32)


def _dot_tn(a, b):
    return lax.dot_general(a, b, (((0,), (0,)), ((), ())), preferred_element_type=F32)


TOKEN_TILE_ROWS = D_MODEL // V7X_LANES


def _load_token_tiles(ref, lead, n_rows, first_row=0):
    start = first_row * TOKEN_TILE_ROWS
    chunks = [ref[lead + (pl.ds(start + c, n_rows, stride=TOKEN_TILE_ROWS), slice(None))]
              for c in range(TOKEN_TILE_ROWS)]
    return jnp.concatenate(chunks, axis=1)


def _store_token_tiles(ref, lead, val, first_row=0):
    start = first_row * TOKEN_TILE_ROWS
    for c in range(TOKEN_TILE_ROWS):
        ref[lead + (pl.ds(start + c, val.shape[0], stride=TOKEN_TILE_ROWS), slice(None))] = (
            val[:, c * V7X_LANES:(c + 1) * V7X_LANES])


def _params(*semantics):
    return pltpu.CompilerParams(dimension_semantics=semantics,
                                vmem_limit_bytes=V7X_VMEM_LIMIT_BYTES)


def _memkv_kernel(mem_ref, g_ref, wk_ref, wv_ref, k_ref, v_ref):
    hm = _rms(mem_ref[0], g_ref[...]).astype(BF16)
    k = _dot(hm, wk_ref[...])
    v = _dot(hm, wv_ref[...])
    k_ref[0] = (k * (XATTN_HD ** -0.5)).astype(BF16)
    v_ref[0] = v.astype(BF16)


def _memkv(mem, g, wk, wv):
    b, m, d = mem.shape
    const = lambda i: (0, 0)
    return pl.pallas_call(
        _memkv_kernel,
        grid=(b,),
        in_specs=[pl.BlockSpec((1, m, d), lambda i: (i, 0, 0)),
                  pl.BlockSpec((1, d), const),
                  pl.BlockSpec((d, d), const),
                  pl.BlockSpec((d, d), const)],
        out_specs=[pl.BlockSpec((1, m, d), lambda i: (i, 0, 0)),
                   pl.BlockSpec((1, m, d), lambda i: (i, 0, 0))],
        out_shape=[jax.ShapeDtypeStruct((b, m, d), BF16)] * 2,
        compiler_params=_params("arbitrary"),
        name="memkv",
    )(mem, g, wk, wv)


def _mix_kernel(x_ref, g_ref, win_ref, wpool_ref, pscale_ref, wgate_ref, bgate_ref,
                gnorm_ref, wout_ref, ltri_ref, o_ref, uprev_ref, state_ref, oacc_ref):
    t = x_ref.shape[1]
    j = pl.program_id(1)

    @pl.when(j == 0)
    def _():
        uprev_ref[...] = jnp.zeros_like(uprev_ref)
        state_ref[...] = jnp.zeros_like(state_ref)

    x = x_ref[0]
    h = _rms(x, g_ref[...]).astype(BF16)
    def project(c0, width):
        return _dot(h, win_ref[:, c0:c0 + width])

    glr = project(_G0, V7X_LANES)
    u = project(_U0, POOL_W)
    gp = _dot(glr.astype(BF16), wgate_ref[...]) + bgate_ref[...]
    q = project(_Q0, GLA_KEY)
    k = project(_K0, GLA_KEY)
    g = jax.nn.log_sigmoid(gp) / GATE_NORMALIZER
    g_hi = g.astype(BF16)
    g_lo = (g - g_hi.astype(F32)).astype(BF16)
    gc = _dot(ltri_ref[...], g_hi) + _dot(ltri_ref[...], g_lo)
    v = project(_V0, GLA_VAL)
    r = project(_R0, GLA_VAL)

    u_ext = jnp.concatenate([uprev_ref[...], u], axis=0)
    uprev_ref[...] = u[t - POOL_TAIL:, :]
    row = lax.broadcasted_iota(I32, (t, POOL_GROUP_W), 0)
    pos = (j * t + row + 1).astype(F32)
    pooled = []
    for gi, w in enumerate(POOL_WINDOWS):
        cols = slice(gi * POOL_GROUP_W, (gi + 1) * POOL_GROUP_W)
        wsum = u_ext[:, cols]
        shift = 1
        while shift < w:
            wsum = wsum + pltpu.roll(wsum, shift, axis=0)
            shift *= 2
        p = wsum[POOL_TAIL:, :] / jnp.minimum(pos, float(w)) - u[:, cols]
        pooled.append(_dot(p.astype(BF16), wpool_ref[gi]))
    pool = jnp.concatenate(pooled, axis=1) * pscale_ref[...]

    qe = q * (GLA_DK ** -0.5) * jnp.exp(gc)
    ke = k * jnp.exp(-gc)

    lane = lax.broadcasted_iota(I32, (CHUNK, V7X_LANES), 1)
    first_head = lane < GLA_DK
    causal = (lax.broadcasted_iota(I32, (CHUNK, CHUNK), 0)
              >= lax.broadcasted_iota(I32, (CHUNK, CHUNK), 1))
    srow = lax.broadcasted_iota(I32, (2 * GLA_DV, V7X_LANES), 0)
    slane = lax.broadcasted_iota(I32, (2 * GLA_DV, V7X_LANES), 1)
    same_head = (srow < GLA_DV) == (slane < GLA_DK)

    states = [state_ref[0], state_ref[1]]
    for c in range(t // CHUNK):
        rows = slice(c * CHUNK, (c + 1) * CHUNK)
        g_last = gc[c * CHUNK + CHUNK - 1:c * CHUNK + CHUNK, :]
        k2 = k[rows] * jnp.exp(g_last - gc[rows])
        dec = jnp.exp(g_last)
        for p in range(GLA_HEADS // 2):
            lanes = slice(p * V7X_LANES, (p + 1) * V7X_LANES)
            vals = slice(p * 2 * GLA_DV, (p + 1) * 2 * GLA_DV)
            qe_p = qe[rows, lanes]
            ke_p = ke[rows, lanes].astype(BF16)
            v_p = v[rows, vals].astype(BF16)
            s_prev = states[p]
            outs = []
            for a in range(2):
                q_a = jnp.where(first_head if a == 0 else ~first_head, qe_p, 0.0)
                att = jnp.where(causal, _dot_nt(q_a.astype(BF16), ke_p), 0.0)
                outs.append(_dot(att.astype(BF16), v_p[:, a * GLA_DV:(a + 1) * GLA_DV]))
            inter = _dot_nt(qe_p.astype(BF16), s_prev.astype(BF16))
            oacc_ref[rows, vals] = jnp.concatenate(outs, axis=1) + inter
            d_state = _dot_tn(v_p, k2[:, lanes].astype(BF16))
            states[p] = s_prev * dec[:, lanes] + jnp.where(same_head, d_state, 0.0)
    state_ref[0] = states[0]
    state_ref[1] = states[1]

    o = oacc_ref[...]
    gla = []
    for hh in range(GLA_HEADS):
        cols = slice(hh * GLA_DV, (hh + 1) * GLA_DV)
        oh = o[:, cols]
        on = oh * lax.rsqrt(jnp.mean(oh * oh, axis=-1, keepdims=True) + EPS) * gnorm_ref[...]
        gla.append(on * jax.nn.silu(r[:, cols]))
    mix = jnp.concatenate([pool] + gla, axis=1).astype(BF16)
    o_ref[0] = x + _dot(mix, wout_ref[...])


def _mix(x, g, win, wpool, pscale, wgate, bgate, gnorm, wout):
    b, s, d = x.shape
    t = SEQ_TILE
    assert s % t == 0 and t % CHUNK == 0 and t >= POOL_TAIL >= max(POOL_WINDOWS) - 1
    ii = jnp.arange(t)
    ltri = ((ii[:, None] // CHUNK == ii[None, :] // CHUNK)
            & (ii[None, :] <= ii[:, None])).astype(BF16)
    c2 = lambda bi, j: (0, 0)
    c3 = lambda bi, j: (0, 0, 0)
    return pl.pallas_call(
        _mix_kernel,
        grid=(b, s // t),
        in_specs=[pl.BlockSpec((1, t, d), lambda bi, j: (bi, j, 0)),
                  pl.BlockSpec((1, d), c2),
                  pl.BlockSpec((d, IN_COLS_PADDED), c2),
                  pl.BlockSpec((len(POOL_WINDOWS), POOL_GROUP_W, POOL_GROUP_W), c3),
                  pl.BlockSpec((1, POOL_W), c2),
                  pl.BlockSpec((V7X_LANES, GLA_KEY), c2),
                  pl.BlockSpec((1, GLA_KEY), c2),
                  pl.BlockSpec((1, GLA_DV), c2),
                  pl.BlockSpec((POOL_W + GLA_VAL, d), c2),
                  pl.BlockSpec((t, t), c2)],
        out_specs=pl.BlockSpec((1, t, d), lambda bi, j: (bi, j, 0)),
        out_shape=jax.ShapeDtypeStruct((b, s, d), F32),
        scratch_shapes=[pltpu.VMEM((POOL_TAIL, POOL_W), F32),
                        pltpu.VMEM((GLA_HEADS // 2, 2 * GLA_DV, V7X_LANES), F32),
                        pltpu.VMEM((t, GLA_VAL), F32)],
        compiler_params=_params("arbitrary", "arbitrary"),
        name="mix",
    )(x, g, win, wpool, pscale, wgate, bgate, gnorm, wout, ltri)


def _attend_route_kernel(x_ref, g_ref, wq_ref, k_ref, v_ref, wo_ref,
                         gm_ref, wrt_hi_ref, wrt_lo_ref, br_ref, utri_ref,
                         o_ref, h_ref, eid_ref, rank_ref, gate_ref, cnt_ref, carry_ref):
    x = x_ref[0]
    h = _rms(x, g_ref[...]).astype(BF16)
    q = _dot(h, wq_ref[...])
    heads = []
    for hh in range(XATTN_HEADS):
        cols = slice(hh * XATTN_HD, (hh + 1) * XATTN_HD)
        sc = _dot_nt(q[:, cols].astype(BF16), k_ref[0][:, cols])
        e = jnp.exp(sc - jnp.max(sc, axis=-1, keepdims=True))
        p = e / jnp.sum(e, axis=-1, keepdims=True)
        heads.append(_dot(p.astype(BF16), v_ref[0][:, cols]))
    o = jnp.concatenate(heads, axis=1).astype(BF16)
    x2 = x + _dot(o, wo_ref[...])
    o_ref[0] = x2
    first = (pl.program_id(0) == 0) & (pl.program_id(1) == 0)
    _route_tile(x2, first, gm_ref, wrt_hi_ref, wrt_lo_ref, br_ref, utri_ref,
                h_ref, eid_ref, rank_ref, gate_ref, cnt_ref, carry_ref)


def _attend_route(x, g, wq, kmem, vmem, wo, gm, wrt_hi, wrt_lo, br):
    b, s, d = x.shape
    m = kmem.shape[1]
    t = SEQ_TILE
    nt = s // t
    n = b * s
    ii = jnp.arange(t)
    utri = (ii[:, None] < ii[None, :]).astype(BF16)
    c2 = lambda bi, j: (0, 0)
    flat = lambda bi, j: (bi * nt + j, 0)
    flat_lanes = lambda bi, j: (0, bi * nt + j)
    return pl.pallas_call(
        _attend_route_kernel,
        grid=(b, nt),
        in_specs=[pl.BlockSpec((1, t, d), lambda bi, j: (bi, j, 0)),
                  pl.BlockSpec((1, d), c2),
                  pl.BlockSpec((d, d), c2),
                  pl.BlockSpec((1, m, d), lambda bi, j: (bi, 0, 0)),
                  pl.BlockSpec((1, m, d), lambda bi, j: (bi, 0, 0)),
                  pl.BlockSpec((d, d), c2),
                  pl.BlockSpec((1, d), c2),
                  pl.BlockSpec((N_EXPERTS, d), c2),
                  pl.BlockSpec((N_EXPERTS, d), c2),
                  pl.BlockSpec((N_EXPERTS, 1), c2),
                  pl.BlockSpec((t, t), c2)],
        out_specs=[pl.BlockSpec((1, t, d), lambda bi, j: (bi, j, 0)),
                   pl.BlockSpec((t * TOKEN_TILE_ROWS, V7X_LANES), flat),
                   pl.BlockSpec((TOP_K, t), flat_lanes),
                   pl.BlockSpec((TOP_K, t), flat_lanes),
                   pl.BlockSpec((t, V7X_LANES), flat),
                   pl.BlockSpec((N_EXPERTS, V7X_LANES), c2)],
        out_shape=[jax.ShapeDtypeStruct((b, s, d), F32),
                   jax.ShapeDtypeStruct((n * TOKEN_TILE_ROWS, V7X_LANES), F32),
                   jax.ShapeDtypeStruct((TOP_K, n), I32),
                   jax.ShapeDtypeStruct((TOP_K, n), I32),
                   jax.ShapeDtypeStruct((n, V7X_LANES), F32),
                   jax.ShapeDtypeStruct((N_EXPERTS, V7X_LANES), F32)],
        scratch_shapes=[pltpu.VMEM((N_EXPERTS, V7X_LANES), F32)],
        compiler_params=_params("arbitrary", "arbitrary"),
        name="attend_route",
    )(x, g, wq, kmem, vmem, wo, gm, wrt_hi, wrt_lo, br, utri)


def _route_tile(x, first, g_ref, wrt_hi_ref, wrt_lo_ref, br_ref, utri_ref,
                h_ref, eid_ref, rank_ref, gate_ref, cnt_ref, carry_ref):
    t = x.shape[0]

    @pl.when(first)
    def _():
        carry_ref[...] = jnp.zeros_like(carry_ref)

    h = _rms(x, g_ref[...])
    _store_token_tiles(h_ref, (), h)
    h_hi = h.astype(BF16)
    h_lo = (h - h_hi.astype(F32)).astype(BF16)
    logits = (_dot_nt(wrt_hi_ref[...], h_hi) + _dot_nt(wrt_hi_ref[...], h_lo)
              + _dot_nt(wrt_lo_ref[...], h_hi)) + br_ref[...]

    e_iota = lax.broadcasted_iota(I32, (N_EXPERTS, t), 0)
    e_iota_f = e_iota.astype(F32)
    work = logits
    vals, onehots = [], []
    for _ in range(TOP_K):
        m = jnp.max(work, axis=0, keepdims=True)
        idx = jnp.min(jnp.where(work == m, e_iota_f, float(N_EXPERTS)), axis=0, keepdims=True)
        hit = e_iota_f == idx
        vals.append(m)
        onehots.append(hit)
        work = jnp.where(hit, -jnp.inf, work)

    ex = [jnp.exp(vk - vals[0]) for vk in vals]
    den = ex[0] + ex[1] + ex[2] + ex[3]

    member = jnp.zeros((N_EXPERTS, t), F32)
    for hit in onehots:
        member = member + jnp.where(hit, 1.0, 0.0)
    carry = carry_ref[...]
    before = _dot(member.astype(BF16), utri_ref[...]) + carry[:, 0:1]

    row4 = lax.broadcasted_iota(I32, (TOP_K, t), 0)
    grow = lax.broadcasted_iota(I32, (V7X_LANES, t), 0)
    eid = jnp.zeros((TOP_K, t), I32)
    rank = jnp.zeros((TOP_K, t), I32)
    gates_t = jnp.zeros((V7X_LANES, t), F32)
    for kk in range(TOP_K):
        idx_k = jnp.sum(jnp.where(onehots[kk], e_iota_f, 0.0), axis=0, keepdims=True)
        rank_k = jnp.sum(jnp.where(onehots[kk], before, 0.0), axis=0, keepdims=True)
        eid = jnp.where(row4 == kk, idx_k.astype(I32), eid)
        rank = jnp.where(row4 == kk, rank_k.astype(I32), rank)
        gates_t = jnp.where(grow == kk, ex[kk] / den, gates_t)
    eid_ref[...] = eid
    rank_ref[...] = rank
    gate_ref[...] = gates_t.T

    new_carry = carry + jnp.sum(member, axis=1, keepdims=True)
    carry_ref[...] = new_carry
    cnt_ref[...] = new_carry


def _plan_kernel(start_ref, eid_ref, rank_ref, dest_ref):
    e = eid_ref[...]
    dest = rank_ref[...]
    for ee in range(N_EXPERTS):
        dest = dest + jnp.where(e == ee, start_ref[ee], 0)
    dest_ref[...] = dest


def _plan(starts, eid, rank):
    n = eid.shape[1]
    t = min(PLAN_TILE, n)
    blk = pl.BlockSpec((TOP_K, t), lambda i, s: (0, i))
    return pl.pallas_call(
        _plan_kernel,
        grid_spec=pltpu.PrefetchScalarGridSpec(
            num_scalar_prefetch=1, grid=(n // t,), in_specs=[blk, blk], out_specs=blk),
        out_shape=jax.ShapeDtypeStruct((TOP_K, n), I32),
        compiler_params=_params("arbitrary"),
        name="plan",
    )(starts, eid, rank)


def _invert_kernel(dest_ref, inv_ref):
    def body(i, carry):
        inv_ref[dest_ref[i]] = i
        return carry

    lax.fori_loop(0, dest_ref.shape[0], body, 0, unroll=32)


def _invert(dest_flat):
    return pl.pallas_call(
        _invert_kernel,
        grid_spec=pltpu.PrefetchScalarGridSpec(
            num_scalar_prefetch=1, grid=(1,), in_specs=[],
            out_specs=pl.BlockSpec(memory_space=pltpu.SMEM)),
        out_shape=jax.ShapeDtypeStruct(dest_flat.shape, I32),
        compiler_params=_params("arbitrary"),
        name="invert",
    )(dest_flat)


_ITEM_VALID, _ITEM_FIRST, _ITEM_NEW_EXPERT, _ITEM_HAS_NEXT_EXPERT = 1, 2, 4, 8


def _expert_kernel(blk_ref, lo_ref, hi_ref, flag_ref, exp_ref, next_exp_ref, inv_ref,
                   h_ref, wgu_hbm, bgu_ref, wdn_hbm, bdn_ref, out_ref,
                   wgu_bf, wdn_bf, wgu_f32, wdn_f32, xs_buf, y_buf, gsem, ssem, wsem,
                   *, n_tok, n_blocks):
    tr = TOKEN_TILE_ROWS
    bm = xs_buf.shape[1] // tr
    i = pl.program_id(0)
    flags = flag_ref[i]
    blk = blk_ref[i]
    slots = xs_buf.shape[0]
    slot = lax.rem(blk, slots)

    def token_tile(ref, row):
        return ref.at[pl.ds(pl.multiple_of(row * tr, tr), tr), :]

    def gather(b, s):
        return [pltpu.make_async_copy(
            token_tile(h_ref, inv_ref[b * bm + r] & (n_tok - 1)),
            xs_buf.at[s, pl.ds(r * tr, tr), :], gsem.at[s]) for r in range(bm)]

    def scatter(b, s):
        return [pltpu.make_async_copy(
            y_buf.at[s, pl.ds(r * tr, tr), :],
            token_tile(out_ref, inv_ref[b * bm + r]), ssem.at[s]) for r in range(bm)]

    def wait_gather(s):
        pltpu.make_async_copy(h_ref.at[pl.ds(0, bm * tr), :], xs_buf.at[s], gsem.at[s]).wait()

    def wait_scatter(s):
        pltpu.make_async_copy(y_buf.at[s], out_ref.at[pl.ds(0, bm * tr), :], ssem.at[s]).wait()

    def start_all(copies):
        for r, c in enumerate(copies):
            c.start(priority=r % 2)

    @pl.when(i == 0)
    def _():
        for b in range(slots - 1):
            start_all(gather(b, b))

    def weight_copies(e):
        return (pltpu.make_async_copy(wgu_hbm.at[e], wgu_f32, wsem.at[0]),
                pltpu.make_async_copy(wdn_hbm.at[e], wdn_f32, wsem.at[1]))

    @pl.when(i == 0)
    def _():
        for c in weight_copies(exp_ref[0]):
            c.start()

    @pl.when((flags & _ITEM_NEW_EXPERT) != 0)
    def _():
        for c in weight_copies(exp_ref[i]):
            c.wait()
        wgu_bf[...] = wgu_f32[...].astype(BF16)
        wdn_bf[...] = wdn_f32[...].astype(BF16)

        @pl.when((flags & _ITEM_HAS_NEXT_EXPERT) != 0)
        def _():
            for c in weight_copies(next_exp_ref[i]):
                c.start()

    bgu = bgu_ref[exp_ref[i]]
    bdn = bdn_ref[exp_ref[i]]

    dyn_slot = (slot,)
    half = bm // 2

    def load_x(r0, n_rows):
        return _load_token_tiles(xs_buf, dyn_slot, n_rows, r0).astype(BF16)

    def ffn_rows(x, r0, merge):
        n_rows = x.shape[0]
        gu = _dot(x, wgu_bf[...]) + bgu
        gate = jnp.minimum(gu[:, :D_FF], SWIGLU_LIMIT)
        up = jnp.clip(gu[:, D_FF:], -SWIGLU_LIMIT, SWIGLU_LIMIT)
        act = gate * jax.nn.sigmoid(SWIGLU_ALPHA * gate) * (up + 1.0)
        yb = _dot(act.astype(BF16), wdn_bf[...]) + bdn
        row = r0 + lax.broadcasted_iota(I32, yb.shape, 0)
        mine = (row >= lo_ref[i]) & (row < hi_ref[i])
        other = _load_token_tiles(y_buf, dyn_slot, n_rows, r0) if merge else 0.0
        return jnp.where(mine, yb, other)

    def first_item(s, has_prev):
        wait_gather(s)
        x_top = load_x(0, half)
        ahead = slots - 1
        start_all(gather(jnp.minimum(blk + ahead, n_blocks - 1), (s + ahead) % slots))
        _store_token_tiles(y_buf, dyn_slot, ffn_rows(x_top, 0, False), 0)
        if has_prev:
            start_all(scatter(blk - 1, (s - 1) % slots))
        _store_token_tiles(y_buf, dyn_slot, ffn_rows(load_x(half, half), half, False), half)

    first = (flags & (_ITEM_VALID | _ITEM_FIRST)) == (_ITEM_VALID | _ITEM_FIRST)
    later = (flags & (_ITEM_VALID | _ITEM_FIRST)) == _ITEM_VALID

    @pl.when(first & (blk == 0))
    def _():
        first_item(0, False)

    @pl.when(first & (blk >= slots))
    def _():
        for s in range(slots):
            @pl.when(slot == s)
            def _():
                wait_scatter(s)

    for s in range(slots):
        @pl.when(first & (slot == s) & (blk > 0))
        def _():
            first_item(s, True)

    for r0 in (0, half):
        @pl.when(later & (lo_ref[i] < r0 + half) & (hi_ref[i] > r0))
        def _():
            _store_token_tiles(y_buf, dyn_slot, ffn_rows(load_x(r0, half), r0, True), r0)

    @pl.when(i == pl.num_programs(0) - 1)
    def _():
        start_all(scatter(n_blocks - 1, (n_blocks - 1) % slots))
        for s in range(slots):
            wait_scatter(s)
        for ahead in range(1, slots):
            wait_gather((n_blocks - 1 + ahead) % slots)


def _expert_items(counts, n_rows):
    bm = EXPERT_ROWS
    n_items = n_rows // bm + N_EXPERTS - 1
    ends = jnp.cumsum(counts)
    starts = ends - counts
    blocks_of = jnp.where(counts > 0, (ends - 1) // bm - starts // bm + 1, 0)
    item_end = jnp.cumsum(blocks_of)
    item_start = item_end - blocks_of
    total = item_end[-1]
    idx = jnp.arange(n_items, dtype=I32)
    valid = idx < total
    idc = jnp.minimum(idx, jnp.maximum(total - 1, 0))
    owner = ((item_start[None, :] <= idc[:, None]) & (idc[:, None] < item_end[None, :])).astype(I32)
    pick = lambda v: jnp.sum(owner * v[None, :], axis=1)
    e = pick(jnp.arange(N_EXPERTS, dtype=I32))
    blk = pick(starts // bm) + idc - pick(item_start)
    lo = jnp.clip(pick(starts) - blk * bm, 0, bm)
    hi = jnp.clip(pick(ends) - blk * bm, 0, bm)
    prev_blk = jnp.concatenate([jnp.full((1,), -1, I32), blk[:-1]])
    prev_e = jnp.concatenate([jnp.full((1,), -1, I32), e[:-1]])
    ids = jnp.arange(N_EXPERTS, dtype=I32)
    owners = jnp.where(counts > 0, ids, N_EXPERTS)
    next_of = jnp.min(jnp.where(ids[None, :] > ids[:, None], owners[None, :], N_EXPERTS), axis=1)
    next_e = pick(next_of)
    new_expert = e != prev_e
    flags = jnp.where(valid,
                      _ITEM_VALID
                      + jnp.where(blk != prev_blk, _ITEM_FIRST, 0)
                      + jnp.where(new_expert, _ITEM_NEW_EXPERT, 0)
                      + jnp.where(new_expert & (next_e < N_EXPERTS), _ITEM_HAS_NEXT_EXPERT, 0), 0)
    as_i32 = lambda v: v.astype(I32)
    return (as_i32(blk), as_i32(lo), as_i32(hi), as_i32(flags), as_i32(e),
            as_i32(jnp.minimum(next_e, N_EXPERTS - 1)))


def _experts(items, inv, h, wgu, bgu, wdn, bdn):
    tr, d = TOKEN_TILE_ROWS, D_MODEL
    n = h.shape[0] // tr
    bm = EXPERT_ROWS
    n_rows = inv.shape[0]
    n_blocks = n_rows // bm
    assert n_rows % bm == 0 and n_blocks >= EXPERT_SLOTS and n & (n - 1) == 0
    blk, lo, hi, flags, e, next_e = items
    whole = lambda i, *prefetch: (0, 0, 0)
    return pl.pallas_call(
        functools.partial(_expert_kernel, n_tok=n, n_blocks=n_blocks),
        grid_spec=pltpu.PrefetchScalarGridSpec(
            num_scalar_prefetch=7, grid=(blk.shape[0],),
            in_specs=[pl.BlockSpec(memory_space=pl.ANY),
                      pl.BlockSpec(memory_space=pl.ANY),
                      pl.BlockSpec((N_EXPERTS, 1, 2 * D_FF), whole),
                      pl.BlockSpec(memory_space=pl.ANY),
                      pl.BlockSpec((N_EXPERTS, 1, d), whole)],
            out_specs=pl.BlockSpec(memory_space=pl.ANY),
            scratch_shapes=[pltpu.VMEM((d, 2 * D_FF), BF16),
                            pltpu.VMEM((D_FF, d), BF16),
                            pltpu.VMEM((d, 2 * D_FF), F32),
                            pltpu.VMEM((D_FF, d), F32),
                            pltpu.VMEM((EXPERT_SLOTS, bm * tr, V7X_LANES), F32),
                            pltpu.VMEM((EXPERT_SLOTS, bm * tr, V7X_LANES), F32),
                            pltpu.SemaphoreType.DMA((EXPERT_SLOTS,)),
                            pltpu.SemaphoreType.DMA((EXPERT_SLOTS,)),
                            pltpu.SemaphoreType.DMA((2,))]),
        out_shape=jax.ShapeDtypeStruct((n_rows * tr, V7X_LANES), F32),
        compiler_params=_params("arbitrary"),
        name="experts",
    )(blk, lo, hi, flags, e, next_e, inv, h, wgu, bgu, wdn, bdn)


def _combine_kernel(x_ref, gate_ref, g_ref, y_ref, o_ref):
    gates = gate_ref[...]
    moe = jnp.zeros(x_ref.shape, F32)
    for kk in range(TOP_K):
        moe = moe + _load_token_tiles(y_ref, (kk,), x_ref.shape[0]) * gates[:, kk:kk + 1]
    o_ref[...] = _rms(x_ref[...] + moe, g_ref[...])


def _combine(x2d, gates, g, y):
    n, d = x2d.shape
    t = min(COMBINE_TILE, n)
    return pl.pallas_call(
        _combine_kernel,
        grid=(n // t,),
        in_specs=[pl.BlockSpec((t, d), lambda i: (i, 0)),
                  pl.BlockSpec((t, V7X_LANES), lambda i: (i, 0)),
                  pl.BlockSpec((1, d), lambda i: (0, 0)),
                  pl.BlockSpec((TOP_K, t * TOKEN_TILE_ROWS, V7X_LANES), lambda i: (0, i, 0))],
        out_specs=pl.BlockSpec((t, d), lambda i: (i, 0)),
        out_shape=jax.ShapeDtypeStruct((n, d), F32),
        compiler_params=_params("arbitrary"),
        name="combine",
    )(x2d, gates, g, y.reshape(TOP_K, n * TOKEN_TILE_ROWS, V7X_LANES))


def _layer(x, mem, norm_mix_g, w_in, w_pool, pool_scale, w_gate_up, b_gate_up, gla_norm_g, w_out,
           norm_xattn_g, norm_mem_g, w_xq, w_xk, w_xv, w_xo,
           norm_moe_g, w_router, b_router, w_gu, b_gu, w_dn, b_dn, out_g):
    b, s, d = x.shape
    n = b * s
    row = lambda a: a.reshape(1, -1)

    s0, s1, s2, s3, s4 = (POOL_W, POOL_W + GLA_KEY, POOL_W + 2 * GLA_KEY,
                          POOL_W + 2 * GLA_KEY + GLA_VAL, POOL_W + 2 * GLA_KEY + GLA_VAL + GATE_RANK)
    win = jnp.concatenate(
        [w_in[:, :s3], w_in[:, s4:], w_in[:, s3:s4],
         jnp.zeros((d, V7X_LANES - GATE_RANK), w_in.dtype)], axis=1).astype(BF16)
    wgate = jnp.concatenate(
        [w_gate_up, jnp.zeros((V7X_LANES - GATE_RANK, GLA_KEY), w_gate_up.dtype)], axis=0).astype(BF16)
    wrt = w_router.T
    wrt_hi = wrt.astype(BF16)
    wrt_lo = (wrt - wrt_hi.astype(F32)).astype(BF16)

    kmem, vmem = _memkv(mem, row(norm_mem_g), w_xk.astype(BF16), w_xv.astype(BF16))
    x1 = _mix(x, row(norm_mix_g), win, w_pool.astype(BF16), row(pool_scale), wgate,
              row(b_gate_up), row(gla_norm_g), w_out.astype(BF16))
    x2, h, eid, rank, gates, cnt = _attend_route(
        x1, row(norm_xattn_g), w_xq.astype(BF16), kmem, vmem, w_xo.astype(BF16),
        row(norm_moe_g), wrt_hi, wrt_lo, b_router.reshape(-1, 1))
    x2 = x2.reshape(n, d)
    counts = cnt[:, 0].astype(I32)
    starts = (jnp.cumsum(counts) - counts).astype(I32)
    dest = _plan(starts, eid, rank).reshape(TOP_K * n)
    items = _expert_items(counts, TOP_K * n)
    y = _experts(items, _invert(dest), h, w_gu, b_gu.reshape(N_EXPERTS, 1, -1),
                 w_dn, b_dn.reshape(N_EXPERTS, 1, -1))
    out = _combine(x2, gates, row(out_g), y)
    return out.reshape(b, s, d)


def kernel(x, mem, norm_mix_g, w_in, w_pool, pool_scale, w_gate_up, b_gate_up, gla_norm_g, w_out,
           norm_xattn_g, norm_mem_g, w_xq, w_xk, w_xv, w_xo, norm_moe_g, w_router, b_router,
           w_gu, b_gu, w_dn, b_dn, norm_final_g):
    depth = norm_mix_g.shape[0]
    assert depth == 1, "the final rmsnorm is fused into the (single) layer's combine stage"
    return _layer(x, mem, norm_mix_g[0], w_in[0], w_pool[0], pool_scale[0], w_gate_up[0],
                  b_gate_up[0], gla_norm_g[0], w_out[0], norm_xattn_g[0], norm_mem_g[0],
                  w_xq[0], w_xk[0], w_xv[0], w_xo[0], norm_moe_g[0], w_router[0], b_router[0],
                  w_gu[0], b_gu[0], w_dn[0], b_dn[0], norm_final_g)
```

```python
import functools

import jax
import jax.numpy as jnp
from jax import lax
from jax.experimental import pallas as pl
from jax.experimental.pallas import tpu as pltpu

F32 = jnp.float32
BF16 = jnp.bfloat16
I32 = jnp.int32

V7X_LANES = 128
V7X_VMEM_LIMIT_BYTES = 56 * 1024 * 1024

D_MODEL = 1024
POOL_WINDOWS = (2, 4, 8, 16)
POOL_GROUP_W = 128
POOL_W = len(POOL_WINDOWS) * POOL_GROUP_W
GLA_HEADS = 4
GLA_DK = 64
GLA_DV = 128
GLA_KEY = GLA_HEADS * GLA_DK
GLA_VAL = GLA_HEADS * GLA_DV
GATE_RANK = 16
GATE_NORMALIZER = 16.0
CHUNK = 64
XATTN_HEADS = 4
XATTN_HD = D_MODEL // XATTN_HEADS
N_EXPERTS = 32
TOP_K = 4
D_FF = D_MODEL
SWIGLU_LIMIT = 7.0
SWIGLU_ALPHA = 1.702
EPS = 1e-6

_U0 = 0
_Q0 = _U0 + POOL_W
_K0 = _Q0 + GLA_KEY
_V0 = _K0 + GLA_KEY
_R0 = _V0 + GLA_VAL
_G0 = _R0 + GLA_VAL
IN_COLS_PADDED = _G0 + V7X_LANES

SEQ_TILE = 512
POOL_TAIL = 16
PLAN_TILE = 2048
EXPERT_ROWS = 256
EXPERT_SLOTS = 3
COMBINE_TILE = 256


def _rms(x, g):
    ms = jnp.mean(x * x, axis=-1, keepdims=True)
    return x * lax.rsqrt(ms + EPS) * g


def _dot(a, b):
    return jnp.dot(a, b, preferred_element_type=F32)


def _dot_nt(a, b):
    return lax.dot_general(a, b, (((1,), (1,)), ((), ())), preferred_element_type=F32)


def _dot_tn(a, b):
    return lax.dot_general(a, b, (((0,), (0,)), ((), ())), preferred_element_type=F32)


TOKEN_TILE_ROWS = D_MODEL // V7X_LANES


def _load_token_tiles(ref, lead, n_rows, first_row=0):
    start = first_row * TOKEN_TILE_ROWS
    chunks = [ref[lead + (pl.ds(start + c, n_rows, stride=TOKEN_TILE_ROWS), slice(None))]
              for c in range(TOKEN_TILE_ROWS)]
    return jnp.concatenate(chunks, axis=1)


def _store_token_tiles(ref, lead, val, first_row=0):
    start = first_row * TOKEN_TILE_ROWS
    for c in range(TOKEN_TILE_ROWS):
        ref[lead + (pl.ds(start + c, val.shape[0], stride=TOKEN_TILE_ROWS), slice(None))] = (
            val[:, c * V7X_LANES:(c + 1) * V7X_LANES])


def _params(*semantics):
    return pltpu.CompilerParams(dimension_semantics=semantics,
                                vmem_limit_bytes=V7X_VMEM_LIMIT_BYTES)


def _memkv_kernel(mem_ref, g_ref, wk_ref, wv_ref, k_ref, v_ref):
    hm = _rms(mem_ref[0], g_ref[...]).astype(BF16)
    k = _dot(hm, wk_ref[...])
    v = _dot(hm, wv_ref[...])
    k_ref[0] = (k * (XATTN_HD ** -0.5)).astype(BF16)
    v_ref[0] = v.astype(BF16)


def _memkv(mem, g, wk, wv):
    b, m, d = mem.shape
    const = lambda i: (0, 0)
    return pl.pallas_call(
        _memkv_kernel,
        grid=(b,),
        in_specs=[pl.BlockSpec((1, m, d), lambda i: (i, 0, 0)),
                  pl.BlockSpec((1, d), const),
                  pl.BlockSpec((d, d), const),
                  pl.BlockSpec((d, d), const)],
        out_specs=[pl.BlockSpec((1, m, d), lambda i: (i, 0, 0)),
                   pl.BlockSpec((1, m, d), lambda i: (i, 0, 0))],
        out_shape=[jax.ShapeDtypeStruct((b, m, d), BF16)] * 2,
        compiler_params=_params("arbitrary"),
        name="memkv",
    )(mem, g, wk, wv)


def _mix_kernel(x_ref, g_ref, win_ref, wpool_ref, pscale_ref, wgate_ref, bgate_ref,
                gnorm_ref, wout_ref, ltri_ref, o_ref, uprev_ref, state_ref, oacc_ref):
    t = x_ref.shape[1]
    j = pl.program_id(1)

    @pl.when(j == 0)
    def _():
        uprev_ref[...] = jnp.zeros_like(uprev_ref)
        state_ref[...] = jnp.zeros_like(state_ref)

    x = x_ref[0]
    h = _rms(x, g_ref[...]).astype(BF16)
    def project(c0, width):
        return _dot(h, win_ref[:, c0:c0 + width])

    glr = project(_G0, V7X_LANES)
    u = project(_U0, POOL_W)
    gp = _dot(glr.astype(BF16), wgate_ref[...]) + bgate_ref[...]
    q = project(_Q0, GLA_KEY)
    k = project(_K0, GLA_KEY)
    g = jax.nn.log_sigmoid(gp) / GATE_NORMALIZER
    g_hi = g.astype(BF16)
    g_lo = (g - g_hi.astype(F32)).astype(BF16)
    gc = _dot(ltri_ref[...], g_hi) + _dot(ltri_ref[...], g_lo)
    v = project(_V0, GLA_VAL)
    r = project(_R0, GLA_VAL)

    u_ext = jnp.concatenate([uprev_ref[...], u], axis=0)
    uprev_ref[...] = u[t - POOL_TAIL:, :]
    row = lax.broadcasted_iota(I32, (t, POOL_GROUP_W), 0)
    pos = (j * t + row + 1).astype(F32)
    pooled = []
    for gi, w in enumerate(POOL_WINDOWS):
        cols = slice(gi * POOL_GROUP_W, (gi + 1) * POOL_GROUP_W)
        wsum = u_ext[:, cols]
        shift = 1
        while shift < w:
            wsum = wsum + pltpu.roll(wsum, shift, axis=0)
            shift *= 2
        p = wsum[POOL_TAIL:, :] / jnp.minimum(pos, float(w)) - u[:, cols]
        pooled.append(_dot(p.astype(BF16), wpool_ref[gi]))
    pool = jnp.concatenate(pooled, axis=1) * pscale_ref[...]

    qe = q * (GLA_DK ** -0.5) * jnp.exp(gc)
    ke = k * jnp.exp(-gc)

    n_chunks = t // CHUNK
    r2 = lax.broadcasted_iota(I32, (2 * CHUNK, V7X_LANES), 0)
    l2 = lax.broadcasted_iota(I32, (2 * CHUNK, V7X_LANES), 1)
    own_lanes = (r2 < CHUNK) == (l2 < GLA_DK)
    causal2 = (r2 & (CHUNK - 1)) >= (l2 & (CHUNK - 1))
    first_lanes = lax.broadcasted_iota(I32, (CHUNK, V7X_LANES), 1) < CHUNK
    brow = lax.broadcasted_iota(I32, (2 * GLA_DK, 2 * GLA_DV), 0)
    bcol = lax.broadcasted_iota(I32, (2 * GLA_DK, 2 * GLA_DV), 1)
    same_head = (brow < GLA_DK) == (bcol < GLA_DV)
    own_values = (brow < CHUNK) == (bcol < GLA_DV)
    pairs = [(c, p) for c in range(n_chunks) for p in range(GLA_HEADS // 2)]

    def part(c, p):
        return (slice(c * CHUNK, (c + 1) * CHUNK), slice(p * V7X_LANES, (p + 1) * V7X_LANES),
                slice(p * 2 * GLA_DV, (p + 1) * 2 * GLA_DV))

    d_state, decay = {}, {}
    for c, p in pairs:
        rows, lanes, vals = part(c, p)
        g_last = gc[c * CHUNK + CHUNK - 1:c * CHUNK + CHUNK, lanes]
        k2 = (k[rows, lanes] * jnp.exp(g_last - gc[rows, lanes])).astype(BF16)
        d_state[c, p] = jnp.where(same_head, _dot_tn(k2, v[rows, vals].astype(BF16)), 0.0)
        dec = jnp.broadcast_to(jnp.exp(g_last), (V7X_LANES, V7X_LANES)).T
        decay[c, p] = jnp.concatenate([dec, dec], axis=1)
    state_in = {}
    states = [state_ref[0], state_ref[1]]
    for c, p in pairs:
        state_in[c, p] = states[p]
        states[p] = states[p] * decay[c, p] + d_state[c, p]
    state_ref[0] = states[0]
    state_ref[1] = states[1]
    for c, p in pairs:
        rows, lanes, vals = part(c, p)
        qe_p = qe[rows, lanes]
        ke_p = ke[rows, lanes]
        v_f = v[rows, vals]
        q2 = jnp.where(own_lanes, jnp.concatenate([qe_p, qe_p], axis=0), 0.0)
        k2x = jnp.concatenate([ke_p, ke_p], axis=0)
        a2 = jnp.where(causal2, _dot_nt(q2.astype(BF16), k2x.astype(BF16)), 0.0)
        att = jnp.where(first_lanes, a2[:CHUNK], a2[CHUNK:])
        v_bd = jnp.where(own_values, jnp.concatenate([v_f, v_f], axis=0), 0.0).astype(BF16)
        lhs = jnp.concatenate([att, qe_p], axis=1).astype(BF16)
        rhs = jnp.concatenate([v_bd, state_in[c, p].astype(BF16)], axis=0)
        oacc_ref[rows, vals] = _dot(lhs, rhs)

    o = oacc_ref[...]
    gla = []
    for hh in range(GLA_HEADS):
        cols = slice(hh * GLA_DV, (hh + 1) * GLA_DV)
        oh = o[:, cols]
        on = oh * lax.rsqrt(jnp.mean(oh * oh, axis=-1, keepdims=True) + EPS) * gnorm_ref[...]
        gla.append(on * jax.nn.silu(r[:, cols]))
    mix = jnp.concatenate([pool] + gla, axis=1).astype(BF16)
    o_ref[0] = x + _dot(mix, wout_ref[...])


def _mix(x, g, win, wpool, pscale, wgate, bgate, gnorm, wout):
    b, s, d = x.shape
    t = SEQ_TILE
    assert s % t == 0 and t % CHUNK == 0 and t >= POOL_TAIL >= max(POOL_WINDOWS) - 1
    ii = jnp.arange(t)
    ltri = ((ii[:, None] // CHUNK == ii[None, :] // CHUNK)
            & (ii[None, :] <= ii[:, None])).astype(BF16)
    c2 = lambda bi, j: (0, 0)
    c3 = lambda bi, j: (0, 0, 0)
    return pl.pallas_call(
        _mix_kernel,
        grid=(b, s // t),
        in_specs=[pl.BlockSpec((1, t, d), lambda bi, j: (bi, j, 0)),
                  pl.BlockSpec((1, d), c2),
                  pl.BlockSpec((d, IN_COLS_PADDED), c2),
                  pl.BlockSpec((len(POOL_WINDOWS), POOL_GROUP_W, POOL_GROUP_W), c3),
                  pl.BlockSpec((1, POOL_W), c2),
                  pl.BlockSpec((V7X_LANES, GLA_KEY), c2),
                  pl.BlockSpec((1, GLA_KEY), c2),
                  pl.BlockSpec((1, GLA_DV), c2),
                  pl.BlockSpec((POOL_W + GLA_VAL, d), c2),
                  pl.BlockSpec((t, t), c2)],
        out_specs=pl.BlockSpec((1, t, d), lambda bi, j: (bi, j, 0)),
        out_shape=jax.ShapeDtypeStruct((b, s, d), F32),
        scratch_shapes=[pltpu.VMEM((POOL_TAIL, POOL_W), F32),
                        pltpu.VMEM((GLA_HEADS // 2, 2 * GLA_DK, 2 * GLA_DV), F32),
                        pltpu.VMEM((t, GLA_VAL), F32)],
        compiler_params=_params("arbitrary", "arbitrary"),
        name="mix",
    )(x, g, win, wpool, pscale, wgate, bgate, gnorm, wout, ltri)


def _attend_route_kernel(x_ref, g_ref, wq_ref, k_ref, v_ref, wo_ref,
                         gm_ref, wrt_hi_ref, wrt_lo_ref, br_ref, utri_ref,
                         o_ref, h_ref, eid_ref, rank_ref, gate_ref, cnt_ref, carry_ref):
    x = x_ref[0]
    h = _rms(x, g_ref[...]).astype(BF16)
    q = _dot(h, wq_ref[...])
    heads = []
    for hh in range(XATTN_HEADS):
        cols = slice(hh * XATTN_HD, (hh + 1) * XATTN_HD)
        sc = _dot_nt(q[:, cols].astype(BF16), k_ref[0][:, cols])
        e = jnp.exp(sc - jnp.max(sc, axis=-1, keepdims=True))
        p = e / jnp.sum(e, axis=-1, keepdims=True)
        heads.append(_dot(p.astype(BF16), v_ref[0][:, cols]))
    o = jnp.concatenate(heads, axis=1).astype(BF16)
    x2 = x + _dot(o, wo_ref[...])
    o_ref[0] = x2
    first = (pl.program_id(0) == 0) & (pl.program_id(1) == 0)
    _route_tile(x2, first, gm_ref, wrt_hi_ref, wrt_lo_ref, br_ref, utri_ref,
                h_ref, eid_ref, rank_ref, gate_ref, cnt_ref, carry_ref)


def _attend_route(x, g, wq, kmem, vmem, wo, gm, wrt_hi, wrt_lo, br):
    b, s, d = x.shape
    m = kmem.shape[1]
    t = SEQ_TILE
    nt = s // t
    n = b * s
    ii = jnp.arange(t)
    utri = (ii[:, None] < ii[None, :]).astype(BF16)
    c2 = lambda bi, j: (0, 0)
    flat = lambda bi, j: (bi * nt + j, 0)
    flat_lanes = lambda bi, j: (0, bi * nt + j)
    return pl.pallas_call(
        _attend_route_kernel,
        grid=(b, nt),
        in_specs=[pl.BlockSpec((1, t, d), lambda bi, j: (bi, j, 0)),
                  pl.BlockSpec((1, d), c2),
                  pl.BlockSpec((d, d), c2),
                  pl.BlockSpec((1, m, d), lambda bi, j: (bi, 0, 0)),
                  pl.BlockSpec((1, m, d), lambda bi, j: (bi, 0, 0)),
                  pl.BlockSpec((d, d), c2),
                  pl.BlockSpec((1, d), c2),
                  pl.BlockSpec((N_EXPERTS, d), c2),
                  pl.BlockSpec((N_EXPERTS, d), c2),
                  pl.BlockSpec((N_EXPERTS, 1), c2),
                  pl.BlockSpec((t, t), c2)],
        out_specs=[pl.BlockSpec((1, t, d), lambda bi, j: (bi, j, 0)),
                   pl.BlockSpec((t * TOKEN_TILE_ROWS, V7X_LANES), flat),
                   pl.BlockSpec((TOP_K, t), flat_lanes),
                   pl.BlockSpec((TOP_K, t), flat_lanes),
                   pl.BlockSpec((t, V7X_LANES), flat),
                   pl.BlockSpec((N_EXPERTS, V7X_LANES), c2)],
        out_shape=[jax.ShapeDtypeStruct((b, s, d), F32),
                   jax.ShapeDtypeStruct((n * TOKEN_TILE_ROWS, V7X_LANES), F32),
                   jax.ShapeDtypeStruct((TOP_K, n), I32),
                   jax.ShapeDtypeStruct((TOP_K, n), I32),
                   jax.ShapeDtypeStruct((n, V7X_LANES), F32),
                   jax.ShapeDtypeStruct((N_EXPERTS, V7X_LANES), F32)],
        scratch_shapes=[pltpu.VMEM((N_EXPERTS, V7X_LANES), F32)],
        compiler_params=_params("arbitrary", "arbitrary"),
        name="attend_route",
    )(x, g, wq, kmem, vmem, wo, gm, wrt_hi, wrt_lo, br, utri)


def _route_tile(x, first, g_ref, wrt_hi_ref, wrt_lo_ref, br_ref, utri_ref,
                h_ref, eid_ref, rank_ref, gate_ref, cnt_ref, carry_ref):
    t = x.shape[0]

    @pl.when(first)
    def _():
        carry_ref[...] = jnp.zeros_like(carry_ref)

    h = _rms(x, g_ref[...])
    _store_token_tiles(h_ref, (), h)
    h_hi = h.astype(BF16)
    h_lo = (h - h_hi.astype(F32)).astype(BF16)
    logits = (_dot_nt(wrt_hi_ref[...], h_hi) + _dot_nt(wrt_hi_ref[...], h_lo)
              + _dot_nt(wrt_lo_ref[...], h_hi)) + br_ref[...]

    e_iota = lax.broadcasted_iota(I32, (N_EXPERTS, t), 0)
    e_iota_f = e_iota.astype(F32)
    work = logits
    vals, onehots = [], []
    for _ in range(TOP_K):
        m = jnp.max(work, axis=0, keepdims=True)
        idx = jnp.min(jnp.where(work == m, e_iota_f, float(N_EXPERTS)), axis=0, keepdims=True)
        hit = e_iota_f == idx
        vals.append(m)
        onehots.append(hit)
        work = jnp.where(hit, -jnp.inf, work)

    ex = [jnp.exp(vk - vals[0]) for vk in vals]
    den = ex[0] + ex[1] + ex[2] + ex[3]

    member = jnp.zeros((N_EXPERTS, t), F32)
    for hit in onehots:
        member = member + jnp.where(hit, 1.0, 0.0)
    carry = carry_ref[...]
    before = _dot(member.astype(BF16), utri_ref[...]) + carry[:, 0:1]

    row4 = lax.broadcasted_iota(I32, (TOP_K, t), 0)
    grow = lax.broadcasted_iota(I32, (V7X_LANES, t), 0)
    eid = jnp.zeros((TOP_K, t), I32)
    rank = jnp.zeros((TOP_K, t), I32)
    gates_t = jnp.zeros((V7X_LANES, t), F32)
    for kk in range(TOP_K):
        idx_k = jnp.sum(jnp.where(onehots[kk], e_iota_f, 0.0), axis=0, keepdims=True)
        rank_k = jnp.sum(jnp.where(onehots[kk], before, 0.0), axis=0, keepdims=True)
        eid = jnp.where(row4 == kk, idx_k.astype(I32), eid)
        rank = jnp.where(row4 == kk, rank_k.astype(I32), rank)
        gates_t = jnp.where(grow == kk, ex[kk] / den, gates_t)
    eid_ref[...] = eid
    rank_ref[...] = rank
    gate_ref[...] = gates_t.T

    new_carry = carry + jnp.sum(member, axis=1, keepdims=True)
    carry_ref[...] = new_carry
    cnt_ref[...] = new_carry


def _plan_kernel(start_ref, eid_ref, rank_ref, dest_ref):
    e = eid_ref[...]
    dest = rank_ref[...]
    for ee in range(N_EXPERTS):
        dest = dest + jnp.where(e == ee, start_ref[ee], 0)
    dest_ref[...] = dest


def _plan(starts, eid, rank):
    n = eid.shape[1]
    t = min(PLAN_TILE, n)
    blk = pl.BlockSpec((TOP_K, t), lambda i, s: (0, i))
    return pl.pallas_call(
        _plan_kernel,
        grid_spec=pltpu.PrefetchScalarGridSpec(
            num_scalar_prefetch=1, grid=(n // t,), in_specs=[blk, blk], out_specs=blk),
        out_shape=jax.ShapeDtypeStruct((TOP_K, n), I32),
        compiler_params=_params("arbitrary"),
        name="plan",
    )(starts, eid, rank)


def _invert_kernel(dest_ref, inv_ref):
    def body(i, carry):
        inv_ref[dest_ref[i]] = i
        return carry

    lax.fori_loop(0, dest_ref.shape[0], body, 0, unroll=32)


def _invert(dest_flat):
    return pl.pallas_call(
        _invert_kernel,
        grid_spec=pltpu.PrefetchScalarGridSpec(
            num_scalar_prefetch=1, grid=(1,), in_specs=[],
            out_specs=pl.BlockSpec(memory_space=pltpu.SMEM)),
        out_shape=jax.ShapeDtypeStruct(dest_flat.shape, I32),
        compiler_params=_params("arbitrary"),
        name="invert",
    )(dest_flat)


_ITEM_VALID, _ITEM_FIRST, _ITEM_NEW_EXPERT, _ITEM_HAS_NEXT_EXPERT = 1, 2, 4, 8


def _expert_kernel(blk_ref, lo_ref, hi_ref, flag_ref, exp_ref, next_exp_ref, inv_ref,
                   h_ref, wgu_hbm, bgu_ref, wdn_hbm, bdn_ref, out_ref,
                   wgu_bf, wdn_bf, wgu_f32, wdn_f32, xs_buf, y_buf, gsem, ssem, wsem,
                   *, n_tok, n_blocks):
    tr = TOKEN_TILE_ROWS
    bm = xs_buf.shape[1] // tr
    i = pl.program_id(0)
    flags = flag_ref[i]
    blk = blk_ref[i]
    slots = xs_buf.shape[0]
    slot = lax.rem(blk, slots)

    def token_tile(ref, row):
        return ref.at[pl.ds(pl.multiple_of(row * tr, tr), tr), :]

    def gather(b, s):
        return [pltpu.make_async_copy(
            token_tile(h_ref, inv_ref[b * bm + r] & (n_tok - 1)),
            xs_buf.at[s, pl.ds(r * tr, tr), :], gsem.at[s]) for r in range(bm)]

    def scatter(b, s):
        return [pltpu.make_async_copy(
            y_buf.at[s, pl.ds(r * tr, tr), :],
            token_tile(out_ref, inv_ref[b * bm + r]), ssem.at[s]) for r in range(bm)]

    def wait_gather(s):
        pltpu.make_async_copy(h_ref.at[pl.ds(0, bm * tr), :], xs_buf.at[s], gsem.at[s]).wait()

    def wait_scatter(s):
        pltpu.make_async_copy(y_buf.at[s], out_ref.at[pl.ds(0, bm * tr), :], ssem.at[s]).wait()

    def start_all(copies):
        for r, c in enumerate(copies):
            c.start(priority=r % 2)

    @pl.when(i == 0)
    def _():
        for b in range(slots - 1):
            start_all(gather(b, b))

    def weight_copies(e):
        return (pltpu.make_async_copy(wgu_hbm.at[e], wgu_f32, wsem.at[0]),
                pltpu.make_async_copy(wdn_hbm.at[e], wdn_f32, wsem.at[1]))

    @pl.when(i == 0)
    def _():
        for c in weight_copies(exp_ref[0]):
            c.start()

    @pl.when((flags & _ITEM_NEW_EXPERT) != 0)
    def _():
        for c in weight_copies(exp_ref[i]):
            c.wait()
        wgu_bf[...] = wgu_f32[...].astype(BF16)
        wdn_bf[...] = wdn_f32[...].astype(BF16)

        @pl.when((flags & _ITEM_HAS_NEXT_EXPERT) != 0)
        def _():
            for c in weight_copies(next_exp_ref[i]):
                c.start()

    bgu = bgu_ref[exp_ref[i]]
    bdn = bdn_ref[exp_ref[i]]

    dyn_slot = (slot,)
    half = bm // 2

    def load_x(r0, n_rows):
        return _load_token_tiles(xs_buf, dyn_slot, n_rows, r0).astype(BF16)

    def ffn_rows(x, r0, merge):
        n_rows = x.shape[0]
        gu = _dot(x, wgu_bf[...]) + bgu
        gate = jnp.minimum(gu[:, :D_FF], SWIGLU_LIMIT)
        up = jnp.clip(gu[:, D_FF:], -SWIGLU_LIMIT, SWIGLU_LIMIT)
        act = gate * jax.nn.sigmoid(SWIGLU_ALPHA * gate) * (up + 1.0)
        yb = _dot(act.astype(BF16), wdn_bf[...]) + bdn
        row = r0 + lax.broadcasted_iota(I32, yb.shape, 0)
        mine = (row >= lo_ref[i]) & (row < hi_ref[i])
        other = _load_token_tiles(y_buf, dyn_slot, n_rows, r0) if merge else 0.0
        return jnp.where(mine, yb, other)

    def first_item(s, has_prev):
        wait_gather(s)
        x_top = load_x(0, half)
        ahead = slots - 1
        start_all(gather(jnp.minimum(blk + ahead, n_blocks - 1), (s + ahead) % slots))
        _store_token_tiles(y_buf, dyn_slot, ffn_rows(x_top, 0, False), 0)
        if has_prev:
            start_all(scatter(blk - 1, (s - 1) % slots))
        _store_token_tiles(y_buf, dyn_slot, ffn_rows(load_x(half, half), half, False), half)

    first = (flags & (_ITEM_VALID | _ITEM_FIRST)) == (_ITEM_VALID | _ITEM_FIRST)
    later = (flags & (_ITEM_VALID | _ITEM_FIRST)) == _ITEM_VALID

    @pl.when(first & (blk == 0))
    def _():
        first_item(0, False)

    @pl.when(first & (blk >= slots))
    def _():
        for s in range(slots):
            @pl.when(slot == s)
            def _():
                wait_scatter(s)

    for s in range(slots):
        @pl.when(first & (slot == s) & (blk > 0))
        def _():
            first_item(s, True)

    for r0 in (0, half):
        @pl.when(later & (lo_ref[i] < r0 + half) & (hi_ref[i] > r0))
        def _():
            _store_token_tiles(y_buf, dyn_slot, ffn_rows(load_x(r0, half), r0, True), r0)

    @pl.when(i == pl.num_programs(0) - 1)
    def _():
        start_all(scatter(n_blocks - 1, (n_blocks - 1) % slots))
        for s in range(slots):
            wait_scatter(s)
        for ahead in range(1, slots):
            wait_gather((n_blocks - 1 + ahead) % slots)


def _expert_items(counts, n_rows):
    bm = EXPERT_ROWS
    n_items = n_rows // bm + N_EXPERTS - 1
    ends = jnp.cumsum(counts)
    starts = ends - counts
    blocks_of = jnp.where(counts > 0, (ends - 1) // bm - starts // bm + 1, 0)
    item_end = jnp.cumsum(blocks_of)
    item_start = item_end - blocks_of
    total = item_end[-1]
    idx = jnp.arange(n_items, dtype=I32)
    valid = idx < total
    idc = jnp.minimum(idx, jnp.maximum(total - 1, 0))
    owner = ((item_start[None, :] <= idc[:, None]) & (idc[:, None] < item_end[None, :])).astype(I32)
    pick = lambda v: jnp.sum(owner * v[None, :], axis=1)
    e = pick(jnp.arange(N_EXPERTS, dtype=I32))
    blk = pick(starts // bm) + idc - pick(item_start)
    lo = jnp.clip(pick(starts) - blk * bm, 0, bm)
    hi = jnp.clip(pick(ends) - blk * bm, 0, bm)
    prev_blk = jnp.concatenate([jnp.full((1,), -1, I32), blk[:-1]])
    prev_e = jnp.concatenate([jnp.full((1,), -1, I32), e[:-1]])
    ids = jnp.arange(N_EXPERTS, dtype=I32)
    owners = jnp.where(counts > 0, ids, N_EXPERTS)
    next_of = jnp.min(jnp.where(ids[None, :] > ids[:, None], owners[None, :], N_EXPERTS), axis=1)
    next_e = pick(next_of)
    new_expert = e != prev_e
    flags = jnp.where(valid,
                      _ITEM_VALID
                      + jnp.where(blk != prev_blk, _ITEM_FIRST, 0)
                      + jnp.where(new_expert, _ITEM_NEW_EXPERT, 0)
                      + jnp.where(new_expert & (next_e < N_EXPERTS), _ITEM_HAS_NEXT_EXPERT, 0), 0)
    as_i32 = lambda v: v.astype(I32)
    return (as_i32(blk), as_i32(lo), as_i32(hi), as_i32(flags), as_i32(e),
            as_i32(jnp.minimum(next_e, N_EXPERTS - 1)))


def _experts(items, inv, h, wgu, bgu, wdn, bdn):
    tr, d = TOKEN_TILE_ROWS, D_MODEL
    n = h.shape[0] // tr
    bm = EXPERT_ROWS
    n_rows = inv.shape[0]
    n_blocks = n_rows // bm
    assert n_rows % bm == 0 and n_blocks >= EXPERT_SLOTS and n & (n - 1) == 0
    blk, lo, hi, flags, e, next_e = items
    whole = lambda i, *prefetch: (0, 0, 0)
    return pl.pallas_call(
        functools.partial(_expert_kernel, n_tok=n, n_blocks=n_blocks),
        grid_spec=pltpu.PrefetchScalarGridSpec(
            num_scalar_prefetch=7, grid=(blk.shape[0],),
            in_specs=[pl.BlockSpec(memory_space=pl.ANY),
                      pl.BlockSpec(memory_space=pl.ANY),
                      pl.BlockSpec((N_EXPERTS, 1, 2 * D_FF), whole),
                      pl.BlockSpec(memory_space=pl.ANY),
                      pl.BlockSpec((N_EXPERTS, 1, d), whole)],
            out_specs=pl.BlockSpec(memory_space=pl.ANY),
            scratch_shapes=[pltpu.VMEM((d, 2 * D_FF), BF16),
                            pltpu.VMEM((D_FF, d), BF16),
                            pltpu.VMEM((d, 2 * D_FF), F32),
                            pltpu.VMEM((D_FF, d), F32),
                            pltpu.VMEM((EXPERT_SLOTS, bm * tr, V7X_LANES), F32),
                            pltpu.VMEM((EXPERT_SLOTS, bm * tr, V7X_LANES), F32),
                            pltpu.SemaphoreType.DMA((EXPERT_SLOTS,)),
                            pltpu.SemaphoreType.DMA((EXPERT_SLOTS,)),
                            pltpu.SemaphoreType.DMA((2,))]),
        out_shape=jax.ShapeDtypeStruct((n_rows * tr, V7X_LANES), F32),
        compiler_params=_params("arbitrary"),
        name="experts",
    )(blk, lo, hi, flags, e, next_e, inv, h, wgu, bgu, wdn, bdn)


def _combine_kernel(x_ref, gate_ref, g_ref, y_ref, o_ref):
    gates = gate_ref[...]
    moe = jnp.zeros(x_ref.shape, F32)
    for kk in range(TOP_K):
        moe = moe + _load_token_tiles(y_ref, (kk,), x_ref.shape[0]) * gates[:, kk:kk + 1]
    o_ref[...] = _rms(x_ref[...] + moe, g_ref[...])


def _combine(x2d, gates, g, y):
    n, d = x2d.shape
    t = min(COMBINE_TILE, n)
    return pl.pallas_call(
        _combine_kernel,
        grid=(n // t,),
        in_specs=[pl.BlockSpec((t, d), lambda i: (i, 0)),
                  pl.BlockSpec((t, V7X_LANES), lambda i: (i, 0)),
                  pl.BlockSpec((1, d), lambda i: (0, 0)),
                  pl.BlockSpec((TOP_K, t * TOKEN_TILE_ROWS, V7X_LANES), lambda i: (0, i, 0))],
        out_specs=pl.BlockSpec((t, d), lambda i: (i, 0)),
        out_shape=jax.ShapeDtypeStruct((n, d), F32),
        compiler_params=_params("arbitrary"),
        name="combine",
    )(x2d, gates, g, y.reshape(TOP_K, n * TOKEN_TILE_ROWS, V7X_LANES))


def _layer(x, mem, norm_mix_g, w_in, w_pool, pool_scale, w_gate_up, b_gate_up, gla_norm_g, w_out,
           norm_xattn_g, norm_mem_g, w_xq, w_xk, w_xv, w_xo,
           norm_moe_g, w_router, b_router, w_gu, b_gu, w_dn, b_dn, out_g):
    b, s, d = x.shape
    n = b * s
    row = lambda a: a.reshape(1, -1)

    s0, s1, s2, s3, s4 = (POOL_W, POOL_W + GLA_KEY, POOL_W + 2 * GLA_KEY,
                          POOL_W + 2 * GLA_KEY + GLA_VAL, POOL_W + 2 * GLA_KEY + GLA_VAL + GATE_RANK)
    win = jnp.concatenate(
        [w_in[:, :s3], w_in[:, s4:], w_in[:, s3:s4],
         jnp.zeros((d, V7X_LANES - GATE_RANK), w_in.dtype)], axis=1).astype(BF16)
    wgate = jnp.concatenate(
        [w_gate_up, jnp.zeros((V7X_LANES - GATE_RANK, GLA_KEY), w_gate_up.dtype)], axis=0).astype(BF16)
    wrt = w_router.T
    wrt_hi = wrt.astype(BF16)
    wrt_lo = (wrt - wrt_hi.astype(F32)).astype(BF16)

    kmem, vmem = _memkv(mem, row(norm_mem_g), w_xk.astype(BF16), w_xv.astype(BF16))
    x1 = _mix(x, row(norm_mix_g), win, w_pool.astype(BF16), row(pool_scale), wgate,
              row(b_gate_up), row(gla_norm_g), w_out.astype(BF16))
    x2, h, eid, rank, gates, cnt = _attend_route(
        x1, row(norm_xattn_g), w_xq.astype(BF16), kmem, vmem, w_xo.astype(BF16),
        row(norm_moe_g), wrt_hi, wrt_lo, b_router.reshape(-1, 1))
    x2 = x2.reshape(n, d)
    counts = cnt[:, 0].astype(I32)
    starts = (jnp.cumsum(counts) - counts).astype(I32)
    dest = _plan(starts, eid, rank).reshape(TOP_K * n)
    items = _expert_items(counts, TOP_K * n)
    y = _experts(items, _invert(dest), h, w_gu, b_gu.reshape(N_EXPERTS, 1, -1),
                 w_dn, b_dn.reshape(N_EXPERTS, 1, -1))
    out = _combine(x2, gates, row(out_g), y)
    return out.reshape(b, s, d)


def kernel(x, mem, norm_mix_g, w_in, w_pool, pool_scale, w_gate_up, b_gate_up, gla_norm_g, w_out,
           norm_xattn_g, norm_mem_g, w_xq, w_xk, w_xv, w_xo, norm_moe_g, w_router, b_router,
           w_gu, b_gu, w_dn, b_dn, norm_final_g):
    depth = norm_mix_g.shape[0]
    assert depth == 1, "the final rmsnorm is fused into the (single) layer's combine stage"
    return _layer(x, mem, norm_mix_g[0], w_in[0], w_pool[0], pool_scale[0], w_gate_up[0],
                  b_gate_up[0], gla_norm_g[0], w_out[0], norm_xattn_g[0], norm_mem_g[0],
                  w_xq[0], w_xk[0], w_xv[0], w_xo[0], norm_moe_g[0], w_router[0], b_router[0],
                  w_gu[0], b_gu[0], w_dn[0], b_dn[0], norm_final_g)
```

```python
import functools

import jax
import jax.numpy as jnp
from jax import lax
from jax.experimental import pallas as pl
from jax.experimental.pallas import tpu as pltpu

F32 = jnp.float32
BF16 = jnp.bfloat16
I32 = jnp.int32

V7X_LANES = 128
V7X_VMEM_LIMIT_BYTES = 56 * 1024 * 1024

D_MODEL = 1024
POOL_WINDOWS = (2, 4, 8, 16)
POOL_GROUP_W = 128
POOL_W = len(POOL_WINDOWS) * POOL_GROUP_W
GLA_HEADS = 4
GLA_DK = 64
GLA_DV = 128
GLA_KEY = GLA_HEADS * GLA_DK
GLA_VAL = GLA_HEADS * GLA_DV
GATE_RANK = 16
GATE_NORMALIZER = 16.0
CHUNK = 64
XATTN_HEADS = 4
XATTN_HD = D_MODEL // XATTN_HEADS
N_EXPERTS = 32
TOP_K = 4
D_FF = D_MODEL
SWIGLU_LIMIT = 7.0
SWIGLU_ALPHA = 1.702
EPS = 1e-6

_U0 = 0
_Q0 = _U0 + POOL_W
_K0 = _Q0 + GLA_KEY
_V0 = _K0 + GLA_KEY
_R0 = _V0 + GLA_VAL
_G0 = _R0 + GLA_VAL
IN_COLS_PADDED = _G0 + V7X_LANES

SEQ_TILE = 512
POOL_TAIL = 16
PLAN_TILE = 2048
EXPERT_ROWS = 256
EXPERT_SLOTS = 3
COMBINE_TILE = 512


def _rms(x, g):
    ms = jnp.mean(x * x, axis=-1, keepdims=True)
    return x * lax.rsqrt(ms + EPS) * g


def _dot(a, b):
    return jnp.dot(a, b, preferred_element_type=F32)


def _dot_nt(a, b):
    return lax.dot_general(a, b, (((1,), (1,)), ((), ())), preferred_element_type=F32)


def _dot_tn(a, b):
    return lax.dot_general(a, b, (((0,), (0,)), ((), ())), preferred_element_type=F32)


TOKEN_TILE_ROWS = D_MODEL // V7X_LANES


def _load_token_tiles(ref, lead, n_rows, first_row=0):
    start = first_row * TOKEN_TILE_ROWS
    chunks = [ref[lead + (pl.ds(start + c, n_rows, stride=TOKEN_TILE_ROWS), slice(None))]
              for c in range(TOKEN_TILE_ROWS)]
    return jnp.concatenate(chunks, axis=1)


def _store_token_tiles(ref, lead, val, first_row=0):
    start = first_row * TOKEN_TILE_ROWS
    for c in range(TOKEN_TILE_ROWS):
        ref[lead + (pl.ds(start + c, val.shape[0], stride=TOKEN_TILE_ROWS), slice(None))] = (
            val[:, c * V7X_LANES:(c + 1) * V7X_LANES])


def _params(*semantics):
    return pltpu.CompilerParams(dimension_semantics=semantics,
                                vmem_limit_bytes=V7X_VMEM_LIMIT_BYTES)


def _memkv_kernel(mem_ref, g_ref, wk_ref, wv_ref, k_ref, v_ref):
    hm = _rms(mem_ref[0], g_ref[...]).astype(BF16)
    k = _dot(hm, wk_ref[...])
    v = _dot(hm, wv_ref[...])
    k_ref[0] = (k * (XATTN_HD ** -0.5)).astype(BF16)
    v_ref[0] = v.astype(BF16)


def _memkv(mem, g, wk, wv):
    b, m, d = mem.shape
    const = lambda i: (0, 0)
    return pl.pallas_call(
        _memkv_kernel,
        grid=(b,),
        in_specs=[pl.BlockSpec((1, m, d), lambda i: (i, 0, 0)),
                  pl.BlockSpec((1, d), const),
                  pl.BlockSpec((d, d), const),
                  pl.BlockSpec((d, d), const)],
        out_specs=[pl.BlockSpec((1, m, d), lambda i: (i, 0, 0)),
                   pl.BlockSpec((1, m, d), lambda i: (i, 0, 0))],
        out_shape=[jax.ShapeDtypeStruct((b, m, d), BF16)] * 2,
        compiler_params=_params("arbitrary"),
        name="memkv",
    )(mem, g, wk, wv)


def _mix_kernel(x_ref, g_ref, win_ref, wpool_ref, pscale_ref, wgate_ref, bgate_ref,
                gnorm_ref, wout_ref, ltri_ref, o_ref, uprev_ref, state_ref, oacc_ref):
    t = x_ref.shape[1]
    j = pl.program_id(1)

    @pl.when(j == 0)
    def _():
        uprev_ref[...] = jnp.zeros_like(uprev_ref)
        state_ref[...] = jnp.zeros_like(state_ref)

    x = x_ref[0]
    h = _rms(x, g_ref[...]).astype(BF16)
    def project(c0, width):
        return _dot(h, win_ref[:, c0:c0 + width])

    glr = project(_G0, V7X_LANES)
    u = project(_U0, POOL_W)
    gp = _dot(glr.astype(BF16), wgate_ref[...]) + bgate_ref[...]
    q = project(_Q0, GLA_KEY)
    k = project(_K0, GLA_KEY)
    g = jax.nn.log_sigmoid(gp) / GATE_NORMALIZER
    g_hi = g.astype(BF16)
    g_lo = (g - g_hi.astype(F32)).astype(BF16)
    span = ltri_ref.shape[0]
    gc = jnp.concatenate(
        [_dot(ltri_ref[...], g_hi[r0:r0 + span]) + _dot(ltri_ref[...], g_lo[r0:r0 + span])
         for r0 in range(0, t, span)], axis=0)
    v = project(_V0, GLA_VAL)
    r = project(_R0, GLA_VAL)

    u_ext = jnp.concatenate([uprev_ref[...], u], axis=0)
    uprev_ref[...] = u[t - POOL_TAIL:, :]
    row = lax.broadcasted_iota(I32, (t, POOL_GROUP_W), 0)
    pos = (j * t + row + 1).astype(F32)
    pooled = []
    for gi, w in enumerate(POOL_WINDOWS):
        cols = slice(gi * POOL_GROUP_W, (gi + 1) * POOL_GROUP_W)
        wsum = u_ext[:, cols]
        shift = 1
        while shift < w:
            wsum = wsum + pltpu.roll(wsum, shift, axis=0)
            shift *= 2
        p = wsum[POOL_TAIL:, :] / jnp.minimum(pos, float(w)) - u[:, cols]
        pooled.append(_dot(p.astype(BF16), wpool_ref[gi]))
    pool = jnp.concatenate(pooled, axis=1) * pscale_ref[...]

    qe = q * (GLA_DK ** -0.5) * jnp.exp(gc)
    ke = k * jnp.exp(-gc)

    n_chunks = t // CHUNK
    r2 = lax.broadcasted_iota(I32, (2 * CHUNK, V7X_LANES), 0)
    l2 = lax.broadcasted_iota(I32, (2 * CHUNK, V7X_LANES), 1)
    own_lanes = (r2 < CHUNK) == (l2 < GLA_DK)
    causal2 = (r2 & (CHUNK - 1)) >= (l2 & (CHUNK - 1))
    first_lanes = lax.broadcasted_iota(I32, (CHUNK, V7X_LANES), 1) < CHUNK
    brow = lax.broadcasted_iota(I32, (2 * GLA_DK, 2 * GLA_DV), 0)
    bcol = lax.broadcasted_iota(I32, (2 * GLA_DK, 2 * GLA_DV), 1)
    same_head = (brow < GLA_DK) == (bcol < GLA_DV)
    own_values = (brow < CHUNK) == (bcol < GLA_DV)
    pairs = [(c, p) for c in range(n_chunks) for p in range(GLA_HEADS // 2)]

    def part(c, p):
        return (slice(c * CHUNK, (c + 1) * CHUNK), slice(p * V7X_LANES, (p + 1) * V7X_LANES),
                slice(p * 2 * GLA_DV, (p + 1) * 2 * GLA_DV))

    d_state, decay = {}, {}
    for c, p in pairs:
        rows, lanes, vals = part(c, p)
        g_last = gc[c * CHUNK + CHUNK - 1:c * CHUNK + CHUNK, lanes]
        k2 = (k[rows, lanes] * jnp.exp(g_last - gc[rows, lanes])).astype(BF16)
        d_state[c, p] = jnp.where(same_head, _dot_tn(k2, v[rows, vals].astype(BF16)), 0.0)
        dec = jnp.broadcast_to(jnp.exp(g_last), (V7X_LANES, V7X_LANES)).T
        decay[c, p] = jnp.concatenate([dec, dec], axis=1)
    state_in = {}
    states = [state_ref[0], state_ref[1]]
    for c, p in pairs:
        state_in[c, p] = states[p]
        states[p] = states[p] * decay[c, p] + d_state[c, p]
    state_ref[0] = states[0]
    state_ref[1] = states[1]
    for c, p in pairs:
        rows, lanes, vals = part(c, p)
        qe_p = qe[rows, lanes]
        ke_p = ke[rows, lanes]
        v_f = v[rows, vals]
        q2 = jnp.where(own_lanes, jnp.concatenate([qe_p, qe_p], axis=0), 0.0)
        k2x = jnp.concatenate([ke_p, ke_p], axis=0)
        a2 = jnp.where(causal2, _dot_nt(q2.astype(BF16), k2x.astype(BF16)), 0.0)
        att = jnp.where(first_lanes, a2[:CHUNK], a2[CHUNK:])
        v_bd = jnp.where(own_values, jnp.concatenate([v_f, v_f], axis=0), 0.0).astype(BF16)
        lhs = jnp.concatenate([att, qe_p], axis=1).astype(BF16)
        rhs = jnp.concatenate([v_bd, state_in[c, p].astype(BF16)], axis=0)
        oacc_ref[rows, vals] = _dot(lhs, rhs)

    o = oacc_ref[...]
    gla = []
    for hh in range(GLA_HEADS):
        cols = slice(hh * GLA_DV, (hh + 1) * GLA_DV)
        oh = o[:, cols]
        on = oh * lax.rsqrt(jnp.mean(oh * oh, axis=-1, keepdims=True) + EPS) * gnorm_ref[...]
        gla.append(on * jax.nn.silu(r[:, cols]))
    mix = jnp.concatenate([pool] + gla, axis=1).astype(BF16)
    o_ref[0] = x + _dot(mix, wout_ref[...])


def _mix(x, g, win, wpool, pscale, wgate, bgate, gnorm, wout):
    b, s, d = x.shape
    t = SEQ_TILE
    assert s % t == 0 and t % (2 * CHUNK) == 0 and t >= POOL_TAIL >= max(POOL_WINDOWS) - 1
    ii = jnp.arange(2 * CHUNK)
    ltri = ((ii[:, None] // CHUNK == ii[None, :] // CHUNK)
            & (ii[None, :] <= ii[:, None])).astype(BF16)
    c2 = lambda bi, j: (0, 0)
    c3 = lambda bi, j: (0, 0, 0)
    return pl.pallas_call(
        _mix_kernel,
        grid=(b, s // t),
        in_specs=[pl.BlockSpec((1, t, d), lambda bi, j: (bi, j, 0)),
                  pl.BlockSpec((1, d), c2),
                  pl.BlockSpec((d, IN_COLS_PADDED), c2),
                  pl.BlockSpec((len(POOL_WINDOWS), POOL_GROUP_W, POOL_GROUP_W), c3),
                  pl.BlockSpec((1, POOL_W), c2),
                  pl.BlockSpec((V7X_LANES, GLA_KEY), c2),
                  pl.BlockSpec((1, GLA_KEY), c2),
                  pl.BlockSpec((1, GLA_DV), c2),
                  pl.BlockSpec((POOL_W + GLA_VAL, d), c2),
                  pl.BlockSpec((2 * CHUNK, 2 * CHUNK), c2)],
        out_specs=pl.BlockSpec((1, t, d), lambda bi, j: (bi, j, 0)),
        out_shape=jax.ShapeDtypeStruct((b, s, d), F32),
        scratch_shapes=[pltpu.VMEM((POOL_TAIL, POOL_W), F32),
                        pltpu.VMEM((GLA_HEADS // 2, 2 * GLA_DK, 2 * GLA_DV), F32),
                        pltpu.VMEM((t, GLA_VAL), F32)],
        compiler_params=_params("arbitrary", "arbitrary"),
        name="mix",
    )(x, g, win, wpool, pscale, wgate, bgate, gnorm, wout, ltri)


def _attend_route_kernel(x_ref, g_ref, wq_ref, k_ref, v_ref, wo_ref,
                         gm_ref, wrt_hi_ref, wrt_lo_ref, br_ref, utri_ref,
                         o_ref, h_ref, eid_ref, rank_ref, gate_ref, cnt_ref, carry_ref):
    x = x_ref[0]
    h = _rms(x, g_ref[...]).astype(BF16)
    q = _dot(h, wq_ref[...])
    heads = []
    for hh in range(XATTN_HEADS):
        cols = slice(hh * XATTN_HD, (hh + 1) * XATTN_HD)
        sc = _dot_nt(q[:, cols].astype(BF16), k_ref[0][:, cols])
        e = jnp.exp(sc - jnp.max(sc, axis=-1, keepdims=True))
        p = e / jnp.sum(e, axis=-1, keepdims=True)
        heads.append(_dot(p.astype(BF16), v_ref[0][:, cols]))
    o = jnp.concatenate(heads, axis=1).astype(BF16)
    x2 = x + _dot(o, wo_ref[...])
    o_ref[0] = x2
    first = (pl.program_id(0) == 0) & (pl.program_id(1) == 0)
    _route_tile(x2, first, gm_ref, wrt_hi_ref, wrt_lo_ref, br_ref, utri_ref,
                h_ref, eid_ref, rank_ref, gate_ref, cnt_ref, carry_ref)


def _attend_route(x, g, wq, kmem, vmem, wo, gm, wrt_hi, wrt_lo, br):
    b, s, d = x.shape
    m = kmem.shape[1]
    t = SEQ_TILE
    nt = s // t
    n = b * s
    ii = jnp.arange(t)
    utri = (ii[:, None] < ii[None, :]).astype(BF16)
    c2 = lambda bi, j: (0, 0)
    flat = lambda bi, j: (bi * nt + j, 0)
    flat_lanes = lambda bi, j: (0, bi * nt + j)
    return pl.pallas_call(
        _attend_route_kernel,
        grid=(b, nt),
        in_specs=[pl.BlockSpec((1, t, d), lambda bi, j: (bi, j, 0)),
                  pl.BlockSpec((1, d), c2),
                  pl.BlockSpec((d, d), c2),
                  pl.BlockSpec((1, m, d), lambda bi, j: (bi, 0, 0)),
                  pl.BlockSpec((1, m, d), lambda bi, j: (bi, 0, 0)),
                  pl.BlockSpec((d, d), c2),
                  pl.BlockSpec((1, d), c2),
                  pl.BlockSpec((N_EXPERTS, d), c2),
                  pl.BlockSpec((N_EXPERTS, d), c2),
                  pl.BlockSpec((N_EXPERTS, 1), c2),
                  pl.BlockSpec((t, t), c2)],
        out_specs=[pl.BlockSpec((1, t, d), lambda bi, j: (bi, j, 0)),
                   pl.BlockSpec((t * TOKEN_TILE_ROWS, V7X_LANES), flat),
                   pl.BlockSpec((TOP_K, t), flat_lanes),
                   pl.BlockSpec((TOP_K, t), flat_lanes),
                   pl.BlockSpec((t, V7X_LANES), flat),
                   pl.BlockSpec((N_EXPERTS, V7X_LANES), c2)],
        out_shape=[jax.ShapeDtypeStruct((b, s, d), F32),
                   jax.ShapeDtypeStruct((n * TOKEN_TILE_ROWS, V7X_LANES), F32),
                   jax.ShapeDtypeStruct((TOP_K, n), I32),
                   jax.ShapeDtypeStruct((TOP_K, n), I32),
                   jax.ShapeDtypeStruct((n, V7X_LANES), F32),
                   jax.ShapeDtypeStruct((N_EXPERTS, V7X_LANES), F32)],
        scratch_shapes=[pltpu.VMEM((N_EXPERTS, V7X_LANES), F32)],
        compiler_params=_params("arbitrary", "arbitrary"),
        name="attend_route",
    )(x, g, wq, kmem, vmem, wo, gm, wrt_hi, wrt_lo, br, utri)


def _route_tile(x, first, g_ref, wrt_hi_ref, wrt_lo_ref, br_ref, utri_ref,
                h_ref, eid_ref, rank_ref, gate_ref, cnt_ref, carry_ref):
    t = x.shape[0]

    @pl.when(first)
    def _():
        carry_ref[...] = jnp.zeros_like(carry_ref)

    h = _rms(x, g_ref[...])
    _store_token_tiles(h_ref, (), h)
    h_hi = h.astype(BF16)
    h_lo = (h - h_hi.astype(F32)).astype(BF16)
    logits = (_dot_nt(wrt_hi_ref[...], h_hi) + _dot_nt(wrt_hi_ref[...], h_lo)
              + _dot_nt(wrt_lo_ref[...], h_hi)) + br_ref[...]

    e_iota = lax.broadcasted_iota(I32, (N_EXPERTS, t), 0)
    e_iota_f = e_iota.astype(F32)
    work = logits
    vals, onehots = [], []
    for _ in range(TOP_K):
        m = jnp.max(work, axis=0, keepdims=True)
        idx = jnp.min(jnp.where(work == m, e_iota_f, float(N_EXPERTS)), axis=0, keepdims=True)
        hit = e_iota_f == idx
        vals.append(m)
        onehots.append(hit)
        work = jnp.where(hit, -jnp.inf, work)

    ex = [jnp.exp(vk - vals[0]) for vk in vals]
    den = ex[0] + ex[1] + ex[2] + ex[3]

    member = jnp.zeros((N_EXPERTS, t), F32)
    for hit in onehots:
        member = member + jnp.where(hit, 1.0, 0.0)
    carry = carry_ref[...]
    before = _dot(member.astype(BF16), utri_ref[...]) + carry[:, 0:1]

    row4 = lax.broadcasted_iota(I32, (TOP_K, t), 0)
    grow = lax.broadcasted_iota(I32, (V7X_LANES, t), 0)
    eid = jnp.zeros((TOP_K, t), I32)
    rank = jnp.zeros((TOP_K, t), I32)
    gates_t = jnp.zeros((V7X_LANES, t), F32)
    for kk in range(TOP_K):
        idx_k = jnp.sum(jnp.where(onehots[kk], e_iota_f, 0.0), axis=0, keepdims=True)
        rank_k = jnp.sum(jnp.where(onehots[kk], before, 0.0), axis=0, keepdims=True)
        eid = jnp.where(row4 == kk, idx_k.astype(I32), eid)
        rank = jnp.where(row4 == kk, rank_k.astype(I32), rank)
        gates_t = jnp.where(grow == kk, ex[kk] / den, gates_t)
    eid_ref[...] = eid
    rank_ref[...] = rank
    gate_ref[...] = gates_t.T

    new_carry = carry + jnp.sum(member, axis=1, keepdims=True)
    carry_ref[...] = new_carry
    cnt_ref[...] = new_carry


def _plan_kernel(start_ref, eid_ref, rank_ref, dest_ref):
    e = eid_ref[...]
    dest = rank_ref[...]
    for ee in range(N_EXPERTS):
        dest = dest + jnp.where(e == ee, start_ref[ee], 0)
    dest_ref[...] = dest


def _plan(starts, eid, rank):
    n = eid.shape[1]
    t = min(PLAN_TILE, n)
    blk = pl.BlockSpec((TOP_K, t), lambda i, s: (0, i))
    return pl.pallas_call(
        _plan_kernel,
        grid_spec=pltpu.PrefetchScalarGridSpec(
            num_scalar_prefetch=1, grid=(n // t,), in_specs=[blk, blk], out_specs=blk),
        out_shape=jax.ShapeDtypeStruct((TOP_K, n), I32),
        compiler_params=_params("arbitrary"),
        name="plan",
    )(starts, eid, rank)


def _invert_kernel(dest_ref, inv_ref):
    def body(i, carry):
        inv_ref[dest_ref[i]] = i
        return carry

    lax.fori_loop(0, dest_ref.shape[0], body, 0, unroll=32)


def _invert(dest_flat):
    return pl.pallas_call(
        _invert_kernel,
        grid_spec=pltpu.PrefetchScalarGridSpec(
            num_scalar_prefetch=1, grid=(1,), in_specs=[],
            out_specs=pl.BlockSpec(memory_space=pltpu.SMEM)),
        out_shape=jax.ShapeDtypeStruct(dest_flat.shape, I32),
        compiler_params=_params("arbitrary"),
        name="invert",
    )(dest_flat)


_ITEM_VALID, _ITEM_FIRST, _ITEM_NEW_EXPERT, _ITEM_HAS_NEXT_EXPERT = 1, 2, 4, 8


def _expert_kernel(blk_ref, lo_ref, hi_ref, flag_ref, exp_ref, next_exp_ref, inv_ref,
                   h_ref, wgu_hbm, bgu_ref, wdn_hbm, bdn_ref, out_ref,
                   wgu_bf, wdn_bf, wgu_f32, wdn_f32, xs_buf, y_buf, gsem, ssem, wsem,
                   *, n_tok, n_blocks):
    tr = TOKEN_TILE_ROWS
    bm = xs_buf.shape[1] // tr
    i = pl.program_id(0)
    flags = flag_ref[i]
    blk = blk_ref[i]
    slots = xs_buf.shape[0]
    slot = lax.rem(blk, slots)

    def token_tile(ref, row):
        return ref.at[pl.ds(pl.multiple_of(row * tr, tr), tr), :]

    def gather(b, s):
        return [pltpu.make_async_copy(
            token_tile(h_ref, inv_ref[b * bm + r] & (n_tok - 1)),
            xs_buf.at[s, pl.ds(r * tr, tr), :], gsem.at[s]) for r in range(bm)]

    def scatter(b, s):
        return [pltpu.make_async_copy(
            y_buf.at[s, pl.ds(r * tr, tr), :],
            token_tile(out_ref, inv_ref[b * bm + r]), ssem.at[s]) for r in range(bm)]

    def wait_gather(s):
        pltpu.make_async_copy(h_ref.at[pl.ds(0, bm * tr), :], xs_buf.at[s], gsem.at[s]).wait()

    def wait_scatter(s):
        pltpu.make_async_copy(y_buf.at[s], out_ref.at[pl.ds(0, bm * tr), :], ssem.at[s]).wait()

    def start_all(copies):
        for r, c in enumerate(copies):
            c.start(priority=r % 2)

    @pl.when(i == 0)
    def _():
        for b in range(slots - 1):
            start_all(gather(b, b))

    def weight_copies(e):
        return (pltpu.make_async_copy(wgu_hbm.at[e], wgu_f32, wsem.at[0]),
                pltpu.make_async_copy(wdn_hbm.at[e], wdn_f32, wsem.at[1]))

    @pl.when(i == 0)
    def _():
        for c in weight_copies(exp_ref[0]):
            c.start()

    @pl.when((flags & _ITEM_NEW_EXPERT) != 0)
    def _():
        for c in weight_copies(exp_ref[i]):
            c.wait()
        wgu_bf[...] = wgu_f32[...].astype(BF16)
        wdn_bf[...] = wdn_f32[...].astype(BF16)

        @pl.when((flags & _ITEM_HAS_NEXT_EXPERT) != 0)
        def _():
            for c in weight_copies(next_exp_ref[i]):
                c.start()

    bgu = bgu_ref[exp_ref[i]]
    bdn = bdn_ref[exp_ref[i]]

    dyn_slot = (slot,)
    half = bm // 2

    def load_x(r0, n_rows):
        return _load_token_tiles(xs_buf, dyn_slot, n_rows, r0).astype(BF16)

    def ffn_rows(x, r0, merge):
        n_rows = x.shape[0]
        gu = _dot(x, wgu_bf[...]) + bgu
        gate = jnp.minimum(gu[:, :D_FF], SWIGLU_LIMIT)
        up = jnp.clip(gu[:, D_FF:], -SWIGLU_LIMIT, SWIGLU_LIMIT)
        act = gate * jax.nn.sigmoid(SWIGLU_ALPHA * gate) * (up + 1.0)
        yb = _dot(act.astype(BF16), wdn_bf[...]) + bdn
        row = r0 + lax.broadcasted_iota(I32, yb.shape, 0)
        mine = (row >= lo_ref[i]) & (row < hi_ref[i])
        other = _load_token_tiles(y_buf, dyn_slot, n_rows, r0) if merge else 0.0
        return jnp.where(mine, yb, other)

    def first_item(s, has_prev):
        wait_gather(s)
        x_top = load_x(0, half)
        ahead = slots - 1
        start_all(gather(jnp.minimum(blk + ahead, n_blocks - 1), (s + ahead) % slots))
        _store_token_tiles(y_buf, dyn_slot, ffn_rows(x_top, 0, False), 0)
        if has_prev:
            start_all(scatter(blk - 1, (s - 1) % slots))
        _store_token_tiles(y_buf, dyn_slot, ffn_rows(load_x(half, half), half, False), half)

    first = (flags & (_ITEM_VALID | _ITEM_FIRST)) == (_ITEM_VALID | _ITEM_FIRST)
    later = (flags & (_ITEM_VALID | _ITEM_FIRST)) == _ITEM_VALID

    @pl.when(first & (blk == 0))
    def _():
        first_item(0, False)

    @pl.when(first & (blk >= slots))
    def _():
        for s in range(slots):
            @pl.when(slot == s)
            def _():
                wait_scatter(s)

    for s in range(slots):
        @pl.when(first & (slot == s) & (blk > 0))
        def _():
            first_item(s, True)

    for r0 in (0, half):
        @pl.when(later & (lo_ref[i] < r0 + half) & (hi_ref[i] > r0))
        def _():
            _store_token_tiles(y_buf, dyn_slot, ffn_rows(load_x(r0, half), r0, True), r0)

    @pl.when(i == pl.num_programs(0) - 1)
    def _():
        start_all(scatter(n_blocks - 1, (n_blocks - 1) % slots))
        for s in range(slots):
            wait_scatter(s)
        for ahead in range(1, slots):
            wait_gather((n_blocks - 1 + ahead) % slots)


def _expert_items(counts, n_rows):
    bm = EXPERT_ROWS
    n_items = n_rows // bm + N_EXPERTS - 1
    ends = jnp.cumsum(counts)
    starts = ends - counts
    blocks_of = jnp.where(counts > 0, (ends - 1) // bm - starts // bm + 1, 0)
    item_end = jnp.cumsum(blocks_of)
    item_start = item_end - blocks_of
    total = item_end[-1]
    idx = jnp.arange(n_items, dtype=I32)
    valid = idx < total
    idc = jnp.minimum(idx, jnp.maximum(total - 1, 0))
    owner = ((item_start[None, :] <= idc[:, None]) & (idc[:, None] < item_end[None, :])).astype(I32)
    pick = lambda v: jnp.sum(owner * v[None, :], axis=1)
    e = pick(jnp.arange(N_EXPERTS, dtype=I32))
    blk = pick(starts // bm) + idc - pick(item_start)
    lo = jnp.clip(pick(starts) - blk * bm, 0, bm)
    hi = jnp.clip(pick(ends) - blk * bm, 0, bm)
    prev_blk = jnp.concatenate([jnp.full((1,), -1, I32), blk[:-1]])
    prev_e = jnp.concatenate([jnp.full((1,), -1, I32), e[:-1]])
    ids = jnp.arange(N_EXPERTS, dtype=I32)
    owners = jnp.where(counts > 0, ids, N_EXPERTS)
    next_of = jnp.min(jnp.where(ids[None, :] > ids[:, None], owners[None, :], N_EXPERTS), axis=1)
    next_e = pick(next_of)
    new_expert = e != prev_e
    flags = jnp.where(valid,
                      _ITEM_VALID
                      + jnp.where(blk != prev_blk, _ITEM_FIRST, 0)
                      + jnp.where(new_expert, _ITEM_NEW_EXPERT, 0)
                      + jnp.where(new_expert & (next_e < N_EXPERTS), _ITEM_HAS_NEXT_EXPERT, 0), 0)
    as_i32 = lambda v: v.astype(I32)
    return (as_i32(blk), as_i32(lo), as_i32(hi), as_i32(flags), as_i32(e),
            as_i32(jnp.minimum(next_e, N_EXPERTS - 1)))


def _experts(items, inv, h, wgu, bgu, wdn, bdn):
    tr, d = TOKEN_TILE_ROWS, D_MODEL
    n = h.shape[0] // tr
    bm = EXPERT_ROWS
    n_rows = inv.shape[0]
    n_blocks = n_rows // bm
    assert n_rows % bm == 0 and n_blocks >= EXPERT_SLOTS and n & (n - 1) == 0
    blk, lo, hi, flags, e, next_e = items
    whole = lambda i, *prefetch: (0, 0, 0)
    return pl.pallas_call(
        functools.partial(_expert_kernel, n_tok=n, n_blocks=n_blocks),
        grid_spec=pltpu.PrefetchScalarGridSpec(
            num_scalar_prefetch=7, grid=(blk.shape[0],),
            in_specs=[pl.BlockSpec(memory_space=pl.ANY),
                      pl.BlockSpec(memory_space=pl.ANY),
                      pl.BlockSpec((N_EXPERTS, 1, 2 * D_FF), whole),
                      pl.BlockSpec(memory_space=pl.ANY),
                      pl.BlockSpec((N_EXPERTS, 1, d), whole)],
            out_specs=pl.BlockSpec(memory_space=pl.ANY),
            scratch_shapes=[pltpu.VMEM((d, 2 * D_FF), BF16),
                            pltpu.VMEM((D_FF, d), BF16),
                            pltpu.VMEM((d, 2 * D_FF), F32),
                            pltpu.VMEM((D_FF, d), F32),
                            pltpu.VMEM((EXPERT_SLOTS, bm * tr, V7X_LANES), F32),
                            pltpu.VMEM((EXPERT_SLOTS, bm * tr, V7X_LANES), F32),
                            pltpu.SemaphoreType.DMA((EXPERT_SLOTS,)),
                            pltpu.SemaphoreType.DMA((EXPERT_SLOTS,)),
                            pltpu.SemaphoreType.DMA((2,))]),
        out_shape=jax.ShapeDtypeStruct((n_rows * tr, V7X_LANES), F32),
        compiler_params=_params("arbitrary"),
        name="experts",
    )(blk, lo, hi, flags, e, next_e, inv, h, wgu, bgu, wdn, bdn)


def _combine_kernel(x_ref, gate_ref, g_ref, y_ref, o_ref):
    gates = gate_ref[...]
    moe = jnp.zeros(x_ref.shape, F32)
    for kk in range(TOP_K):
        moe = moe + _load_token_tiles(y_ref, (kk,), x_ref.shape[0]) * gates[:, kk:kk + 1]
    o_ref[...] = _rms(x_ref[...] + moe, g_ref[...])


def _combine(x2d, gates, g, y):
    n, d = x2d.shape
    t = min(COMBINE_TILE, n)
    return pl.pallas_call(
        _combine_kernel,
        grid=(n // t,),
        in_specs=[pl.BlockSpec((t, d), lambda i: (i, 0)),
                  pl.BlockSpec((t, V7X_LANES), lambda i: (i, 0)),
                  pl.BlockSpec((1, d), lambda i: (0, 0)),
                  pl.BlockSpec((TOP_K, t * TOKEN_TILE_ROWS, V7X_LANES), lambda i: (0, i, 0))],
        out_specs=pl.BlockSpec((t, d), lambda i: (i, 0)),
        out_shape=jax.ShapeDtypeStruct((n, d), F32),
        compiler_params=_params("arbitrary"),
        name="combine",
    )(x2d, gates, g, y.reshape(TOP_K, n * TOKEN_TILE_ROWS, V7X_LANES))


def _layer(x, mem, norm_mix_g, w_in, w_pool, pool_scale, w_gate_up, b_gate_up, gla_norm_g, w_out,
           norm_xattn_g, norm_mem_g, w_xq, w_xk, w_xv, w_xo,
           norm_moe_g, w_router, b_router, w_gu, b_gu, w_dn, b_dn, out_g):
    b, s, d = x.shape
    n = b * s
    row = lambda a: a.reshape(1, -1)

    s0, s1, s2, s3, s4 = (POOL_W, POOL_W + GLA_KEY, POOL_W + 2 * GLA_KEY,
                          POOL_W + 2 * GLA_KEY + GLA_VAL, POOL_W + 2 * GLA_KEY + GLA_VAL + GATE_RANK)
    win = jnp.concatenate(
        [w_in[:, :s3], w_in[:, s4:], w_in[:, s3:s4],
         jnp.zeros((d, V7X_LANES - GATE_RANK), w_in.dtype)], axis=1).astype(BF16)
    wgate = jnp.concatenate(
        [w_gate_up, jnp.zeros((V7X_LANES - GATE_RANK, GLA_KEY), w_gate_up.dtype)], axis=0).astype(BF16)
    wrt = w_router.T
    wrt_hi = wrt.astype(BF16)
    wrt_lo = (wrt - wrt_hi.astype(F32)).astype(BF16)

    kmem, vmem = _memkv(mem, row(norm_mem_g), w_xk.astype(BF16), w_xv.astype(BF16))
    x1 = _mix(x, row(norm_mix_g), win, w_pool.astype(BF16), row(pool_scale), wgate,
              row(b_gate_up), row(gla_norm_g), w_out.astype(BF16))
    x2, h, eid, rank, gates, cnt = _attend_route(
        x1, row(norm_xattn_g), w_xq.astype(BF16), kmem, vmem, w_xo.astype(BF16),
        row(norm_moe_g), wrt_hi, wrt_lo, b_router.reshape(-1, 1))
    x2 = x2.reshape(n, d)
    counts = cnt[:, 0].astype(I32)
    starts = (jnp.cumsum(counts) - counts).astype(I32)
    dest = _plan(starts, eid, rank).reshape(TOP_K * n)
    items = _expert_items(counts, TOP_K * n)
    y = _experts(items, _invert(dest), h, w_gu, b_gu.reshape(N_EXPERTS, 1, -1),
                 w_dn, b_dn.reshape(N_EXPERTS, 1, -1))
    out = _combine(x2, gates, row(out_g), y)
    return out.reshape(b, s, d)


def kernel(x, mem, norm_mix_g, w_in, w_pool, pool_scale, w_gate_up, b_gate_up, gla_norm_g, w_out,
           norm_xattn_g, norm_mem_g, w_xq, w_xk, w_xv, w_xo, norm_moe_g, w_router, b_router,
           w_gu, b_gu, w_dn, b_dn, norm_final_g):
    depth = norm_mix_g.shape[0]
    assert depth == 1, "the final rmsnorm is fused into the (single) layer's combine stage"
    return _layer(x, mem, norm_mix_g[0], w_in[0], w_pool[0], pool_scale[0], w_gate_up[0],
                  b_gate_up[0], gla_norm_g[0], w_out[0], norm_xattn_g[0], norm_mem_g[0],
                  w_xq[0], w_xk[0], w_xv[0], w_xo[0], norm_moe_g[0], w_router[0], b_router[0],
                  w_gu[0], b_gu[0], w_dn[0], b_dn[0], norm_final_g)
```

```python
import functools

import jax
import jax.numpy as jnp
from jax import lax
from jax.experimental import pallas as pl
from jax.experimental.pallas import tpu as pltpu

F32 = jnp.float32
BF16 = jnp.bfloat16
I32 = jnp.int32

V7X_LANES = 128
V7X_VMEM_LIMIT_BYTES = 56 * 1024 * 1024

D_MODEL = 1024
POOL_WINDOWS = (2, 4, 8, 16)
POOL_GROUP_W = 128
POOL_W = len(POOL_WINDOWS) * POOL_GROUP_W
GLA_HEADS = 4
GLA_DK = 64
GLA_DV = 128
GLA_KEY = GLA_HEADS * GLA_DK
GLA_VAL = GLA_HEADS * GLA_DV
GATE_RANK = 16
GATE_NORMALIZER = 16.0
CHUNK = 64
XATTN_HEADS = 4
XATTN_HD = D_MODEL // XATTN_HEADS
N_EXPERTS = 32
TOP_K = 4
D_FF = D_MODEL
SWIGLU_LIMIT = 7.0
SWIGLU_ALPHA = 1.702
EPS = 1e-6

_U0 = 0
_Q0 = _U0 + POOL_W
_K0 = _Q0 + GLA_KEY
_V0 = _K0 + GLA_KEY
_G0 = _V0 + GLA_VAL
_R0 = _G0 + GATE_RANK

SEQ_TILE = 512
POOL_TAIL = 16
PLAN_TILE = 2048
EXPERT_ROWS = 256
EXPERT_SLOTS = 3
COMBINE_TILE = 512


def _rms(x, g):
    ms = jnp.mean(x * x, axis=-1, keepdims=True)
    return x * lax.rsqrt(ms + EPS) * g


def _dot(a, b):
    return jnp.dot(a, b, preferred_element_type=F32)


def _dot_nt(a, b):
    return lax.dot_general(a, b, (((1,), (1,)), ((), ())), preferred_element_type=F32)


def _dot_tn(a, b):
    return lax.dot_general(a, b, (((0,), (0,)), ((), ())), preferred_element_type=F32)


TOKEN_TILE_ROWS = D_MODEL // V7X_LANES


def _load_token_tiles(ref, lead, n_rows, first_row=0):
    start = first_row * TOKEN_TILE_ROWS
    chunks = [ref[lead + (pl.ds(start + c, n_rows, stride=TOKEN_TILE_ROWS), slice(None))]
              for c in range(TOKEN_TILE_ROWS)]
    return jnp.concatenate(chunks, axis=1)


def _store_token_tiles(ref, lead, val, first_row=0):
    start = first_row * TOKEN_TILE_ROWS
    for c in range(TOKEN_TILE_ROWS):
        ref[lead + (pl.ds(start + c, val.shape[0], stride=TOKEN_TILE_ROWS), slice(None))] = (
            val[:, c * V7X_LANES:(c + 1) * V7X_LANES])


def _params(*semantics):
    return pltpu.CompilerParams(dimension_semantics=semantics,
                                vmem_limit_bytes=V7X_VMEM_LIMIT_BYTES)


def _memkv_kernel(mem_ref, g_ref, wk_ref, wv_ref, k_ref, v_ref):
    hm = _rms(mem_ref[0], g_ref[...]).astype(BF16)
    k = _dot(hm, wk_ref[...])
    v = _dot(hm, wv_ref[...])
    k_ref[0] = (k * (XATTN_HD ** -0.5)).astype(BF16)
    v_ref[0] = v.astype(BF16)


def _memkv(mem, g, wk, wv):
    b, m, d = mem.shape
    const = lambda i: (0, 0)
    return pl.pallas_call(
        _memkv_kernel,
        grid=(b,),
        in_specs=[pl.BlockSpec((1, m, d), lambda i: (i, 0, 0)),
                  pl.BlockSpec((1, d), const),
                  pl.BlockSpec((d, d), const),
                  pl.BlockSpec((d, d), const)],
        out_specs=[pl.BlockSpec((1, m, d), lambda i: (i, 0, 0)),
                   pl.BlockSpec((1, m, d), lambda i: (i, 0, 0))],
        out_shape=[jax.ShapeDtypeStruct((b, m, d), BF16)] * 2,
        compiler_params=_params("arbitrary"),
        name="memkv",
    )(mem, g, wk, wv)


def _mix_kernel(x_ref, g_ref, wa_ref, wg_ref, wr_ref, wpool_ref, pscale_ref, wgate_ref, bgate_ref,
                gnorm_ref, wout_ref, ltri_ref, o_ref, uprev_ref, state_ref, oacc_ref):
    t = x_ref.shape[1]
    j = pl.program_id(1)

    @pl.when(j == 0)
    def _():
        uprev_ref[...] = jnp.zeros_like(uprev_ref)
        state_ref[...] = jnp.zeros_like(state_ref)

    x = x_ref[0]
    h = _rms(x, g_ref[...]).astype(BF16)
    def project(c0, width):
        return _dot(h, wa_ref[:, c0:c0 + width])

    glr = _dot(h, wg_ref[...])
    u = project(_U0, POOL_W)
    gp = _dot(glr.astype(BF16), wgate_ref[...]) + bgate_ref[...]
    q = project(_Q0, GLA_KEY)
    k = project(_K0, GLA_KEY)
    g = jax.nn.log_sigmoid(gp) / GATE_NORMALIZER
    g_hi = g.astype(BF16)
    g_lo = (g - g_hi.astype(F32)).astype(BF16)
    span = ltri_ref.shape[0]
    gc = jnp.concatenate(
        [_dot(ltri_ref[...], g_hi[r0:r0 + span]) + _dot(ltri_ref[...], g_lo[r0:r0 + span])
         for r0 in range(0, t, span)], axis=0)
    v = project(_V0, GLA_VAL)
    r = _dot(h, wr_ref[...])

    u_ext = jnp.concatenate([uprev_ref[...], u], axis=0)
    uprev_ref[...] = u[t - POOL_TAIL:, :]
    row = lax.broadcasted_iota(I32, (t, POOL_GROUP_W), 0)
    pos = (j * t + row + 1).astype(F32)
    pooled = []
    for gi, w in enumerate(POOL_WINDOWS):
        cols = slice(gi * POOL_GROUP_W, (gi + 1) * POOL_GROUP_W)
        wsum = u_ext[:, cols]
        shift = 1
        while shift < w:
            wsum = wsum + pltpu.roll(wsum, shift, axis=0)
            shift *= 2
        p = wsum[POOL_TAIL:, :] / jnp.minimum(pos, float(w)) - u[:, cols]
        pooled.append(_dot(p.astype(BF16), wpool_ref[gi]))
    pool = jnp.concatenate(pooled, axis=1) * pscale_ref[...]

    qe = q * (GLA_DK ** -0.5) * jnp.exp(gc)
    ke = k * jnp.exp(-gc)

    n_chunks = t // CHUNK
    r2 = lax.broadcasted_iota(I32, (2 * CHUNK, V7X_LANES), 0)
    l2 = lax.broadcasted_iota(I32, (2 * CHUNK, V7X_LANES), 1)
    own_lanes = (r2 < CHUNK) == (l2 < GLA_DK)
    causal2 = (r2 & (CHUNK - 1)) >= (l2 & (CHUNK - 1))
    first_lanes = lax.broadcasted_iota(I32, (CHUNK, V7X_LANES), 1) < CHUNK
    brow = lax.broadcasted_iota(I32, (2 * GLA_DK, 2 * GLA_DV), 0)
    bcol = lax.broadcasted_iota(I32, (2 * GLA_DK, 2 * GLA_DV), 1)
    same_head = (brow < GLA_DK) == (bcol < GLA_DV)
    own_values = (brow < CHUNK) == (bcol < GLA_DV)
    pairs = [(c, p) for c in range(n_chunks) for p in range(GLA_HEADS // 2)]

    def part(c, p):
        return (slice(c * CHUNK, (c + 1) * CHUNK), slice(p * V7X_LANES, (p + 1) * V7X_LANES),
                slice(p * 2 * GLA_DV, (p + 1) * 2 * GLA_DV))

    d_state, decay = {}, {}
    for c, p in pairs:
        rows, lanes, vals = part(c, p)
        g_last = gc[c * CHUNK + CHUNK - 1:c * CHUNK + CHUNK, lanes]
        k2 = (k[rows, lanes] * jnp.exp(g_last - gc[rows, lanes])).astype(BF16)
        d_state[c, p] = jnp.where(same_head, _dot_tn(k2, v[rows, vals].astype(BF16)), 0.0)
        dec = jnp.broadcast_to(jnp.exp(g_last), (V7X_LANES, V7X_LANES)).T
        decay[c, p] = jnp.concatenate([dec, dec], axis=1)
    state_in = {}
    states = [state_ref[0], state_ref[1]]
    for c, p in pairs:
        state_in[c, p] = states[p]
        states[p] = states[p] * decay[c, p] + d_state[c, p]
    state_ref[0] = states[0]
    state_ref[1] = states[1]
    for c, p in pairs:
        rows, lanes, vals = part(c, p)
        qe_p = qe[rows, lanes]
        ke_p = ke[rows, lanes]
        v_f = v[rows, vals]
        q2 = jnp.where(own_lanes, jnp.concatenate([qe_p, qe_p], axis=0), 0.0)
        k2x = jnp.concatenate([ke_p, ke_p], axis=0)
        a2 = jnp.where(causal2, _dot_nt(q2.astype(BF16), k2x.astype(BF16)), 0.0)
        att = jnp.where(first_lanes, a2[:CHUNK], a2[CHUNK:])
        v_bd = jnp.where(own_values, jnp.concatenate([v_f, v_f], axis=0), 0.0).astype(BF16)
        lhs = jnp.concatenate([att, qe_p], axis=1).astype(BF16)
        rhs = jnp.concatenate([v_bd, state_in[c, p].astype(BF16)], axis=0)
        oacc_ref[rows, vals] = _dot(lhs, rhs)

    o = oacc_ref[...]
    gla = []
    for hh in range(GLA_HEADS):
        cols = slice(hh * GLA_DV, (hh + 1) * GLA_DV)
        oh = o[:, cols]
        on = oh * lax.rsqrt(jnp.mean(oh * oh, axis=-1, keepdims=True) + EPS) * gnorm_ref[...]
        gla.append(on * jax.nn.silu(r[:, cols]))
    mix = jnp.concatenate([pool] + gla, axis=1).astype(BF16)
    o_ref[0] = x + _dot(mix, wout_ref[...])


def _mix(x, g, wa, wg, wr, wpool, pscale, wgate, bgate, gnorm, wout):
    b, s, d = x.shape
    t = SEQ_TILE
    assert s % t == 0 and t % (2 * CHUNK) == 0 and t >= POOL_TAIL >= max(POOL_WINDOWS) - 1
    assert 2 * GLA_DK == V7X_LANES and GLA_HEADS % 2 == 0 and CHUNK & (CHUNK - 1) == 0
    ii = jnp.arange(2 * CHUNK)
    ltri = ((ii[:, None] // CHUNK == ii[None, :] // CHUNK)
            & (ii[None, :] <= ii[:, None])).astype(BF16)
    c2 = lambda bi, j: (0, 0)
    c3 = lambda bi, j: (0, 0, 0)
    return pl.pallas_call(
        _mix_kernel,
        grid=(b, s // t),
        in_specs=[pl.BlockSpec((1, t, d), lambda bi, j: (bi, j, 0)),
                  pl.BlockSpec((1, d), c2),
                  pl.BlockSpec((d, _G0), c2),
                  pl.BlockSpec((d, V7X_LANES), c2),
                  pl.BlockSpec((d, GLA_VAL), c2),
                  pl.BlockSpec((len(POOL_WINDOWS), POOL_GROUP_W, POOL_GROUP_W), c3),
                  pl.BlockSpec((1, POOL_W), c2),
                  pl.BlockSpec((V7X_LANES, GLA_KEY), c2),
                  pl.BlockSpec((1, GLA_KEY), c2),
                  pl.BlockSpec((1, GLA_DV), c2),
                  pl.BlockSpec((POOL_W + GLA_VAL, d), c2),
                  pl.BlockSpec((2 * CHUNK, 2 * CHUNK), c2)],
        out_specs=pl.BlockSpec((1, t, d), lambda bi, j: (bi, j, 0)),
        out_shape=jax.ShapeDtypeStruct((b, s, d), F32),
        scratch_shapes=[pltpu.VMEM((POOL_TAIL, POOL_W), F32),
                        pltpu.VMEM((GLA_HEADS // 2, 2 * GLA_DK, 2 * GLA_DV), F32),
                        pltpu.VMEM((t, GLA_VAL), F32)],
        compiler_params=_params("arbitrary", "arbitrary"),
        name="mix",
    )(x, g, wa, wg, wr, wpool, pscale, wgate, bgate, gnorm, wout, ltri)


def _attend_route_kernel(x_ref, g_ref, wq_ref, k_ref, v_ref, wo_ref,
                         gm_ref, wrt_hi_ref, wrt_lo_ref, br_ref, utri_ref,
                         o_ref, h_ref, eid_ref, rank_ref, gate_ref, cnt_ref, carry_ref):
    x = x_ref[0]
    h = _rms(x, g_ref[...]).astype(BF16)
    q = _dot(h, wq_ref[...])
    heads = []
    for hh in range(XATTN_HEADS):
        cols = slice(hh * XATTN_HD, (hh + 1) * XATTN_HD)
        sc = _dot_nt(q[:, cols].astype(BF16), k_ref[0][:, cols])
        e = jnp.exp(sc - jnp.max(sc, axis=-1, keepdims=True))
        p = e / jnp.sum(e, axis=-1, keepdims=True)
        heads.append(_dot(p.astype(BF16), v_ref[0][:, cols]))
    o = jnp.concatenate(heads, axis=1).astype(BF16)
    x2 = x + _dot(o, wo_ref[...])
    o_ref[0] = x2
    first = (pl.program_id(0) == 0) & (pl.program_id(1) == 0)
    _route_tile(x2, first, gm_ref, wrt_hi_ref, wrt_lo_ref, br_ref, utri_ref,
                h_ref, eid_ref, rank_ref, gate_ref, cnt_ref, carry_ref)


def _attend_route(x, g, wq, kmem, vmem, wo, gm, wrt_hi, wrt_lo, br):
    b, s, d = x.shape
    m = kmem.shape[1]
    t = SEQ_TILE
    nt = s // t
    n = b * s
    ii = jnp.arange(t)
    utri = (ii[:, None] < ii[None, :]).astype(BF16)
    c2 = lambda bi, j: (0, 0)
    flat = lambda bi, j: (bi * nt + j, 0)
    flat_lanes = lambda bi, j: (0, bi * nt + j)
    return pl.pallas_call(
        _attend_route_kernel,
        grid=(b, nt),
        in_specs=[pl.BlockSpec((1, t, d), lambda bi, j: (bi, j, 0)),
                  pl.BlockSpec((1, d), c2),
                  pl.BlockSpec((d, d), c2),
                  pl.BlockSpec((1, m, d), lambda bi, j: (bi, 0, 0)),
                  pl.BlockSpec((1, m, d), lambda bi, j: (bi, 0, 0)),
                  pl.BlockSpec((d, d), c2),
                  pl.BlockSpec((1, d), c2),
                  pl.BlockSpec((N_EXPERTS, d), c2),
                  pl.BlockSpec((N_EXPERTS, d), c2),
                  pl.BlockSpec((N_EXPERTS, 1), c2),
                  pl.BlockSpec((t, t), c2)],
        out_specs=[pl.BlockSpec((1, t, d), lambda bi, j: (bi, j, 0)),
                   pl.BlockSpec((t * TOKEN_TILE_ROWS, V7X_LANES), flat),
                   pl.BlockSpec((TOP_K, t), flat_lanes),
                   pl.BlockSpec((TOP_K, t), flat_lanes),
                   pl.BlockSpec((t, V7X_LANES), flat),
                   pl.BlockSpec((N_EXPERTS, V7X_LANES), c2)],
        out_shape=[jax.ShapeDtypeStruct((b, s, d), F32),
                   jax.ShapeDtypeStruct((n * TOKEN_TILE_ROWS, V7X_LANES), F32),
                   jax.ShapeDtypeStruct((TOP_K, n), I32),
                   jax.ShapeDtypeStruct((TOP_K, n), I32),
                   jax.ShapeDtypeStruct((n, V7X_LANES), F32),
                   jax.ShapeDtypeStruct((N_EXPERTS, V7X_LANES), F32)],
        scratch_shapes=[pltpu.VMEM((N_EXPERTS, V7X_LANES), F32)],
        compiler_params=_params("arbitrary", "arbitrary"),
        name="attend_route",
    )(x, g, wq, kmem, vmem, wo, gm, wrt_hi, wrt_lo, br, utri)


def _route_tile(x, first, g_ref, wrt_hi_ref, wrt_lo_ref, br_ref, utri_ref,
                h_ref, eid_ref, rank_ref, gate_ref, cnt_ref, carry_ref):
    t = x.shape[0]

    @pl.when(first)
    def _():
        carry_ref[...] = jnp.zeros_like(carry_ref)

    h = _rms(x, g_ref[...])
    _store_token_tiles(h_ref, (), h)
    h_hi = h.astype(BF16)
    h_lo = (h - h_hi.astype(F32)).astype(BF16)
    logits = (_dot_nt(wrt_hi_ref[...], h_hi) + _dot_nt(wrt_hi_ref[...], h_lo)
              + _dot_nt(wrt_lo_ref[...], h_hi)) + br_ref[...]

    e_iota = lax.broadcasted_iota(I32, (N_EXPERTS, t), 0)
    e_iota_f = e_iota.astype(F32)
    work = logits
    vals, onehots = [], []
    for _ in range(TOP_K):
        m = jnp.max(work, axis=0, keepdims=True)
        idx = jnp.min(jnp.where(work == m, e_iota_f, float(N_EXPERTS)), axis=0, keepdims=True)
        hit = e_iota_f == idx
        vals.append(m)
        onehots.append(hit)
        work = jnp.where(hit, -jnp.inf, work)

    ex = [jnp.exp(vk - vals[0]) for vk in vals]
    den = ex[0] + ex[1] + ex[2] + ex[3]

    member = jnp.zeros((N_EXPERTS, t), F32)
    for hit in onehots:
        member = member + jnp.where(hit, 1.0, 0.0)
    carry = carry_ref[...]
    before = _dot(member.astype(BF16), utri_ref[...]) + carry[:, 0:1]

    row4 = lax.broadcasted_iota(I32, (TOP_K, t), 0)
    grow = lax.broadcasted_iota(I32, (V7X_LANES, t), 0)
    eid = jnp.zeros((TOP_K, t), I32)
    rank = jnp.zeros((TOP_K, t), I32)
    gates_t = jnp.zeros((V7X_LANES, t), F32)
    for kk in range(TOP_K):
        idx_k = jnp.sum(jnp.where(onehots[kk], e_iota_f, 0.0), axis=0, keepdims=True)
        rank_k = jnp.sum(jnp.where(onehots[kk], before, 0.0), axis=0, keepdims=True)
        eid = jnp.where(row4 == kk, idx_k.astype(I32), eid)
        rank = jnp.where(row4 == kk, rank_k.astype(I32), rank)
        gates_t = jnp.where(grow == kk, ex[kk] / den, gates_t)
    eid_ref[...] = eid
    rank_ref[...] = rank
    gate_ref[...] = gates_t.T

    new_carry = carry + jnp.sum(member, axis=1, keepdims=True)
    carry_ref[...] = new_carry
    cnt_ref[...] = new_carry


def _plan_kernel(start_ref, eid_ref, rank_ref, dest_ref):
    e = eid_ref[...]
    dest = rank_ref[...]
    for ee in range(N_EXPERTS):
        dest = dest + jnp.where(e == ee, start_ref[ee], 0)
    dest_ref[...] = dest


def _plan(starts, eid, rank):
    n = eid.shape[1]
    t = min(PLAN_TILE, n)
    blk = pl.BlockSpec((TOP_K, t), lambda i, s: (0, i))
    return pl.pallas_call(
        _plan_kernel,
        grid_spec=pltpu.PrefetchScalarGridSpec(
            num_scalar_prefetch=1, grid=(n // t,), in_specs=[blk, blk], out_specs=blk),
        out_shape=jax.ShapeDtypeStruct((TOP_K, n), I32),
        compiler_params=_params("arbitrary"),
        name="plan",
    )(starts, eid, rank)


def _invert_kernel(dest_ref, inv_ref):
    def body(i, carry):
        inv_ref[dest_ref[i]] = i
        return carry

    lax.fori_loop(0, dest_ref.shape[0], body, 0, unroll=32)


def _invert(dest_flat):
    return pl.pallas_call(
        _invert_kernel,
        grid_spec=pltpu.PrefetchScalarGridSpec(
            num_scalar_prefetch=1, grid=(1,), in_specs=[],
            out_specs=pl.BlockSpec(memory_space=pltpu.SMEM)),
        out_shape=jax.ShapeDtypeStruct(dest_flat.shape, I32),
        compiler_params=_params("arbitrary"),
        name="invert",
    )(dest_flat)


_ITEM_VALID, _ITEM_FIRST, _ITEM_NEW_EXPERT, _ITEM_HAS_NEXT_EXPERT = 1, 2, 4, 8


def _expert_kernel(blk_ref, lo_ref, hi_ref, flag_ref, exp_ref, next_exp_ref, inv_ref,
                   h_ref, wgu_hbm, bgu_ref, wdn_hbm, bdn_ref, out_ref,
                   wgu_bf, wdn_bf, wgu_f32, wdn_f32, xs_buf, y_buf, gsem, ssem, wsem,
                   *, n_tok, n_blocks):
    tr = TOKEN_TILE_ROWS
    bm = xs_buf.shape[1] // tr
    i = pl.program_id(0)
    flags = flag_ref[i]
    blk = blk_ref[i]
    slots = xs_buf.shape[0]
    slot = lax.rem(blk, slots)

    def token_tile(ref, row):
        return ref.at[pl.ds(pl.multiple_of(row * tr, tr), tr), :]

    def gather(b, s):
        return [pltpu.make_async_copy(
            token_tile(h_ref, inv_ref[b * bm + r] & (n_tok - 1)),
            xs_buf.at[s, pl.ds(r * tr, tr), :], gsem.at[s]) for r in range(bm)]

    def scatter(b, s):
        return [pltpu.make_async_copy(
            y_buf.at[s, pl.ds(r * tr, tr), :],
            token_tile(out_ref, inv_ref[b * bm + r]), ssem.at[s]) for r in range(bm)]

    def wait_gather(s):
        pltpu.make_async_copy(h_ref.at[pl.ds(0, bm * tr), :], xs_buf.at[s], gsem.at[s]).wait()

    def wait_scatter(s):
        pltpu.make_async_copy(y_buf.at[s], out_ref.at[pl.ds(0, bm * tr), :], ssem.at[s]).wait()

    def start_all(copies):
        for r, c in enumerate(copies):
            c.start(priority=r % 2)

    @pl.when(i == 0)
    def _():
        for b in range(slots - 1):
            start_all(gather(b, b))

    def weight_copies(e):
        return (pltpu.make_async_copy(wgu_hbm.at[e], wgu_f32, wsem.at[0]),
                pltpu.make_async_copy(wdn_hbm.at[e], wdn_f32, wsem.at[1]))

    @pl.when(i == 0)
    def _():
        for c in weight_copies(exp_ref[0]):
            c.start()

    @pl.when((flags & _ITEM_NEW_EXPERT) != 0)
    def _():
        for c in weight_copies(exp_ref[i]):
            c.wait()
        wgu_bf[...] = wgu_f32[...].astype(BF16)
        wdn_bf[...] = wdn_f32[...].astype(BF16)

        @pl.when((flags & _ITEM_HAS_NEXT_EXPERT) != 0)
        def _():
            for c in weight_copies(next_exp_ref[i]):
                c.start()

    bgu = bgu_ref[exp_ref[i]]
    bdn = bdn_ref[exp_ref[i]]

    dyn_slot = (slot,)
    half = bm // 2

    def load_x(r0, n_rows):
        return _load_token_tiles(xs_buf, dyn_slot, n_rows, r0).astype(BF16)

    def ffn_rows(x, r0, merge):
        n_rows = x.shape[0]
        gu = _dot(x, wgu_bf[...]) + bgu
        gate = jnp.minimum(gu[:, :D_FF], SWIGLU_LIMIT)
        up = jnp.clip(gu[:, D_FF:], -SWIGLU_LIMIT, SWIGLU_LIMIT)
        act = gate * jax.nn.sigmoid(SWIGLU_ALPHA * gate) * (up + 1.0)
        yb = _dot(act.astype(BF16), wdn_bf[...]) + bdn
        row = r0 + lax.broadcasted_iota(I32, yb.shape, 0)
        mine = (row >= lo_ref[i]) & (row < hi_ref[i])
        other = _load_token_tiles(y_buf, dyn_slot, n_rows, r0) if merge else 0.0
        return jnp.where(mine, yb, other)

    def first_item(s, has_prev):
        wait_gather(s)
        x_top = load_x(0, half)
        ahead = slots - 1
        start_all(gather(jnp.minimum(blk + ahead, n_blocks - 1), (s + ahead) % slots))
        _store_token_tiles(y_buf, dyn_slot, ffn_rows(x_top, 0, False), 0)
        if has_prev:
            start_all(scatter(blk - 1, (s - 1) % slots))
        _store_token_tiles(y_buf, dyn_slot, ffn_rows(load_x(half, half), half, False), half)

    first = (flags & (_ITEM_VALID | _ITEM_FIRST)) == (_ITEM_VALID | _ITEM_FIRST)
    later = (flags & (_ITEM_VALID | _ITEM_FIRST)) == _ITEM_VALID

    @pl.when(first & (blk == 0))
    def _():
        first_item(0, False)

    @pl.when(first & (blk >= slots))
    def _():
        for s in range(slots):
            @pl.when(slot == s)
            def _():
                wait_scatter(s)

    for s in range(slots):
        @pl.when(first & (slot == s) & (blk > 0))
        def _():
            first_item(s, True)

    for r0 in (0, half):
        @pl.when(later & (lo_ref[i] < r0 + half) & (hi_ref[i] > r0))
        def _():
            _store_token_tiles(y_buf, dyn_slot, ffn_rows(load_x(r0, half), r0, True), r0)

    @pl.when(i == pl.num_programs(0) - 1)
    def _():
        start_all(scatter(n_blocks - 1, (n_blocks - 1) % slots))
        for s in range(slots):
            wait_scatter(s)
        for ahead in range(1, slots):
            wait_gather((n_blocks - 1 + ahead) % slots)


def _expert_items(counts, n_rows):
    bm = EXPERT_ROWS
    n_items = n_rows // bm + N_EXPERTS - 1
    ends = jnp.cumsum(counts)
    starts = ends - counts
    blocks_of = jnp.where(counts > 0, (ends - 1) // bm - starts // bm + 1, 0)
    item_end = jnp.cumsum(blocks_of)
    item_start = item_end - blocks_of
    total = item_end[-1]
    idx = jnp.arange(n_items, dtype=I32)
    valid = idx < total
    idc = jnp.minimum(idx, jnp.maximum(total - 1, 0))
    owner = ((item_start[None, :] <= idc[:, None]) & (idc[:, None] < item_end[None, :])).astype(I32)
    pick = lambda v: jnp.sum(owner * v[None, :], axis=1)
    e = pick(jnp.arange(N_EXPERTS, dtype=I32))
    blk = pick(starts // bm) + idc - pick(item_start)
    lo = jnp.clip(pick(starts) - blk * bm, 0, bm)
    hi = jnp.clip(pick(ends) - blk * bm, 0, bm)
    prev_blk = jnp.concatenate([jnp.full((1,), -1, I32), blk[:-1]])
    prev_e = jnp.concatenate([jnp.full((1,), -1, I32), e[:-1]])
    ids = jnp.arange(N_EXPERTS, dtype=I32)
    owners = jnp.where(counts > 0, ids, N_EXPERTS)
    next_of = jnp.min(jnp.where(ids[None, :] > ids[:, None], owners[None, :], N_EXPERTS), axis=1)
    next_e = pick(next_of)
    new_expert = e != prev_e
    flags = jnp.where(valid,
                      _ITEM_VALID
                      + jnp.where(blk != prev_blk, _ITEM_FIRST, 0)
                      + jnp.where(new_expert, _ITEM_NEW_EXPERT, 0)
                      + jnp.where(new_expert & (next_e < N_EXPERTS), _ITEM_HAS_NEXT_EXPERT, 0), 0)
    as_i32 = lambda v: v.astype(I32)
    return (as_i32(blk), as_i32(lo), as_i32(hi), as_i32(flags), as_i32(e),
            as_i32(jnp.minimum(next_e, N_EXPERTS - 1)))


def _experts(items, inv, h, wgu, bgu, wdn, bdn):
    tr, d = TOKEN_TILE_ROWS, D_MODEL
    n = h.shape[0] // tr
    bm = EXPERT_ROWS
    n_rows = inv.shape[0]
    n_blocks = n_rows // bm
    assert n_rows % bm == 0 and n_blocks >= EXPERT_SLOTS and n & (n - 1) == 0
    blk, lo, hi, flags, e, next_e = items
    whole = lambda i, *prefetch: (0, 0, 0)
    return pl.pallas_call(
        functools.partial(_expert_kernel, n_tok=n, n_blocks=n_blocks),
        grid_spec=pltpu.PrefetchScalarGridSpec(
            num_scalar_prefetch=7, grid=(blk.shape[0],),
            in_specs=[pl.BlockSpec(memory_space=pl.ANY),
                      pl.BlockSpec(memory_space=pl.ANY),
                      pl.BlockSpec((N_EXPERTS, 1, 2 * D_FF), whole),
                      pl.BlockSpec(memory_space=pl.ANY),
                      pl.BlockSpec((N_EXPERTS, 1, d), whole)],
            out_specs=pl.BlockSpec(memory_space=pl.ANY),
            scratch_shapes=[pltpu.VMEM((d, 2 * D_FF), BF16),
                            pltpu.VMEM((D_FF, d), BF16),
                            pltpu.VMEM((d, 2 * D_FF), F32),
                            pltpu.VMEM((D_FF, d), F32),
                            pltpu.VMEM((EXPERT_SLOTS, bm * tr, V7X_LANES), F32),
                            pltpu.VMEM((EXPERT_SLOTS, bm * tr, V7X_LANES), F32),
                            pltpu.SemaphoreType.DMA((EXPERT_SLOTS,)),
                            pltpu.SemaphoreType.DMA((EXPERT_SLOTS,)),
                            pltpu.SemaphoreType.DMA((2,))]),
        out_shape=jax.ShapeDtypeStruct((n_rows * tr, V7X_LANES), F32),
        compiler_params=_params("arbitrary"),
        name="experts",
    )(blk, lo, hi, flags, e, next_e, inv, h, wgu, bgu, wdn, bdn)


def _combine_kernel(x_ref, gate_ref, g_ref, y_ref, o_ref):
    gates = gate_ref[...]
    moe = jnp.zeros(x_ref.shape, F32)
    for kk in range(TOP_K):
        moe = moe + _load_token_tiles(y_ref, (kk,), x_ref.shape[0]) * gates[:, kk:kk + 1]
    o_ref[...] = _rms(x_ref[...] + moe, g_ref[...])


def _combine(x2d, gates, g, y):
    n, d = x2d.shape
    t = min(COMBINE_TILE, n)
    return pl.pallas_call(
        _combine_kernel,
        grid=(n // t,),
        in_specs=[pl.BlockSpec((t, d), lambda i: (i, 0)),
                  pl.BlockSpec((t, V7X_LANES), lambda i: (i, 0)),
                  pl.BlockSpec((1, d), lambda i: (0, 0)),
                  pl.BlockSpec((TOP_K, t * TOKEN_TILE_ROWS, V7X_LANES), lambda i: (0, i, 0))],
        out_specs=pl.BlockSpec((t, d), lambda i: (i, 0)),
        out_shape=jax.ShapeDtypeStruct((n, d), F32),
        compiler_params=_params("arbitrary"),
        name="combine",
    )(x2d, gates, g, y.reshape(TOP_K, n * TOKEN_TILE_ROWS, V7X_LANES))


def _layer(x, mem, norm_mix_g, w_in, w_pool, pool_scale, w_gate_up, b_gate_up, gla_norm_g, w_out,
           norm_xattn_g, norm_mem_g, w_xq, w_xk, w_xv, w_xo,
           norm_moe_g, w_router, b_router, w_gu, b_gu, w_dn, b_dn, out_g):
    b, s, d = x.shape
    n = b * s
    row = lambda a: a.reshape(1, -1)

    pad = V7X_LANES - GATE_RANK
    wa = w_in[:, :_G0].astype(BF16)
    wg = jnp.pad(w_in[:, _G0:_R0], ((0, 0), (0, pad))).astype(BF16)
    wr = w_in[:, _R0:].astype(BF16)
    wgate = jnp.pad(w_gate_up, ((0, pad), (0, 0))).astype(BF16)
    wrt = w_router.T
    wrt_hi = wrt.astype(BF16)
    wrt_lo = (wrt - wrt_hi.astype(F32)).astype(BF16)

    kmem, vmem = _memkv(mem, row(norm_mem_g), w_xk.astype(BF16), w_xv.astype(BF16))
    x1 = _mix(x, row(norm_mix_g), wa, wg, wr, w_pool.astype(BF16), row(pool_scale), wgate,
              row(b_gate_up), row(gla_norm_g), w_out.astype(BF16))
    x2, h, eid, rank, gates, cnt = _attend_route(
        x1, row(norm_xattn_g), w_xq.astype(BF16), kmem, vmem, w_xo.astype(BF16),
        row(norm_moe_g), wrt_hi, wrt_lo, b_router.reshape(-1, 1))
    x2 = x2.reshape(n, d)
    counts = cnt[:, 0].astype(I32)
    starts = (jnp.cumsum(counts) - counts).astype(I32)
    dest = _plan(starts, eid, rank).reshape(TOP_K * n)
    items = _expert_items(counts, TOP_K * n)
    y = _experts(items, _invert(dest), h, w_gu, b_gu.reshape(N_EXPERTS, 1, -1),
                 w_dn, b_dn.reshape(N_EXPERTS, 1, -1))
    out = _combine(x2, gates, row(out_g), y)
    return out.reshape(b, s, d)


def kernel(x, mem, norm_mix_g, w_in, w_pool, pool_scale, w_gate_up, b_gate_up, gla_norm_g, w_out,
           norm_xattn_g, norm_mem_g, w_xq, w_xk, w_xv, w_xo, norm_moe_g, w_router, b_router,
           w_gu, b_gu, w_dn, b_dn, norm_final_g):
    depth = norm_mix_g.shape[0]
    assert depth == 1, "the final rmsnorm is fused into the (single) layer's combine stage"
    return _layer(x, mem, norm_mix_g[0], w_in[0], w_pool[0], pool_scale[0], w_gate_up[0],
                  b_gate_up[0], gla_norm_g[0], w_out[0], norm_xattn_g[0], norm_mem_g[0],
                  w_xq[0], w_xk[0], w_xv[0], w_xo[0], norm_moe_g[0], w_router[0], b_router[0],
                  w_gu[0], b_gu[0], w_dn[0], b_dn[0], norm_final_g)
```

```python
import functools

import jax
import jax.numpy as jnp
from jax import lax
from jax.experimental import pallas as pl
from jax.experimental.pallas import tpu as pltpu

F32 = jnp.float32
BF16 = jnp.bfloat16
I32 = jnp.int32

V7X_LANES = 128
V7X_VMEM_LIMIT_BYTES = 56 * 1024 * 1024

D_MODEL = 1024
POOL_WINDOWS = (2, 4, 8, 16)
POOL_GROUP_W = 128
POOL_W = len(POOL_WINDOWS) * POOL_GROUP_W
GLA_HEADS = 4
GLA_DK = 64
GLA_DV = 128
GLA_KEY = GLA_HEADS * GLA_DK
GLA_VAL = GLA_HEADS * GLA_DV
GATE_RANK = 16
GATE_NORMALIZER = 16.0
CHUNK = 64
XATTN_HEADS = 4
XATTN_HD = D_MODEL // XATTN_HEADS
N_EXPERTS = 32
TOP_K = 4
D_FF = D_MODEL
SWIGLU_LIMIT = 7.0
SWIGLU_ALPHA = 1.702
EPS = 1e-6

_U0 = 0
_Q0 = _U0 + POOL_W
_K0 = _Q0 + GLA_KEY
_V0 = _K0 + GLA_KEY
_G0 = _V0 + GLA_VAL
_R0 = _G0 + GATE_RANK

SEQ_TILE = 512
POOL_TAIL = 16
PLAN_TILE = 2048
EXPERT_ROWS = 256
EXPERT_SLOTS = 3
COMBINE_TILE = 512


def _rms(x, g):
    ms = jnp.mean(x * x, axis=-1, keepdims=True)
    return x * lax.rsqrt(ms + EPS) * g


def _dot(a, b):
    return jnp.dot(a, b, preferred_element_type=F32)


def _dot_nt(a, b):
    return lax.dot_general(a, b, (((1,), (1,)), ((), ())), preferred_element_type=F32)


def _dot_tn(a, b):
    return lax.dot_general(a, b, (((0,), (0,)), ((), ())), preferred_element_type=F32)


TOKEN_TILE_ROWS = D_MODEL // V7X_LANES


def _load_token_tiles(ref, lead, n_rows, first_row=0):
    start = first_row * TOKEN_TILE_ROWS
    chunks = [ref[lead + (pl.ds(start + c, n_rows, stride=TOKEN_TILE_ROWS), slice(None))]
              for c in range(TOKEN_TILE_ROWS)]
    return jnp.concatenate(chunks, axis=1)


def _store_token_tiles(ref, lead, val, first_row=0):
    start = first_row * TOKEN_TILE_ROWS
    for c in range(TOKEN_TILE_ROWS):
        ref[lead + (pl.ds(start + c, val.shape[0], stride=TOKEN_TILE_ROWS), slice(None))] = (
            val[:, c * V7X_LANES:(c + 1) * V7X_LANES])


def _params(*semantics):
    return pltpu.CompilerParams(dimension_semantics=semantics,
                                vmem_limit_bytes=V7X_VMEM_LIMIT_BYTES)


def _memkv_kernel(mem_ref, g_ref, wk_ref, wv_ref, k_ref, v_ref):
    hm = _rms(mem_ref[0], g_ref[...]).astype(BF16)
    k = _dot(hm, wk_ref[...])
    v = _dot(hm, wv_ref[...])
    k_ref[0] = (k * (XATTN_HD ** -0.5)).astype(BF16)
    v_ref[0] = v.astype(BF16)


def _memkv(mem, g, wk, wv):
    b, m, d = mem.shape
    const = lambda i: (0, 0)
    return pl.pallas_call(
        _memkv_kernel,
        grid=(b,),
        in_specs=[pl.BlockSpec((1, m, d), lambda i: (i, 0, 0)),
                  pl.BlockSpec((1, d), const),
                  pl.BlockSpec((d, d), const),
                  pl.BlockSpec((d, d), const)],
        out_specs=[pl.BlockSpec((1, m, d), lambda i: (i, 0, 0)),
                   pl.BlockSpec((1, m, d), lambda i: (i, 0, 0))],
        out_shape=[jax.ShapeDtypeStruct((b, m, d), BF16)] * 2,
        compiler_params=_params("arbitrary"),
        name="memkv",
    )(mem, g, wk, wv)


def _mix_kernel(x_ref, g_ref, wa_ref, wg_ref, wr_ref, wpool_ref, pscale_ref, wgate_ref, bgate_ref,
                gnorm_ref, wout_ref, ltri_ref, o_ref, uprev_ref, state_ref, oacc_ref):
    t = x_ref.shape[1]
    j = pl.program_id(1)

    @pl.when(j == 0)
    def _():
        uprev_ref[...] = jnp.zeros_like(uprev_ref)
        state_ref[...] = jnp.zeros_like(state_ref)

    x = x_ref[0]
    h = _rms(x, g_ref[...]).astype(BF16)
    def project(c0, width):
        return _dot(h, wa_ref[:, c0:c0 + width])

    glr = _dot(h, wg_ref[...])
    u = project(_U0, POOL_W)
    gp = _dot(glr.astype(BF16), wgate_ref[...]) + bgate_ref[...]
    q = project(_Q0, GLA_KEY)
    k = project(_K0, GLA_KEY)
    g = jax.nn.log_sigmoid(gp) / GATE_NORMALIZER
    g_hi = g.astype(BF16)
    g_lo = (g - g_hi.astype(F32)).astype(BF16)
    span = ltri_ref.shape[0]
    gc = jnp.concatenate(
        [_dot(ltri_ref[...], g_hi[r0:r0 + span]) + _dot(ltri_ref[...], g_lo[r0:r0 + span])
         for r0 in range(0, t, span)], axis=0)
    v = project(_V0, GLA_VAL)
    r = _dot(h, wr_ref[...])

    u_ext = jnp.concatenate([uprev_ref[...], u], axis=0)
    uprev_ref[...] = u[t - POOL_TAIL:, :]
    row = lax.broadcasted_iota(I32, (t, POOL_GROUP_W), 0)
    pos = (j * t + row + 1).astype(F32)
    pooled = []
    for gi, w in enumerate(POOL_WINDOWS):
        cols = slice(gi * POOL_GROUP_W, (gi + 1) * POOL_GROUP_W)
        wsum = u_ext[:, cols]
        shift = 1
        while shift < w:
            wsum = wsum + pltpu.roll(wsum, shift, axis=0)
            shift *= 2
        p = wsum[POOL_TAIL:, :] / jnp.minimum(pos, float(w)) - u[:, cols]
        pooled.append(_dot(p.astype(BF16), wpool_ref[gi]))
    pool = jnp.concatenate(pooled, axis=1) * pscale_ref[...]

    qe = q * (GLA_DK ** -0.5) * jnp.exp(gc)
    ke = k * jnp.exp(-gc)

    n_chunks = t // CHUNK
    r2 = lax.broadcasted_iota(I32, (2 * CHUNK, V7X_LANES), 0)
    l2 = lax.broadcasted_iota(I32, (2 * CHUNK, V7X_LANES), 1)
    own_lanes = (r2 < CHUNK) == (l2 < GLA_DK)
    causal2 = (r2 & (CHUNK - 1)) >= (l2 & (CHUNK - 1))
    first_lanes = lax.broadcasted_iota(I32, (CHUNK, V7X_LANES), 1) < CHUNK
    brow = lax.broadcasted_iota(I32, (2 * GLA_DK, 2 * GLA_DV), 0)
    bcol = lax.broadcasted_iota(I32, (2 * GLA_DK, 2 * GLA_DV), 1)
    same_head = (brow < GLA_DK) == (bcol < GLA_DV)
    own_values = (brow < CHUNK) == (bcol < GLA_DV)
    pairs = [(c, p) for c in range(n_chunks) for p in range(GLA_HEADS // 2)]

    def part(c, p):
        return (slice(c * CHUNK, (c + 1) * CHUNK), slice(p * V7X_LANES, (p + 1) * V7X_LANES),
                slice(p * 2 * GLA_DV, (p + 1) * 2 * GLA_DV))

    d_state, decay = {}, {}
    for c, p in pairs:
        rows, lanes, vals = part(c, p)
        g_last = gc[c * CHUNK + CHUNK - 1:c * CHUNK + CHUNK, lanes]
        k2 = (k[rows, lanes] * jnp.exp(g_last - gc[rows, lanes])).astype(BF16)
        d_state[c, p] = jnp.where(same_head, _dot_tn(k2, v[rows, vals].astype(BF16)), 0.0)
        dec = jnp.broadcast_to(jnp.exp(g_last), (V7X_LANES, V7X_LANES)).T
        decay[c, p] = jnp.concatenate([dec, dec], axis=1)
    state_in = {}
    states = [state_ref[0], state_ref[1]]
    for c, p in pairs:
        state_in[c, p] = states[p]
        states[p] = states[p] * decay[c, p] + d_state[c, p]
    state_ref[0] = states[0]
    state_ref[1] = states[1]
    for c, p in pairs:
        rows, lanes, vals = part(c, p)
        qe_p = qe[rows, lanes]
        ke_p = ke[rows, lanes]
        v_f = v[rows, vals]
        q2 = jnp.where(own_lanes, jnp.concatenate([qe_p, qe_p], axis=0), 0.0)
        k2x = jnp.concatenate([ke_p, ke_p], axis=0)
        a2 = jnp.where(causal2, _dot_nt(q2.astype(BF16), k2x.astype(BF16)), 0.0)
        att = jnp.where(first_lanes, a2[:CHUNK], a2[CHUNK:])
        v_bd = jnp.where(own_values, jnp.concatenate([v_f, v_f], axis=0), 0.0).astype(BF16)
        lhs = jnp.concatenate([att, qe_p], axis=1).astype(BF16)
        rhs = jnp.concatenate([v_bd, state_in[c, p].astype(BF16)], axis=0)
        oacc_ref[rows, vals] = _dot(lhs, rhs)

    o = oacc_ref[...]
    gla = []
    for hh in range(GLA_HEADS):
        cols = slice(hh * GLA_DV, (hh + 1) * GLA_DV)
        oh = o[:, cols]
        on = oh * lax.rsqrt(jnp.mean(oh * oh, axis=-1, keepdims=True) + EPS) * gnorm_ref[...]
        gla.append(on * jax.nn.silu(r[:, cols]))
    mix = jnp.concatenate([pool] + gla, axis=1).astype(BF16)
    o_ref[0] = x + _dot(mix, wout_ref[...])


def _mix(x, g, wa, wg, wr, wpool, pscale, wgate, bgate, gnorm, wout):
    b, s, d = x.shape
    t = SEQ_TILE
    assert s % t == 0 and t % (2 * CHUNK) == 0 and t >= POOL_TAIL >= max(POOL_WINDOWS) - 1
    assert 2 * GLA_DK == V7X_LANES and GLA_HEADS % 2 == 0 and CHUNK & (CHUNK - 1) == 0
    ii = jnp.arange(2 * CHUNK)
    ltri = ((ii[:, None] // CHUNK == ii[None, :] // CHUNK)
            & (ii[None, :] <= ii[:, None])).astype(BF16)
    c2 = lambda bi, j: (0, 0)
    c3 = lambda bi, j: (0, 0, 0)
    return pl.pallas_call(
        _mix_kernel,
        grid=(b, s // t),
        in_specs=[pl.BlockSpec((1, t, d), lambda bi, j: (bi, j, 0)),
                  pl.BlockSpec((1, d), c2),
                  pl.BlockSpec((d, _G0), c2),
                  pl.BlockSpec((d, V7X_LANES), c2),
                  pl.BlockSpec((d, GLA_VAL), c2),
                  pl.BlockSpec((len(POOL_WINDOWS), POOL_GROUP_W, POOL_GROUP_W), c3),
                  pl.BlockSpec((1, POOL_W), c2),
                  pl.BlockSpec((V7X_LANES, GLA_KEY), c2),
                  pl.BlockSpec((1, GLA_KEY), c2),
                  pl.BlockSpec((1, GLA_DV), c2),
                  pl.BlockSpec((POOL_W + GLA_VAL, d), c2),
                  pl.BlockSpec((2 * CHUNK, 2 * CHUNK), c2)],
        out_specs=pl.BlockSpec((1, t, d), lambda bi, j: (bi, j, 0)),
        out_shape=jax.ShapeDtypeStruct((b, s, d), F32),
        scratch_shapes=[pltpu.VMEM((POOL_TAIL, POOL_W), F32),
                        pltpu.VMEM((GLA_HEADS // 2, 2 * GLA_DK, 2 * GLA_DV), F32),
                        pltpu.VMEM((t, GLA_VAL), F32)],
        compiler_params=_params("arbitrary", "arbitrary"),
        name="mix",
    )(x, g, wa, wg, wr, wpool, pscale, wgate, bgate, gnorm, wout, ltri)


def _attend_route_kernel(x_ref, g_ref, wq_ref, k_ref, v_ref, wo_ref,
                         gm_ref, wrt_hi_ref, wrt_lo_ref, br_ref, utri_ref,
                         o_ref, h_ref, eid_ref, rank_ref, gate_ref, cnt_ref, carry_ref):
    x = x_ref[0]
    h = _rms(x, g_ref[...]).astype(BF16)
    q = _dot(h, wq_ref[...])
    heads = []
    for hh in range(XATTN_HEADS):
        cols = slice(hh * XATTN_HD, (hh + 1) * XATTN_HD)
        sc = _dot_nt(q[:, cols].astype(BF16), k_ref[0][:, cols])
        e = jnp.exp(sc - jnp.max(sc, axis=-1, keepdims=True))
        p = e / jnp.sum(e, axis=-1, keepdims=True)
        heads.append(_dot(p.astype(BF16), v_ref[0][:, cols]))
    o = jnp.concatenate(heads, axis=1).astype(BF16)
    x2 = x + _dot(o, wo_ref[...])
    o_ref[0] = x2
    first = (pl.program_id(0) == 0) & (pl.program_id(1) == 0)
    _route_tile(x2, first, gm_ref, wrt_hi_ref, wrt_lo_ref, br_ref, utri_ref,
                h_ref, eid_ref, rank_ref, gate_ref, cnt_ref, carry_ref)


def _attend_route(x, g, wq, kmem, vmem, wo, gm, wrt_hi, wrt_lo, br):
    b, s, d = x.shape
    m = kmem.shape[1]
    t = SEQ_TILE
    nt = s // t
    n = b * s
    ii = jnp.arange(t)
    utri = (ii[:, None] < ii[None, :]).astype(BF16)
    c2 = lambda bi, j: (0, 0)
    flat = lambda bi, j: (bi * nt + j, 0)
    flat_lanes = lambda bi, j: (0, bi * nt + j)
    return pl.pallas_call(
        _attend_route_kernel,
        grid=(b, nt),
        in_specs=[pl.BlockSpec((1, t, d), lambda bi, j: (bi, j, 0)),
                  pl.BlockSpec((1, d), c2),
                  pl.BlockSpec((d, d), c2),
                  pl.BlockSpec((1, m, d), lambda bi, j: (bi, 0, 0)),
                  pl.BlockSpec((1, m, d), lambda bi, j: (bi, 0, 0)),
                  pl.BlockSpec((d, d), c2),
                  pl.BlockSpec((1, d), c2),
                  pl.BlockSpec((N_EXPERTS, d), c2),
                  pl.BlockSpec((N_EXPERTS, d), c2),
                  pl.BlockSpec((N_EXPERTS, 1), c2),
                  pl.BlockSpec((t, t), c2)],
        out_specs=[pl.BlockSpec((1, t, d), lambda bi, j: (bi, j, 0)),
                   pl.BlockSpec((t * TOKEN_TILE_ROWS, V7X_LANES), flat),
                   pl.BlockSpec((TOP_K, t), flat_lanes),
                   pl.BlockSpec((TOP_K, t), flat_lanes),
                   pl.BlockSpec((t, V7X_LANES), flat),
                   pl.BlockSpec((N_EXPERTS, V7X_LANES), c2)],
        out_shape=[jax.ShapeDtypeStruct((b, s, d), F32),
                   jax.ShapeDtypeStruct((n * TOKEN_TILE_ROWS, V7X_LANES), F32),
                   jax.ShapeDtypeStruct((TOP_K, n), I32),
                   jax.ShapeDtypeStruct((TOP_K, n), I32),
                   jax.ShapeDtypeStruct((n, V7X_LANES), F32),
                   jax.ShapeDtypeStruct((N_EXPERTS, V7X_LANES), F32)],
        scratch_shapes=[pltpu.VMEM((N_EXPERTS, V7X_LANES), F32)],
        compiler_params=_params("arbitrary", "arbitrary"),
        name="attend_route",
    )(x, g, wq, kmem, vmem, wo, gm, wrt_hi, wrt_lo, br, utri)


def _route_tile(x, first, g_ref, wrt_hi_ref, wrt_lo_ref, br_ref, utri_ref,
                h_ref, eid_ref, rank_ref, gate_ref, cnt_ref, carry_ref):
    t = x.shape[0]

    @pl.when(first)
    def _():
        carry_ref[...] = jnp.zeros_like(carry_ref)

    h = _rms(x, g_ref[...])
    _store_token_tiles(h_ref, (), h)
    h_hi = h.astype(BF16)
    h_lo = (h - h_hi.astype(F32)).astype(BF16)
    logits = (_dot_nt(wrt_hi_ref[...], h_hi) + _dot_nt(wrt_hi_ref[...], h_lo)
              + _dot_nt(wrt_lo_ref[...], h_hi)) + br_ref[...]

    e_iota = lax.broadcasted_iota(I32, (N_EXPERTS, t), 0)
    e_iota_f = e_iota.astype(F32)
    work = logits
    vals, onehots = [], []
    for _ in range(TOP_K):
        m = jnp.max(work, axis=0, keepdims=True)
        idx = jnp.min(jnp.where(work == m, e_iota_f, float(N_EXPERTS)), axis=0, keepdims=True)
        hit = e_iota_f == idx
        vals.append(m)
        onehots.append(hit)
        work = jnp.where(hit, -jnp.inf, work)

    ex = [jnp.exp(vk - vals[0]) for vk in vals]
    den = ex[0] + ex[1] + ex[2] + ex[3]

    member = jnp.zeros((N_EXPERTS, t), F32)
    for hit in onehots:
        member = member + jnp.where(hit, 1.0, 0.0)
    carry = carry_ref[...]
    before = _dot(member.astype(BF16), utri_ref[...]) + carry[:, 0:1]

    row4 = lax.broadcasted_iota(I32, (TOP_K, t), 0)
    grow = lax.broadcasted_iota(I32, (V7X_LANES, t), 0)
    eid = jnp.zeros((TOP_K, t), I32)
    rank = jnp.zeros((TOP_K, t), I32)
    gates_t = jnp.zeros((V7X_LANES, t), F32)
    for kk in range(TOP_K):
        idx_k = jnp.sum(jnp.where(onehots[kk], e_iota_f, 0.0), axis=0, keepdims=True)
        rank_k = jnp.sum(jnp.where(onehots[kk], before, 0.0), axis=0, keepdims=True)
        eid = jnp.where(row4 == kk, idx_k.astype(I32), eid)
        rank = jnp.where(row4 == kk, rank_k.astype(I32), rank)
        gates_t = jnp.where(grow == kk, ex[kk] / den, gates_t)
    eid_ref[...] = eid
    rank_ref[...] = rank
    gate_ref[...] = gates_t.T

    new_carry = carry + jnp.sum(member, axis=1, keepdims=True)
    carry_ref[...] = new_carry
    cnt_ref[...] = new_carry


def _plan_kernel(start_ref, eid_ref, rank_ref, dest_ref):
    e = eid_ref[...]
    dest = rank_ref[...]
    for ee in range(N_EXPERTS):
        dest = dest + jnp.where(e == ee, start_ref[ee], 0)
    dest_ref[...] = dest


def _plan(starts, eid, rank):
    n = eid.shape[1]
    t = min(PLAN_TILE, n)
    blk = pl.BlockSpec((TOP_K, t), lambda i, s: (0, i))
    return pl.pallas_call(
        _plan_kernel,
        grid_spec=pltpu.PrefetchScalarGridSpec(
            num_scalar_prefetch=1, grid=(n // t,), in_specs=[blk, blk], out_specs=blk),
        out_shape=jax.ShapeDtypeStruct((TOP_K, n), I32),
        compiler_params=_params("arbitrary"),
        name="plan",
    )(starts, eid, rank)


def _invert_kernel(dest_ref, inv_ref):
    def body(i, carry):
        inv_ref[dest_ref[i]] = i
        return carry

    lax.fori_loop(0, dest_ref.shape[0], body, 0, unroll=32)


def _invert(dest_flat):
    return pl.pallas_call(
        _invert_kernel,
        grid_spec=pltpu.PrefetchScalarGridSpec(
            num_scalar_prefetch=1, grid=(1,), in_specs=[],
            out_specs=pl.BlockSpec(memory_space=pltpu.SMEM)),
        out_shape=jax.ShapeDtypeStruct(dest_flat.shape, I32),
        compiler_params=_params("arbitrary"),
        name="invert",
    )(dest_flat)


_ITEM_VALID, _ITEM_FIRST, _ITEM_NEW_EXPERT, _ITEM_HAS_NEXT_EXPERT = 1, 2, 4, 8


def _expert_kernel(blk_ref, lo_ref, hi_ref, flag_ref, exp_ref, next_exp_ref, inv_ref,
                   h_ref, wgu_hbm, bgu_ref, wdn_hbm, bdn_ref, out_ref,
                   wgu_bf, wdn_bf, wgu_f32, wdn_f32, xs_buf, y_buf, gsem, ssem, wsem,
                   *, n_tok, n_blocks):
    tr = TOKEN_TILE_ROWS
    bm = xs_buf.shape[1] // tr
    i = pl.program_id(0)
    flags = flag_ref[i]
    blk = blk_ref[i]
    slots = xs_buf.shape[0]
    slot = lax.rem(blk, slots)

    def token_tile(ref, row):
        return ref.at[pl.ds(pl.multiple_of(row * tr, tr), tr), :]

    def gather(b, s):
        return [pltpu.make_async_copy(
            token_tile(h_ref, inv_ref[b * bm + r] & (n_tok - 1)),
            xs_buf.at[s, pl.ds(r * tr, tr), :], gsem.at[s]) for r in range(bm)]

    def scatter(b, s):
        return [pltpu.make_async_copy(
            y_buf.at[s, pl.ds(r * tr, tr), :],
            token_tile(out_ref, inv_ref[b * bm + r]), ssem.at[s]) for r in range(bm)]

    def wait_gather(s):
        pltpu.make_async_copy(h_ref.at[pl.ds(0, bm * tr), :], xs_buf.at[s], gsem.at[s]).wait()

    def wait_scatter(s):
        pltpu.make_async_copy(y_buf.at[s], out_ref.at[pl.ds(0, bm * tr), :], ssem.at[s]).wait()

    def start_all(copies):
        for r, c in enumerate(copies):
            c.start(priority=r % 2)

    @pl.when(i == 0)
    def _():
        for b in range(slots - 1):
            start_all(gather(b, b))

    def weight_copies(e):
        return (pltpu.make_async_copy(wgu_hbm.at[e], wgu_f32, wsem.at[0]),
                pltpu.make_async_copy(wdn_hbm.at[e], wdn_f32, wsem.at[1]))

    @pl.when(i == 0)
    def _():
        for c in weight_copies(exp_ref[0]):
            c.start()

    @pl.when((flags & _ITEM_NEW_EXPERT) != 0)
    def _():
        for c in weight_copies(exp_ref[i]):
            c.wait()
        wgu_bf[...] = wgu_f32[...].astype(BF16)
        wdn_bf[...] = wdn_f32[...].astype(BF16)

        @pl.when((flags & _ITEM_HAS_NEXT_EXPERT) != 0)
        def _():
            for c in weight_copies(next_exp_ref[i]):
                c.start()

    bgu = bgu_ref[exp_ref[i]]
    bdn = bdn_ref[exp_ref[i]]

    dyn_slot = (slot,)
    half = bm // 2

    def load_x(r0, n_rows):
        return _load_token_tiles(xs_buf, dyn_slot, n_rows, r0).astype(BF16)

    def ffn_rows(x, r0, merge):
        n_rows = x.shape[0]
        gu = _dot(x, wgu_bf[...]) + bgu
        gate = jnp.minimum(gu[:, :D_FF], SWIGLU_LIMIT)
        up = jnp.clip(gu[:, D_FF:], -SWIGLU_LIMIT, SWIGLU_LIMIT)
        act = gate * jax.nn.sigmoid(SWIGLU_ALPHA * gate) * (up + 1.0)
        yb = _dot(act.astype(BF16), wdn_bf[...]) + bdn
        row = r0 + lax.broadcasted_iota(I32, yb.shape, 0)
        mine = (row >= lo_ref[i]) & (row < hi_ref[i])
        other = _load_token_tiles(y_buf, dyn_slot, n_rows, r0) if merge else 0.0
        return jnp.where(mine, yb, other)

    def first_item(s, has_prev):
        wait_gather(s)
        x_top = _load_token_tiles(xs_buf, (s,), half, 0).astype(BF16)
        ahead = slots - 1
        start_all(gather(jnp.minimum(blk + ahead, n_blocks - 1), (s + ahead) % slots))
        _store_token_tiles(y_buf, dyn_slot, ffn_rows(x_top, 0, False), 0)
        if has_prev:
            start_all(scatter(blk - 1, (s - 1) % slots))
        _store_token_tiles(y_buf, dyn_slot, ffn_rows(load_x(half, half), half, False), half)

    first = (flags & (_ITEM_VALID | _ITEM_FIRST)) == (_ITEM_VALID | _ITEM_FIRST)
    later = (flags & (_ITEM_VALID | _ITEM_FIRST)) == _ITEM_VALID

    @pl.when(first & (blk == 0))
    def _():
        first_item(0, False)

    @pl.when(first & (blk >= slots))
    def _():
        for s in range(slots):
            @pl.when(slot == s)
            def _():
                wait_scatter(s)

    for s in range(slots):
        @pl.when(first & (slot == s) & (blk > 0))
        def _():
            first_item(s, True)

    for r0 in (0, half):
        @pl.when(later & (lo_ref[i] < r0 + half) & (hi_ref[i] > r0))
        def _():
            _store_token_tiles(y_buf, dyn_slot, ffn_rows(load_x(r0, half), r0, True), r0)

    @pl.when(i == pl.num_programs(0) - 1)
    def _():
        start_all(scatter(n_blocks - 1, (n_blocks - 1) % slots))
        for s in range(slots):
            wait_scatter(s)
        for ahead in range(1, slots):
            wait_gather((n_blocks - 1 + ahead) % slots)


def _expert_items(counts, n_rows):
    bm = EXPERT_ROWS
    n_items = n_rows // bm + N_EXPERTS - 1
    ends = jnp.cumsum(counts)
    starts = ends - counts
    blocks_of = jnp.where(counts > 0, (ends - 1) // bm - starts // bm + 1, 0)
    item_end = jnp.cumsum(blocks_of)
    item_start = item_end - blocks_of
    total = item_end[-1]
    idx = jnp.arange(n_items, dtype=I32)
    valid = idx < total
    idc = jnp.minimum(idx, jnp.maximum(total - 1, 0))
    owner = ((item_start[None, :] <= idc[:, None]) & (idc[:, None] < item_end[None, :])).astype(I32)
    pick = lambda v: jnp.sum(owner * v[None, :], axis=1)
    e = pick(jnp.arange(N_EXPERTS, dtype=I32))
    blk = pick(starts // bm) + idc - pick(item_start)
    lo = jnp.clip(pick(starts) - blk * bm, 0, bm)
    hi = jnp.clip(pick(ends) - blk * bm, 0, bm)
    prev_blk = jnp.concatenate([jnp.full((1,), -1, I32), blk[:-1]])
    prev_e = jnp.concatenate([jnp.full((1,), -1, I32), e[:-1]])
    ids = jnp.arange(N_EXPERTS, dtype=I32)
    owners = jnp.where(counts > 0, ids, N_EXPERTS)
    next_of = jnp.min(jnp.where(ids[None, :] > ids[:, None], owners[None, :], N_EXPERTS), axis=1)
    next_e = pick(next_of)
    new_expert = e != prev_e
    flags = jnp.where(valid,
                      _ITEM_VALID
                      + jnp.where(blk != prev_blk, _ITEM_FIRST, 0)
                      + jnp.where(new_expert, _ITEM_NEW_EXPERT, 0)
                      + jnp.where(new_expert & (next_e < N_EXPERTS), _ITEM_HAS_NEXT_EXPERT, 0), 0)
    as_i32 = lambda v: v.astype(I32)
    return (as_i32(blk), as_i32(lo), as_i32(hi), as_i32(flags), as_i32(e),
            as_i32(jnp.minimum(next_e, N_EXPERTS - 1)))


def _experts(items, inv, h, wgu, bgu, wdn, bdn):
    tr, d = TOKEN_TILE_ROWS, D_MODEL
    n = h.shape[0] // tr
    bm = EXPERT_ROWS
    n_rows = inv.shape[0]
    n_blocks = n_rows // bm
    assert n_rows % bm == 0 and n_blocks >= EXPERT_SLOTS and n & (n - 1) == 0
    blk, lo, hi, flags, e, next_e = items
    whole = lambda i, *prefetch: (0, 0, 0)
    return pl.pallas_call(
        functools.partial(_expert_kernel, n_tok=n, n_blocks=n_blocks),
        grid_spec=pltpu.PrefetchScalarGridSpec(
            num_scalar_prefetch=7, grid=(blk.shape[0],),
            in_specs=[pl.BlockSpec(memory_space=pl.ANY),
                      pl.BlockSpec(memory_space=pl.ANY),
                      pl.BlockSpec((N_EXPERTS, 1, 2 * D_FF), whole),
                      pl.BlockSpec(memory_space=pl.ANY),
                      pl.BlockSpec((N_EXPERTS, 1, d), whole)],
            out_specs=pl.BlockSpec(memory_space=pl.ANY),
            scratch_shapes=[pltpu.VMEM((d, 2 * D_FF), BF16),
                            pltpu.VMEM((D_FF, d), BF16),
                            pltpu.VMEM((d, 2 * D_FF), F32),
                            pltpu.VMEM((D_FF, d), F32),
                            pltpu.VMEM((EXPERT_SLOTS, bm * tr, V7X_LANES), F32),
                            pltpu.VMEM((EXPERT_SLOTS, bm * tr, V7X_LANES), F32),
                            pltpu.SemaphoreType.DMA((EXPERT_SLOTS,)),
                            pltpu.SemaphoreType.DMA((EXPERT_SLOTS,)),
                            pltpu.SemaphoreType.DMA((2,))]),
        out_shape=jax.ShapeDtypeStruct((n_rows * tr, V7X_LANES), F32),
        compiler_params=_params("arbitrary"),
        name="experts",
    )(blk, lo, hi, flags, e, next_e, inv, h, wgu, bgu, wdn, bdn)


def _combine_kernel(x_ref, gate_ref, g_ref, y_ref, o_ref):
    gates = gate_ref[...]
    moe = jnp.zeros(x_ref.shape, F32)
    for kk in range(TOP_K):
        moe = moe + _load_token_tiles(y_ref, (kk,), x_ref.shape[0]) * gates[:, kk:kk + 1]
    o_ref[...] = _rms(x_ref[...] + moe, g_ref[...])


def _combine(x2d, gates, g, y):
    n, d = x2d.shape
    t = min(COMBINE_TILE, n)
    return pl.pallas_call(
        _combine_kernel,
        grid=(n // t,),
        in_specs=[pl.BlockSpec((t, d), lambda i: (i, 0)),
                  pl.BlockSpec((t, V7X_LANES), lambda i: (i, 0)),
                  pl.BlockSpec((1, d), lambda i: (0, 0)),
                  pl.BlockSpec((TOP_K, t * TOKEN_TILE_ROWS, V7X_LANES), lambda i: (0, i, 0))],
        out_specs=pl.BlockSpec((t, d), lambda i: (i, 0)),
        out_shape=jax.ShapeDtypeStruct((n, d), F32),
        compiler_params=_params("arbitrary"),
        name="combine",
    )(x2d, gates, g, y.reshape(TOP_K, n * TOKEN_TILE_ROWS, V7X_LANES))


def _layer(x, mem, norm_mix_g, w_in, w_pool, pool_scale, w_gate_up, b_gate_up, gla_norm_g, w_out,
           norm_xattn_g, norm_mem_g, w_xq, w_xk, w_xv, w_xo,
           norm_moe_g, w_router, b_router, w_gu, b_gu, w_dn, b_dn, out_g):
    b, s, d = x.shape
    n = b * s
    row = lambda a: a.reshape(1, -1)

    pad = V7X_LANES - GATE_RANK
    wa = w_in[:, :_G0].astype(BF16)
    wg = jnp.pad(w_in[:, _G0:_R0], ((0, 0), (0, pad))).astype(BF16)
    wr = w_in[:, _R0:].astype(BF16)
    wgate = jnp.pad(w_gate_up, ((0, pad), (0, 0))).astype(BF16)
    wrt = w_router.T
    wrt_hi = wrt.astype(BF16)
    wrt_lo = (wrt - wrt_hi.astype(F32)).astype(BF16)

    kmem, vmem = _memkv(mem, row(norm_mem_g), w_xk.astype(BF16), w_xv.astype(BF16))
    x1 = _mix(x, row(norm_mix_g), wa, wg, wr, w_pool.astype(BF16), row(pool_scale), wgate,
              row(b_gate_up), row(gla_norm_g), w_out.astype(BF16))
    x2, h, eid, rank, gates, cnt = _attend_route(
        x1, row(norm_xattn_g), w_xq.astype(BF16), kmem, vmem, w_xo.astype(BF16),
        row(norm_moe_g), wrt_hi, wrt_lo, b_router.reshape(-1, 1))
    x2 = x2.reshape(n, d)
    counts = cnt[:, 0].astype(I32)
    starts = (jnp.cumsum(counts) - counts).astype(I32)
    dest = _plan(starts, eid, rank).reshape(TOP_K * n)
    items = _expert_items(counts, TOP_K * n)
    y = _experts(items, _invert(dest), h, w_gu, b_gu.reshape(N_EXPERTS, 1, -1),
                 w_dn, b_dn.reshape(N_EXPERTS, 1, -1))
    out = _combine(x2, gates, row(out_g), y)
    return out.reshape(b, s, d)


def kernel(x, mem, norm_mix_g, w_in, w_pool, pool_scale, w_gate_up, b_gate_up, gla_norm_g, w_out,
           norm_xattn_g, norm_mem_g, w_xq, w_xk, w_xv, w_xo, norm_moe_g, w_router, b_router,
           w_gu, b_gu, w_dn, b_dn, norm_final_g):
    depth = norm_mix_g.shape[0]
    assert depth == 1, "the final rmsnorm is fused into the (single) layer's combine stage"
    return _layer(x, mem, norm_mix_g[0], w_in[0], w_pool[0], pool_scale[0], w_gate_up[0],
                  b_gate_up[0], gla_norm_g[0], w_out[0], norm_xattn_g[0], norm_mem_g[0],
                  w_xq[0], w_xk[0], w_xv[0], w_xo[0], norm_moe_g[0], w_router[0], b_router[0],
                  w_gu[0], b_gu[0], w_dn[0], b_dn[0], norm_final_g)
```

```python
import functools

import jax
import jax.numpy as jnp
from jax import lax
from jax.experimental import pallas as pl
from jax.experimental.pallas import tpu as pltpu

F32 = jnp.float32
BF16 = jnp.bfloat16
I32 = jnp.int32

V7X_LANES = 128
V7X_VMEM_LIMIT_BYTES = 56 * 1024 * 1024

D_MODEL = 1024
POOL_WINDOWS = (2, 4, 8, 16)
POOL_GROUP_W = 128
POOL_W = len(POOL_WINDOWS) * POOL_GROUP_W
GLA_HEADS = 4
GLA_DK = 64
GLA_DV = 128
GLA_KEY = GLA_HEADS * GLA_DK
GLA_VAL = GLA_HEADS * GLA_DV
GATE_RANK = 16
GATE_NORMALIZER = 16.0
CHUNK = 64
XATTN_HEADS = 4
XATTN_HD = D_MODEL // XATTN_HEADS
N_EXPERTS = 32
TOP_K = 4
D_FF = D_MODEL
SWIGLU_LIMIT = 7.0
SWIGLU_ALPHA = 1.702
EPS = 1e-6

_U0 = 0
_Q0 = _U0 + POOL_W
_K0 = _Q0 + GLA_KEY
_V0 = _K0 + GLA_KEY
_G0 = _V0 + GLA_VAL
_R0 = _G0 + GATE_RANK

SEQ_TILE = 1024
ATTEND_TILE = 1024
POOL_TAIL = 16
PLAN_TILE = 2048
EXPERT_ROWS = 256
EXPERT_SLOTS = 3
COMBINE_TILE = 512


def _rms(x, g):
    ms = jnp.mean(x * x, axis=-1, keepdims=True)
    return x * lax.rsqrt(ms + EPS) * g


def _dot(a, b):
    return jnp.dot(a, b, preferred_element_type=F32)


def _dot_nt(a, b):
    return lax.dot_general(a, b, (((1,), (1,)), ((), ())), preferred_element_type=F32)


def _dot_tn(a, b):
    return lax.dot_general(a, b, (((0,), (0,)), ((), ())), preferred_element_type=F32)


TOKEN_TILE_ROWS = D_MODEL // V7X_LANES


def _load_token_tiles(ref, lead, n_rows, first_row=0):
    start = first_row * TOKEN_TILE_ROWS
    chunks = [ref[lead + (pl.ds(start + c, n_rows, stride=TOKEN_TILE_ROWS), slice(None))]
              for c in range(TOKEN_TILE_ROWS)]
    return jnp.concatenate(chunks, axis=1)


def _store_token_tiles(ref, lead, val, first_row=0):
    start = first_row * TOKEN_TILE_ROWS
    for c in range(TOKEN_TILE_ROWS):
        ref[lead + (pl.ds(start + c, val.shape[0], stride=TOKEN_TILE_ROWS), slice(None))] = (
            val[:, c * V7X_LANES:(c + 1) * V7X_LANES])


def _params(*semantics):
    return pltpu.CompilerParams(dimension_semantics=semantics,
                                vmem_limit_bytes=V7X_VMEM_LIMIT_BYTES)


def _memkv_kernel(mem_ref, g_ref, wk_ref, wv_ref, k_ref, v_ref):
    hm = _rms(mem_ref[0], g_ref[...]).astype(BF16)
    k = _dot(hm, wk_ref[...])
    v = _dot(hm, wv_ref[...])
    k_ref[0] = (k * (XATTN_HD ** -0.5)).astype(BF16)
    v_ref[0] = v.astype(BF16)


def _memkv(mem, g, wk, wv):
    b, m, d = mem.shape
    const = lambda i: (0, 0)
    return pl.pallas_call(
        _memkv_kernel,
        grid=(b,),
        in_specs=[pl.BlockSpec((1, m, d), lambda i: (i, 0, 0)),
                  pl.BlockSpec((1, d), const),
                  pl.BlockSpec((d, d), const),
                  pl.BlockSpec((d, d), const)],
        out_specs=[pl.BlockSpec((1, m, d), lambda i: (i, 0, 0)),
                   pl.BlockSpec((1, m, d), lambda i: (i, 0, 0))],
        out_shape=[jax.ShapeDtypeStruct((b, m, d), BF16)] * 2,
        compiler_params=_params("arbitrary"),
        name="memkv",
    )(mem, g, wk, wv)


def _mix_kernel(x_ref, g_ref, wa_ref, wg_ref, wr_ref, wpool_ref, pscale_ref, wgate_ref, bgate_ref,
                gnorm_ref, wout_ref, ltri_ref, o_ref, uprev_ref, state_ref, oacc_ref):
    t = x_ref.shape[1]
    j = pl.program_id(1)

    @pl.when(j == 0)
    def _():
        uprev_ref[...] = jnp.zeros_like(uprev_ref)
        state_ref[...] = jnp.zeros_like(state_ref)

    x = x_ref[0]
    h = _rms(x, g_ref[...]).astype(BF16)
    def project(c0, width):
        return _dot(h, wa_ref[:, c0:c0 + width])

    glr = _dot(h, wg_ref[...])
    u = project(_U0, POOL_W)
    gp = _dot(glr.astype(BF16), wgate_ref[...]) + bgate_ref[...]
    q = project(_Q0, GLA_KEY)
    k = project(_K0, GLA_KEY)
    g = jax.nn.log_sigmoid(gp) / GATE_NORMALIZER
    g_hi = g.astype(BF16)
    g_lo = (g - g_hi.astype(F32)).astype(BF16)
    span = ltri_ref.shape[0]
    gc = jnp.concatenate(
        [_dot(ltri_ref[...], g_hi[r0:r0 + span]) + _dot(ltri_ref[...], g_lo[r0:r0 + span])
         for r0 in range(0, t, span)], axis=0)
    v = project(_V0, GLA_VAL)
    r = _dot(h, wr_ref[...])

    u_ext = jnp.concatenate([uprev_ref[...], u], axis=0)
    uprev_ref[...] = u[t - POOL_TAIL:, :]
    row = lax.broadcasted_iota(I32, (t, POOL_GROUP_W), 0)
    pos = (j * t + row + 1).astype(F32)
    pooled = []
    for gi, w in enumerate(POOL_WINDOWS):
        cols = slice(gi * POOL_GROUP_W, (gi + 1) * POOL_GROUP_W)
        wsum = u_ext[:, cols]
        shift = 1
        while shift < w:
            wsum = wsum + pltpu.roll(wsum, shift, axis=0)
            shift *= 2
        p = wsum[POOL_TAIL:, :] / jnp.minimum(pos, float(w)) - u[:, cols]
        pooled.append(_dot(p.astype(BF16), wpool_ref[gi]))
    pool = jnp.concatenate(pooled, axis=1) * pscale_ref[...]

    qe = q * (GLA_DK ** -0.5) * jnp.exp(gc)
    ke = k * jnp.exp(-gc)

    n_chunks = t // CHUNK
    r2 = lax.broadcasted_iota(I32, (2 * CHUNK, V7X_LANES), 0)
    l2 = lax.broadcasted_iota(I32, (2 * CHUNK, V7X_LANES), 1)
    own_lanes = (r2 < CHUNK) == (l2 < GLA_DK)
    causal2 = (r2 & (CHUNK - 1)) >= (l2 & (CHUNK - 1))
    first_lanes = lax.broadcasted_iota(I32, (CHUNK, V7X_LANES), 1) < CHUNK
    brow = lax.broadcasted_iota(I32, (2 * GLA_DK, 2 * GLA_DV), 0)
    bcol = lax.broadcasted_iota(I32, (2 * GLA_DK, 2 * GLA_DV), 1)
    same_head = (brow < GLA_DK) == (bcol < GLA_DV)
    own_values = (brow < CHUNK) == (bcol < GLA_DV)
    pairs = [(c, p) for c in range(n_chunks) for p in range(GLA_HEADS // 2)]

    def part(c, p):
        return (slice(c * CHUNK, (c + 1) * CHUNK), slice(p * V7X_LANES, (p + 1) * V7X_LANES),
                slice(p * 2 * GLA_DV, (p + 1) * 2 * GLA_DV))

    d_state, decay = {}, {}
    for c, p in pairs:
        rows, lanes, vals = part(c, p)
        g_last = gc[c * CHUNK + CHUNK - 1:c * CHUNK + CHUNK, lanes]
        k2 = (k[rows, lanes] * jnp.exp(g_last - gc[rows, lanes])).astype(BF16)
        d_state[c, p] = jnp.where(same_head, _dot_tn(k2, v[rows, vals].astype(BF16)), 0.0)
        dec = jnp.broadcast_to(jnp.exp(g_last), (V7X_LANES, V7X_LANES)).T
        decay[c, p] = jnp.concatenate([dec, dec], axis=1)
    state_in = {}
    states = [state_ref[0], state_ref[1]]
    for c, p in pairs:
        state_in[c, p] = states[p]
        states[p] = states[p] * decay[c, p] + d_state[c, p]
    state_ref[0] = states[0]
    state_ref[1] = states[1]
    for c, p in pairs:
        rows, lanes, vals = part(c, p)
        qe_p = qe[rows, lanes]
        ke_p = ke[rows, lanes]
        v_f = v[rows, vals]
        q2 = jnp.where(own_lanes, jnp.concatenate([qe_p, qe_p], axis=0), 0.0)
        k2x = jnp.concatenate([ke_p, ke_p], axis=0)
        a2 = jnp.where(causal2, _dot_nt(q2.astype(BF16), k2x.astype(BF16)), 0.0)
        att = jnp.where(first_lanes, a2[:CHUNK], a2[CHUNK:])
        v_bd = jnp.where(own_values, jnp.concatenate([v_f, v_f], axis=0), 0.0).astype(BF16)
        lhs = jnp.concatenate([att, qe_p], axis=1).astype(BF16)
        rhs = jnp.concatenate([v_bd, state_in[c, p].astype(BF16)], axis=0)
        oacc_ref[rows, vals] = _dot(lhs, rhs)

    o = oacc_ref[...]
    gla = []
    for hh in range(GLA_HEADS):
        cols = slice(hh * GLA_DV, (hh + 1) * GLA_DV)
        oh = o[:, cols]
        on = oh * lax.rsqrt(jnp.mean(oh * oh, axis=-1, keepdims=True) + EPS) * gnorm_ref[...]
        gla.append(on * jax.nn.silu(r[:, cols]))
    mix = jnp.concatenate([pool] + gla, axis=1).astype(BF16)
    o_ref[0] = x + _dot(mix, wout_ref[...])


def _mix(x, g, wa, wg, wr, wpool, pscale, wgate, bgate, gnorm, wout):
    b, s, d = x.shape
    t = SEQ_TILE
    assert s % t == 0 and t % (2 * CHUNK) == 0 and t >= POOL_TAIL >= max(POOL_WINDOWS) - 1
    assert 2 * GLA_DK == V7X_LANES and GLA_HEADS % 2 == 0 and CHUNK & (CHUNK - 1) == 0
    ii = jnp.arange(2 * CHUNK)
    ltri = ((ii[:, None] // CHUNK == ii[None, :] // CHUNK)
            & (ii[None, :] <= ii[:, None])).astype(BF16)
    c2 = lambda bi, j: (0, 0)
    c3 = lambda bi, j: (0, 0, 0)
    return pl.pallas_call(
        _mix_kernel,
        grid=(b, s // t),
        in_specs=[pl.BlockSpec((1, t, d), lambda bi, j: (bi, j, 0)),
                  pl.BlockSpec((1, d), c2),
                  pl.BlockSpec((d, _G0), c2),
                  pl.BlockSpec((d, V7X_LANES), c2),
                  pl.BlockSpec((d, GLA_VAL), c2),
                  pl.BlockSpec((len(POOL_WINDOWS), POOL_GROUP_W, POOL_GROUP_W), c3),
                  pl.BlockSpec((1, POOL_W), c2),
                  pl.BlockSpec((V7X_LANES, GLA_KEY), c2),
                  pl.BlockSpec((1, GLA_KEY), c2),
                  pl.BlockSpec((1, GLA_DV), c2),
                  pl.BlockSpec((POOL_W + GLA_VAL, d), c2),
                  pl.BlockSpec((2 * CHUNK, 2 * CHUNK), c2)],
        out_specs=pl.BlockSpec((1, t, d), lambda bi, j: (bi, j, 0)),
        out_shape=jax.ShapeDtypeStruct((b, s, d), F32),
        scratch_shapes=[pltpu.VMEM((POOL_TAIL, POOL_W), F32),
                        pltpu.VMEM((GLA_HEADS // 2, 2 * GLA_DK, 2 * GLA_DV), F32),
                        pltpu.VMEM((t, GLA_VAL), F32)],
        compiler_params=_params("arbitrary", "arbitrary"),
        name="mix",
    )(x, g, wa, wg, wr, wpool, pscale, wgate, bgate, gnorm, wout, ltri)


def _attend_route_kernel(x_ref, g_ref, wq_ref, k_ref, v_ref, wo_ref,
                         gm_ref, wrt_hi_ref, wrt_lo_ref, br_ref, utri_ref,
                         o_ref, h_ref, eid_ref, rank_ref, gate_ref, cnt_ref, carry_ref):
    x = x_ref[0]
    h = _rms(x, g_ref[...]).astype(BF16)
    q = _dot(h, wq_ref[...])
    heads = []
    for hh in range(XATTN_HEADS):
        cols = slice(hh * XATTN_HD, (hh + 1) * XATTN_HD)
        sc = _dot_nt(q[:, cols].astype(BF16), k_ref[0][:, cols])
        e = jnp.exp(sc - jnp.max(sc, axis=-1, keepdims=True))
        p = e / jnp.sum(e, axis=-1, keepdims=True)
        heads.append(_dot(p.astype(BF16), v_ref[0][:, cols]))
    o = jnp.concatenate(heads, axis=1).astype(BF16)
    x2 = x + _dot(o, wo_ref[...])
    o_ref[0] = x2
    first = (pl.program_id(0) == 0) & (pl.program_id(1) == 0)
    _route_tile(x2, first, gm_ref, wrt_hi_ref, wrt_lo_ref, br_ref, utri_ref,
                h_ref, eid_ref, rank_ref, gate_ref, cnt_ref, carry_ref)


def _attend_route(x, g, wq, kmem, vmem, wo, gm, wrt_hi, wrt_lo, br):
    b, s, d = x.shape
    m = kmem.shape[1]
    t = ATTEND_TILE
    nt = s // t
    n = b * s
    ii = jnp.arange(t)
    utri = (ii[:, None] < ii[None, :]).astype(BF16)
    c2 = lambda bi, j: (0, 0)
    flat = lambda bi, j: (bi * nt + j, 0)
    flat_lanes = lambda bi, j: (0, bi * nt + j)
    return pl.pallas_call(
        _attend_route_kernel,
        grid=(b, nt),
        in_specs=[pl.BlockSpec((1, t, d), lambda bi, j: (bi, j, 0)),
                  pl.BlockSpec((1, d), c2),
                  pl.BlockSpec((d, d), c2),
                  pl.BlockSpec((1, m, d), lambda bi, j: (bi, 0, 0)),
                  pl.BlockSpec((1, m, d), lambda bi, j: (bi, 0, 0)),
                  pl.BlockSpec((d, d), c2),
                  pl.BlockSpec((1, d), c2),
                  pl.BlockSpec((N_EXPERTS, d), c2),
                  pl.BlockSpec((N_EXPERTS, d), c2),
                  pl.BlockSpec((N_EXPERTS, 1), c2),
                  pl.BlockSpec((t, t), c2)],
        out_specs=[pl.BlockSpec((1, t, d), lambda bi, j: (bi, j, 0)),
                   pl.BlockSpec((t * TOKEN_TILE_ROWS, V7X_LANES), flat),
                   pl.BlockSpec((TOP_K, t), flat_lanes),
                   pl.BlockSpec((TOP_K, t), flat_lanes),
                   pl.BlockSpec((t, V7X_LANES), flat),
                   pl.BlockSpec((N_EXPERTS, V7X_LANES), c2)],
        out_shape=[jax.ShapeDtypeStruct((b, s, d), F32),
                   jax.ShapeDtypeStruct((n * TOKEN_TILE_ROWS, V7X_LANES), F32),
                   jax.ShapeDtypeStruct((TOP_K, n), I32),
                   jax.ShapeDtypeStruct((TOP_K, n), I32),
                   jax.ShapeDtypeStruct((n, V7X_LANES), F32),
                   jax.ShapeDtypeStruct((N_EXPERTS, V7X_LANES), F32)],
        scratch_shapes=[pltpu.VMEM((N_EXPERTS, V7X_LANES), F32)],
        compiler_params=_params("arbitrary", "arbitrary"),
        name="attend_route",
    )(x, g, wq, kmem, vmem, wo, gm, wrt_hi, wrt_lo, br, utri)


def _route_tile(x, first, g_ref, wrt_hi_ref, wrt_lo_ref, br_ref, utri_ref,
                h_ref, eid_ref, rank_ref, gate_ref, cnt_ref, carry_ref):
    t = x.shape[0]

    @pl.when(first)
    def _():
        carry_ref[...] = jnp.zeros_like(carry_ref)

    h = _rms(x, g_ref[...])
    _store_token_tiles(h_ref, (), h)
    h_hi = h.astype(BF16)
    h_lo = (h - h_hi.astype(F32)).astype(BF16)
    logits = (_dot_nt(wrt_hi_ref[...], h_hi) + _dot_nt(wrt_hi_ref[...], h_lo)
              + _dot_nt(wrt_lo_ref[...], h_hi)) + br_ref[...]

    e_iota = lax.broadcasted_iota(I32, (N_EXPERTS, t), 0)
    e_iota_f = e_iota.astype(F32)
    work = logits
    vals, onehots = [], []
    for _ in range(TOP_K):
        m = jnp.max(work, axis=0, keepdims=True)
        idx = jnp.min(jnp.where(work == m, e_iota_f, float(N_EXPERTS)), axis=0, keepdims=True)
        hit = e_iota_f == idx
        vals.append(m)
        onehots.append(hit)
        work = jnp.where(hit, -jnp.inf, work)

    ex = [jnp.exp(vk - vals[0]) for vk in vals]
    den = ex[0] + ex[1] + ex[2] + ex[3]

    member = jnp.zeros((N_EXPERTS, t), F32)
    for hit in onehots:
        member = member + jnp.where(hit, 1.0, 0.0)
    carry = carry_ref[...]
    before = _dot(member.astype(BF16), utri_ref[...]) + carry[:, 0:1]

    row4 = lax.broadcasted_iota(I32, (TOP_K, t), 0)
    grow = lax.broadcasted_iota(I32, (V7X_LANES, t), 0)
    eid = jnp.zeros((TOP_K, t), I32)
    rank = jnp.zeros((TOP_K, t), I32)
    gates_t = jnp.zeros((V7X_LANES, t), F32)
    for kk in range(TOP_K):
        idx_k = jnp.sum(jnp.where(onehots[kk], e_iota_f, 0.0), axis=0, keepdims=True)
        rank_k = jnp.sum(jnp.where(onehots[kk], before, 0.0), axis=0, keepdims=True)
        eid = jnp.where(row4 == kk, idx_k.astype(I32), eid)
        rank = jnp.where(row4 == kk, rank_k.astype(I32), rank)
        gates_t = jnp.where(grow == kk, ex[kk] / den, gates_t)
    eid_ref[...] = eid
    rank_ref[...] = rank
    gate_ref[...] = gates_t.T

    new_carry = carry + jnp.sum(member, axis=1, keepdims=True)
    carry_ref[...] = new_carry
    cnt_ref[...] = new_carry


def _plan_kernel(start_ref, eid_ref, rank_ref, dest_ref):
    e = eid_ref[...]
    dest = rank_ref[...]
    for ee in range(N_EXPERTS):
        dest = dest + jnp.where(e == ee, start_ref[ee], 0)
    dest_ref[...] = dest


def _plan(starts, eid, rank):
    n = eid.shape[1]
    t = min(PLAN_TILE, n)
    blk = pl.BlockSpec((TOP_K, t), lambda i, s: (0, i))
    return pl.pallas_call(
        _plan_kernel,
        grid_spec=pltpu.PrefetchScalarGridSpec(
            num_scalar_prefetch=1, grid=(n // t,), in_specs=[blk, blk], out_specs=blk),
        out_shape=jax.ShapeDtypeStruct((TOP_K, n), I32),
        compiler_params=_params("arbitrary"),
        name="plan",
    )(starts, eid, rank)


def _invert_kernel(dest_ref, inv_ref):
    def body(i, carry):
        inv_ref[dest_ref[i]] = i
        return carry

    lax.fori_loop(0, dest_ref.shape[0], body, 0, unroll=32)


def _invert(dest_flat):
    return pl.pallas_call(
        _invert_kernel,
        grid_spec=pltpu.PrefetchScalarGridSpec(
            num_scalar_prefetch=1, grid=(1,), in_specs=[],
            out_specs=pl.BlockSpec(memory_space=pltpu.SMEM)),
        out_shape=jax.ShapeDtypeStruct(dest_flat.shape, I32),
        compiler_params=_params("arbitrary"),
        name="invert",
    )(dest_flat)


_ITEM_VALID, _ITEM_FIRST, _ITEM_NEW_EXPERT, _ITEM_HAS_NEXT_EXPERT = 1, 2, 4, 8


def _expert_kernel(blk_ref, lo_ref, hi_ref, flag_ref, exp_ref, next_exp_ref, inv_ref,
                   h_ref, wgu_hbm, bgu_ref, wdn_hbm, bdn_ref, out_ref,
                   wgu_bf, wdn_bf, wgu_f32, wdn_f32, xs_buf, y_buf, gsem, ssem, wsem,
                   *, n_tok, n_blocks):
    tr = TOKEN_TILE_ROWS
    bm = xs_buf.shape[1] // tr
    i = pl.program_id(0)
    flags = flag_ref[i]
    blk = blk_ref[i]
    slots = xs_buf.shape[0]
    slot = lax.rem(blk, slots)

    def token_tile(ref, row):
        return ref.at[pl.ds(pl.multiple_of(row * tr, tr), tr), :]

    def gather(b, s):
        return [pltpu.make_async_copy(
            token_tile(h_ref, inv_ref[b * bm + r] & (n_tok - 1)),
            xs_buf.at[s, pl.ds(r * tr, tr), :], gsem.at[s]) for r in range(bm)]

    def scatter(b, s):
        return [pltpu.make_async_copy(
            y_buf.at[s, pl.ds(r * tr, tr), :],
            token_tile(out_ref, inv_ref[b * bm + r]), ssem.at[s]) for r in range(bm)]

    def wait_gather(s):
        pltpu.make_async_copy(h_ref.at[pl.ds(0, bm * tr), :], xs_buf.at[s], gsem.at[s]).wait()

    def wait_scatter(s):
        pltpu.make_async_copy(y_buf.at[s], out_ref.at[pl.ds(0, bm * tr), :], ssem.at[s]).wait()

    def start_all(copies):
        for r, c in enumerate(copies):
            c.start(priority=r % 2)

    @pl.when(i == 0)
    def _():
        for b in range(slots - 1):
            start_all(gather(b, b))

    def weight_copies(e):
        return (pltpu.make_async_copy(wgu_hbm.at[e], wgu_f32, wsem.at[0]),
                pltpu.make_async_copy(wdn_hbm.at[e], wdn_f32, wsem.at[1]))

    @pl.when(i == 0)
    def _():
        for c in weight_copies(exp_ref[0]):
            c.start()

    @pl.when((flags & _ITEM_NEW_EXPERT) != 0)
    def _():
        for c in weight_copies(exp_ref[i]):
            c.wait()
        wgu_bf[...] = wgu_f32[...].astype(BF16)
        wdn_bf[...] = wdn_f32[...].astype(BF16)

        @pl.when((flags & _ITEM_HAS_NEXT_EXPERT) != 0)
        def _():
            for c in weight_copies(next_exp_ref[i]):
                c.start()

    bgu = bgu_ref[exp_ref[i]]
    bdn = bdn_ref[exp_ref[i]]

    dyn_slot = (slot,)
    half = bm // 2

    def load_x(r0, n_rows):
        return _load_token_tiles(xs_buf, dyn_slot, n_rows, r0).astype(BF16)

    def ffn_rows(x, r0, merge):
        n_rows = x.shape[0]
        gu = _dot(x, wgu_bf[...]) + bgu
        gate = jnp.minimum(gu[:, :D_FF], SWIGLU_LIMIT)
        up = jnp.clip(gu[:, D_FF:], -SWIGLU_LIMIT, SWIGLU_LIMIT)
        act = gate * jax.nn.sigmoid(SWIGLU_ALPHA * gate) * (up + 1.0)
        yb = _dot(act.astype(BF16), wdn_bf[...]) + bdn
        row = r0 + lax.broadcasted_iota(I32, yb.shape, 0)
        mine = (row >= lo_ref[i]) & (row < hi_ref[i])
        other = _load_token_tiles(y_buf, dyn_slot, n_rows, r0) if merge else 0.0
        return jnp.where(mine, yb, other)

    def first_item(s, has_prev):
        wait_gather(s)
        x_top = _load_token_tiles(xs_buf, (s,), half, 0).astype(BF16)
        ahead = slots - 1
        start_all(gather(jnp.minimum(blk + ahead, n_blocks - 1), (s + ahead) % slots))
        _store_token_tiles(y_buf, dyn_slot, ffn_rows(x_top, 0, False), 0)
        if has_prev:
            start_all(scatter(blk - 1, (s - 1) % slots))
        _store_token_tiles(y_buf, dyn_slot, ffn_rows(load_x(half, half), half, False), half)

    first = (flags & (_ITEM_VALID | _ITEM_FIRST)) == (_ITEM_VALID | _ITEM_FIRST)
    later = (flags & (_ITEM_VALID | _ITEM_FIRST)) == _ITEM_VALID

    @pl.when(first & (blk == 0))
    def _():
        first_item(0, False)

    @pl.when(first & (blk >= slots))
    def _():
        for s in range(slots):
            @pl.when(slot == s)
            def _():
                wait_scatter(s)

    for s in range(slots):
        @pl.when(first & (slot == s) & (blk > 0))
        def _():
            first_item(s, True)

    for r0 in (0, half):
        @pl.when(later & (lo_ref[i] < r0 + half) & (hi_ref[i] > r0))
        def _():
            _store_token_tiles(y_buf, dyn_slot, ffn_rows(load_x(r0, half), r0, True), r0)

    @pl.when(i == pl.num_programs(0) - 1)
    def _():
        start_all(scatter(n_blocks - 1, (n_blocks - 1) % slots))
        for s in range(slots):
            wait_scatter(s)
        for ahead in range(1, slots):
            wait_gather((n_blocks - 1 + ahead) % slots)


def _expert_items(counts, n_rows):
    bm = EXPERT_ROWS
    n_items = n_rows // bm + N_EXPERTS - 1
    ends = jnp.cumsum(counts)
    starts = ends - counts
    blocks_of = jnp.where(counts > 0, (ends - 1) // bm - starts // bm + 1, 0)
    item_end = jnp.cumsum(blocks_of)
    item_start = item_end - blocks_of
    total = item_end[-1]
    idx = jnp.arange(n_items, dtype=I32)
    valid = idx < total
    idc = jnp.minimum(idx, jnp.maximum(total - 1, 0))
    owner = ((item_start[None, :] <= idc[:, None]) & (idc[:, None] < item_end[None, :])).astype(I32)
    pick = lambda v: jnp.sum(owner * v[None, :], axis=1)
    e = pick(jnp.arange(N_EXPERTS, dtype=I32))
    blk = pick(starts // bm) + idc - pick(item_start)
    lo = jnp.clip(pick(starts) - blk * bm, 0, bm)
    hi = jnp.clip(pick(ends) - blk * bm, 0, bm)
    prev_blk = jnp.concatenate([jnp.full((1,), -1, I32), blk[:-1]])
    prev_e = jnp.concatenate([jnp.full((1,), -1, I32), e[:-1]])
    ids = jnp.arange(N_EXPERTS, dtype=I32)
    owners = jnp.where(counts > 0, ids, N_EXPERTS)
    next_of = jnp.min(jnp.where(ids[None, :] > ids[:, None], owners[None, :], N_EXPERTS), axis=1)
    next_e = pick(next_of)
    new_expert = e != prev_e
    flags = jnp.where(valid,
                      _ITEM_VALID
                      + jnp.where(blk != prev_blk, _ITEM_FIRST, 0)
                      + jnp.where(new_expert, _ITEM_NEW_EXPERT, 0)
                      + jnp.where(new_expert & (next_e < N_EXPERTS), _ITEM_HAS_NEXT_EXPERT, 0), 0)
    as_i32 = lambda v: v.astype(I32)
    return (as_i32(blk), as_i32(lo), as_i32(hi), as_i32(flags), as_i32(e),
            as_i32(jnp.minimum(next_e, N_EXPERTS - 1)))


def _experts(items, inv, h, wgu, bgu, wdn, bdn):
    tr, d = TOKEN_TILE_ROWS, D_MODEL
    n = h.shape[0] // tr
    bm = EXPERT_ROWS
    n_rows = inv.shape[0]
    n_blocks = n_rows // bm
    assert n_rows % bm == 0 and n_blocks >= EXPERT_SLOTS and n & (n - 1) == 0
    blk, lo, hi, flags, e, next_e = items
    whole = lambda i, *prefetch: (0, 0, 0)
    return pl.pallas_call(
        functools.partial(_expert_kernel, n_tok=n, n_blocks=n_blocks),
        grid_spec=pltpu.PrefetchScalarGridSpec(
            num_scalar_prefetch=7, grid=(blk.shape[0],),
            in_specs=[pl.BlockSpec(memory_space=pl.ANY),
                      pl.BlockSpec(memory_space=pl.ANY),
                      pl.BlockSpec((N_EXPERTS, 1, 2 * D_FF), whole),
                      pl.BlockSpec(memory_space=pl.ANY),
                      pl.BlockSpec((N_EXPERTS, 1, d), whole)],
            out_specs=pl.BlockSpec(memory_space=pl.ANY),
            scratch_shapes=[pltpu.VMEM((d, 2 * D_FF), BF16),
                            pltpu.VMEM((D_FF, d), BF16),
                            pltpu.VMEM((d, 2 * D_FF), F32),
                            pltpu.VMEM((D_FF, d), F32),
                            pltpu.VMEM((EXPERT_SLOTS, bm * tr, V7X_LANES), F32),
                            pltpu.VMEM((EXPERT_SLOTS, bm * tr, V7X_LANES), F32),
                            pltpu.SemaphoreType.DMA((EXPERT_SLOTS,)),
                            pltpu.SemaphoreType.DMA((EXPERT_SLOTS,)),
                            pltpu.SemaphoreType.DMA((2,))]),
        out_shape=jax.ShapeDtypeStruct((n_rows * tr, V7X_LANES), F32),
        compiler_params=_params("arbitrary"),
        name="experts",
    )(blk, lo, hi, flags, e, next_e, inv, h, wgu, bgu, wdn, bdn)


def _combine_kernel(x_ref, gate_ref, g_ref, y_ref, o_ref):
    gates = gate_ref[...]
    moe = jnp.zeros(x_ref.shape, F32)
    for kk in range(TOP_K):
        moe = moe + _load_token_tiles(y_ref, (kk,), x_ref.shape[0]) * gates[:, kk:kk + 1]
    o_ref[...] = _rms(x_ref[...] + moe, g_ref[...])


def _combine(x2d, gates, g, y):
    n, d = x2d.shape
    t = min(COMBINE_TILE, n)
    return pl.pallas_call(
        _combine_kernel,
        grid=(n // t,),
        in_specs=[pl.BlockSpec((t, d), lambda i: (i, 0)),
                  pl.BlockSpec((t, V7X_LANES), lambda i: (i, 0)),
                  pl.BlockSpec((1, d), lambda i: (0, 0)),
                  pl.BlockSpec((TOP_K, t * TOKEN_TILE_ROWS, V7X_LANES), lambda i: (0, i, 0))],
        out_specs=pl.BlockSpec((t, d), lambda i: (i, 0)),
        out_shape=jax.ShapeDtypeStruct((n, d), F32),
        compiler_params=_params("arbitrary"),
        name="combine",
    )(x2d, gates, g, y.reshape(TOP_K, n * TOKEN_TILE_ROWS, V7X_LANES))


def _layer(x, mem, norm_mix_g, w_in, w_pool, pool_scale, w_gate_up, b_gate_up, gla_norm_g, w_out,
           norm_xattn_g, norm_mem_g, w_xq, w_xk, w_xv, w_xo,
           norm_moe_g, w_router, b_router, w_gu, b_gu, w_dn, b_dn, out_g):
    b, s, d = x.shape
    n = b * s
    row = lambda a: a.reshape(1, -1)

    pad = V7X_LANES - GATE_RANK
    wa = w_in[:, :_G0].astype(BF16)
    wg = jnp.pad(w_in[:, _G0:_R0], ((0, 0), (0, pad))).astype(BF16)
    wr = w_in[:, _R0:].astype(BF16)
    wgate = jnp.pad(w_gate_up, ((0, pad), (0, 0))).astype(BF16)
    wrt = w_router.T
    wrt_hi = wrt.astype(BF16)
    wrt_lo = (wrt - wrt_hi.astype(F32)).astype(BF16)

    kmem, vmem = _memkv(mem, row(norm_mem_g), w_xk.astype(BF16), w_xv.astype(BF16))
    x1 = _mix(x, row(norm_mix_g), wa, wg, wr, w_pool.astype(BF16), row(pool_scale), wgate,
              row(b_gate_up), row(gla_norm_g), w_out.astype(BF16))
    x2, h, eid, rank, gates, cnt = _attend_route(
        x1, row(norm_xattn_g), w_xq.astype(BF16), kmem, vmem, w_xo.astype(BF16),
        row(norm_moe_g), wrt_hi, wrt_lo, b_router.reshape(-1, 1))
    x2 = x2.reshape(n, d)
    counts = cnt[:, 0].astype(I32)
    starts = (jnp.cumsum(counts) - counts).astype(I32)
    dest = _plan(starts, eid, rank).reshape(TOP_K * n)
    items = _expert_items(counts, TOP_K * n)
    y = _experts(items, _invert(dest), h, w_gu, b_gu.reshape(N_EXPERTS, 1, -1),
                 w_dn, b_dn.reshape(N_EXPERTS, 1, -1))
    out = _combine(x2, gates, row(out_g), y)
    return out.reshape(b, s, d)


def kernel(x, mem, norm_mix_g, w_in, w_pool, pool_scale, w_gate_up, b_gate_up, gla_norm_g, w_out,
           norm_xattn_g, norm_mem_g, w_xq, w_xk, w_xv, w_xo, norm_moe_g, w_router, b_router,
           w_gu, b_gu, w_dn, b_dn, norm_final_g):
    depth = norm_mix_g.shape[0]
    assert depth == 1, "the final rmsnorm is fused into the (single) layer's combine stage"
    return _layer(x, mem, norm_mix_g[0], w_in[0], w_pool[0], pool_scale[0], w_gate_up[0],
                  b_gate_up[0], gla_norm_g[0], w_out[0], norm_xattn_g[0], norm_mem_g[0],
                  w_xq[0], w_xk[0], w_xv[0], w_xo[0], norm_moe_g[0], w_router[0], b_router[0],
                  w_gu[0], b_gu[0], w_dn[0], b_dn[0], norm_final_g)
```

```python
import functools

import jax
import jax.numpy as jnp
from jax import lax
from jax.experimental import pallas as pl
from jax.experimental.pallas import tpu as pltpu

F32 = jnp.float32
BF16 = jnp.bfloat16
I32 = jnp.int32

V7X_LANES = 128
V7X_VMEM_LIMIT_BYTES = 56 * 1024 * 1024

D_MODEL = 1024
POOL_WINDOWS = (2, 4, 8, 16)
POOL_GROUP_W = 128
POOL_W = len(POOL_WINDOWS) * POOL_GROUP_W
GLA_HEADS = 4
GLA_DK = 64
GLA_DV = 128
GLA_KEY = GLA_HEADS * GLA_DK
GLA_VAL = GLA_HEADS * GLA_DV
GATE_RANK = 16
GATE_NORMALIZER = 16.0
CHUNK = 64
XATTN_HEADS = 4
XATTN_HD = D_MODEL // XATTN_HEADS
N_EXPERTS = 32
TOP_K = 4
D_FF = D_MODEL
SWIGLU_LIMIT = 7.0
SWIGLU_ALPHA = 1.702
EPS = 1e-6

_U0 = 0
_Q0 = _U0 + POOL_W
_K0 = _Q0 + GLA_KEY
_V0 = _K0 + GLA_KEY
_G0 = _V0 + GLA_VAL
_R0 = _G0 + GATE_RANK

SEQ_TILE = 1024
ATTEND_TILE = 1024
POOL_TAIL = 16
PLAN_TILE = 2048
EXPERT_ROWS = 256
EXPERT_SLOTS = 3
COMBINE_TILE = 512


def _rms(x, g):
    ms = jnp.mean(x * x, axis=-1, keepdims=True)
    return x * lax.rsqrt(ms + EPS) * g


def _dot(a, b):
    return jnp.dot(a, b, preferred_element_type=F32)


def _dot_nt(a, b):
    return lax.dot_general(a, b, (((1,), (1,)), ((), ())), preferred_element_type=F32)


def _dot_tn(a, b):
    return lax.dot_general(a, b, (((0,), (0,)), ((), ())), preferred_element_type=F32)


TOKEN_TILE_ROWS = D_MODEL // V7X_LANES


def _load_token_tiles(ref, lead, n_rows, first_row=0):
    start = first_row * TOKEN_TILE_ROWS
    chunks = [ref[lead + (pl.ds(start + c, n_rows, stride=TOKEN_TILE_ROWS), slice(None))]
              for c in range(TOKEN_TILE_ROWS)]
    return jnp.concatenate(chunks, axis=1)


def _store_token_tiles(ref, lead, val, first_row=0):
    start = first_row * TOKEN_TILE_ROWS
    for c in range(TOKEN_TILE_ROWS):
        ref[lead + (pl.ds(start + c, val.shape[0], stride=TOKEN_TILE_ROWS), slice(None))] = (
            val[:, c * V7X_LANES:(c + 1) * V7X_LANES])


def _params(*semantics):
    return pltpu.CompilerParams(dimension_semantics=semantics,
                                vmem_limit_bytes=V7X_VMEM_LIMIT_BYTES)


def _memkv_kernel(mem_ref, g_ref, wk_ref, wv_ref, k_ref, v_ref):
    hm = _rms(mem_ref[0], g_ref[...]).astype(BF16)
    k = _dot(hm, wk_ref[...])
    v = _dot(hm, wv_ref[...])
    k_ref[0] = (k * (XATTN_HD ** -0.5)).astype(BF16)
    v_ref[0] = v.astype(BF16)


def _memkv(mem, g, wk, wv):
    b, m, d = mem.shape
    const = lambda i: (0, 0)
    return pl.pallas_call(
        _memkv_kernel,
        grid=(b,),
        in_specs=[pl.BlockSpec((1, m, d), lambda i: (i, 0, 0)),
                  pl.BlockSpec((1, d), const),
                  pl.BlockSpec((d, d), const),
                  pl.BlockSpec((d, d), const)],
        out_specs=[pl.BlockSpec((1, m, d), lambda i: (i, 0, 0)),
                   pl.BlockSpec((1, m, d), lambda i: (i, 0, 0))],
        out_shape=[jax.ShapeDtypeStruct((b, m, d), BF16)] * 2,
        compiler_params=_params("arbitrary"),
        name="memkv",
    )(mem, g, wk, wv)


def _mix_kernel(x_ref, g_ref, wa_ref, wg_ref, wr_ref, wpool_ref, pscale_ref, wgate_ref, bgate_ref,
                gnorm_ref, wout_ref, ltri_ref, o_ref, uprev_ref, state_ref, oacc_ref):
    t = x_ref.shape[1]
    j = pl.program_id(1)

    @pl.when(j == 0)
    def _():
        uprev_ref[...] = jnp.zeros_like(uprev_ref)
        state_ref[...] = jnp.zeros_like(state_ref)

    x = x_ref[0]
    h = _rms(x, g_ref[...]).astype(BF16)
    def project(c0, width):
        return _dot(h, wa_ref[:, c0:c0 + width])

    glr = _dot(h, wg_ref[...])
    u = project(_U0, POOL_W)
    gp = _dot(glr.astype(BF16), wgate_ref[...]) + bgate_ref[...]
    q = project(_Q0, GLA_KEY)
    k = project(_K0, GLA_KEY)
    g = jax.nn.log_sigmoid(gp) / GATE_NORMALIZER
    g_hi = g.astype(BF16)
    g_lo = (g - g_hi.astype(F32)).astype(BF16)
    span = ltri_ref.shape[0]
    gc = jnp.concatenate(
        [_dot(ltri_ref[...], g_hi[r0:r0 + span]) + _dot(ltri_ref[...], g_lo[r0:r0 + span])
         for r0 in range(0, t, span)], axis=0)
    v = project(_V0, GLA_VAL)
    r = _dot(h, wr_ref[...])

    u_ext = jnp.concatenate([uprev_ref[...], u], axis=0)
    uprev_ref[...] = u[t - POOL_TAIL:, :]
    row = lax.broadcasted_iota(I32, (t, POOL_GROUP_W), 0)
    pos = (j * t + row + 1).astype(F32)
    pooled = []
    for gi, w in enumerate(POOL_WINDOWS):
        cols = slice(gi * POOL_GROUP_W, (gi + 1) * POOL_GROUP_W)
        wsum = u_ext[:, cols]
        shift = 1
        while shift < w:
            wsum = wsum + pltpu.roll(wsum, shift, axis=0)
            shift *= 2
        p = wsum[POOL_TAIL:, :] / jnp.minimum(pos, float(w)) - u[:, cols]
        pooled.append(_dot(p.astype(BF16), wpool_ref[gi]))
    pool = jnp.concatenate(pooled, axis=1) * pscale_ref[...]

    qe = q * (GLA_DK ** -0.5) * jnp.exp(gc)
    ke = k * jnp.exp(-gc)

    n_chunks = t // CHUNK
    r2 = lax.broadcasted_iota(I32, (2 * CHUNK, V7X_LANES), 0)
    l2 = lax.broadcasted_iota(I32, (2 * CHUNK, V7X_LANES), 1)
    own_lanes = (r2 < CHUNK) == (l2 < GLA_DK)
    causal2 = (r2 & (CHUNK - 1)) >= (l2 & (CHUNK - 1))
    first_lanes = lax.broadcasted_iota(I32, (CHUNK, V7X_LANES), 1) < CHUNK
    brow = lax.broadcasted_iota(I32, (2 * GLA_DK, 2 * GLA_DV), 0)
    bcol = lax.broadcasted_iota(I32, (2 * GLA_DK, 2 * GLA_DV), 1)
    same_head = (brow < GLA_DK) == (bcol < GLA_DV)
    own_values = (brow < CHUNK) == (bcol < GLA_DV)
    pairs = [(c, p) for c in range(n_chunks) for p in range(GLA_HEADS // 2)]

    def part(c, p):
        return (slice(c * CHUNK, (c + 1) * CHUNK), slice(p * V7X_LANES, (p + 1) * V7X_LANES),
                slice(p * 2 * GLA_DV, (p + 1) * 2 * GLA_DV))

    d_state, decay = {}, {}
    for c, p in pairs:
        rows, lanes, vals = part(c, p)
        g_last = gc[c * CHUNK + CHUNK - 1:c * CHUNK + CHUNK, lanes]
        k2 = (k[rows, lanes] * jnp.exp(g_last - gc[rows, lanes])).astype(BF16)
        d_state[c, p] = jnp.where(same_head, _dot_tn(k2, v[rows, vals].astype(BF16)), 0.0)
        dec = jnp.broadcast_to(jnp.exp(g_last), (V7X_LANES, V7X_LANES)).T
        decay[c, p] = jnp.concatenate([dec, dec], axis=1)
    state_in = {}
    states = [state_ref[0], state_ref[1]]
    for c, p in pairs:
        state_in[c, p] = states[p]
        states[p] = states[p] * decay[c, p] + d_state[c, p]
    state_ref[0] = states[0]
    state_ref[1] = states[1]
    for c, p in pairs:
        rows, lanes, vals = part(c, p)
        qe_p = qe[rows, lanes]
        ke_p = ke[rows, lanes]
        v_f = v[rows, vals]
        q2 = jnp.where(own_lanes, jnp.concatenate([qe_p, qe_p], axis=0), 0.0)
        k2x = jnp.concatenate([ke_p, ke_p], axis=0)
        a2 = jnp.where(causal2, _dot_nt(q2.astype(BF16), k2x.astype(BF16)), 0.0)
        att = jnp.where(first_lanes, a2[:CHUNK], a2[CHUNK:])
        v_bd = jnp.where(own_values, jnp.concatenate([v_f, v_f], axis=0), 0.0).astype(BF16)
        lhs = jnp.concatenate([att, qe_p], axis=1).astype(BF16)
        rhs = jnp.concatenate([v_bd, state_in[c, p].astype(BF16)], axis=0)
        oacc_ref[rows, vals] = _dot(lhs, rhs)

    o = oacc_ref[...]
    gla = []
    for hh in range(GLA_HEADS):
        cols = slice(hh * GLA_DV, (hh + 1) * GLA_DV)
        oh = o[:, cols]
        on = oh * lax.rsqrt(jnp.mean(oh * oh, axis=-1, keepdims=True) + EPS) * gnorm_ref[...]
        gla.append(on * jax.nn.silu(r[:, cols]))
    mix = jnp.concatenate([pool] + gla, axis=1).astype(BF16)
    o_ref[0] = x + _dot(mix, wout_ref[...])


def _mix(x, g, wa, wg, wr, wpool, pscale, wgate, bgate, gnorm, wout):
    b, s, d = x.shape
    t = SEQ_TILE
    assert s % t == 0 and t % (2 * CHUNK) == 0 and t >= POOL_TAIL >= max(POOL_WINDOWS) - 1
    assert 2 * GLA_DK == V7X_LANES and GLA_HEADS % 2 == 0 and CHUNK & (CHUNK - 1) == 0
    ii = jnp.arange(2 * CHUNK)
    ltri = ((ii[:, None] // CHUNK == ii[None, :] // CHUNK)
            & (ii[None, :] <= ii[:, None])).astype(BF16)
    c2 = lambda bi, j: (0, 0)
    c3 = lambda bi, j: (0, 0, 0)
    return pl.pallas_call(
        _mix_kernel,
        grid=(b, s // t),
        in_specs=[pl.BlockSpec((1, t, d), lambda bi, j: (bi, j, 0)),
                  pl.BlockSpec((1, d), c2),
                  pl.BlockSpec((d, _G0), c2),
                  pl.BlockSpec((d, V7X_LANES), c2),
                  pl.BlockSpec((d, GLA_VAL), c2),
                  pl.BlockSpec((len(POOL_WINDOWS), POOL_GROUP_W, POOL_GROUP_W), c3),
                  pl.BlockSpec((1, POOL_W), c2),
                  pl.BlockSpec((V7X_LANES, GLA_KEY), c2),
                  pl.BlockSpec((1, GLA_KEY), c2),
                  pl.BlockSpec((1, GLA_DV), c2),
                  pl.BlockSpec((POOL_W + GLA_VAL, d), c2),
                  pl.BlockSpec((2 * CHUNK, 2 * CHUNK), c2)],
        out_specs=pl.BlockSpec((1, t, d), lambda bi, j: (bi, j, 0)),
        out_shape=jax.ShapeDtypeStruct((b, s, d), F32),
        scratch_shapes=[pltpu.VMEM((POOL_TAIL, POOL_W), F32),
                        pltpu.VMEM((GLA_HEADS // 2, 2 * GLA_DK, 2 * GLA_DV), F32),
                        pltpu.VMEM((t, GLA_VAL), F32)],
        compiler_params=_params("arbitrary", "arbitrary"),
        name="mix",
    )(x, g, wa, wg, wr, wpool, pscale, wgate, bgate, gnorm, wout, ltri)


def _attend_route_kernel(x_ref, g_ref, wq_ref, k_ref, v_ref, wo_ref,
                         gm_ref, wrt_hi_ref, wrt_lo_ref, br_ref, utri_ref,
                         o_ref, h_ref, eid_ref, rank_ref, gate_ref, cnt_ref, carry_ref):
    x = x_ref[0]
    h = _rms(x, g_ref[...]).astype(BF16)
    q = _dot(h, wq_ref[...])
    heads = []
    for hh in range(XATTN_HEADS):
        cols = slice(hh * XATTN_HD, (hh + 1) * XATTN_HD)
        sc = _dot_nt(q[:, cols].astype(BF16), k_ref[0][:, cols])
        e = jnp.exp(sc - jnp.max(sc, axis=-1, keepdims=True))
        p = e / jnp.sum(e, axis=-1, keepdims=True)
        heads.append(_dot(p.astype(BF16), v_ref[0][:, cols]))
    o = jnp.concatenate(heads, axis=1).astype(BF16)
    x2 = x + _dot(o, wo_ref[...])
    o_ref[0] = x2
    first = (pl.program_id(0) == 0) & (pl.program_id(1) == 0)
    _route_tile(x2, first, gm_ref, wrt_hi_ref, wrt_lo_ref, br_ref, utri_ref,
                h_ref, eid_ref, rank_ref, gate_ref, cnt_ref, carry_ref)


def _attend_route(x, g, wq, kmem, vmem, wo, gm, wrt_hi, wrt_lo, br):
    b, s, d = x.shape
    m = kmem.shape[1]
    t = ATTEND_TILE
    nt = s // t
    n = b * s
    ii = jnp.arange(t)
    utri = (ii[:, None] < ii[None, :]).astype(BF16)
    c2 = lambda bi, j: (0, 0)
    flat = lambda bi, j: (bi * nt + j, 0)
    flat_lanes = lambda bi, j: (0, bi * nt + j)
    return pl.pallas_call(
        _attend_route_kernel,
        grid=(b, nt),
        in_specs=[pl.BlockSpec((1, t, d), lambda bi, j: (bi, j, 0)),
                  pl.BlockSpec((1, d), c2),
                  pl.BlockSpec((d, d), c2),
                  pl.BlockSpec((1, m, d), lambda bi, j: (bi, 0, 0)),
                  pl.BlockSpec((1, m, d), lambda bi, j: (bi, 0, 0)),
                  pl.BlockSpec((d, d), c2),
                  pl.BlockSpec((1, d), c2),
                  pl.BlockSpec((N_EXPERTS, d), c2),
                  pl.BlockSpec((N_EXPERTS, d), c2),
                  pl.BlockSpec((N_EXPERTS, 1), c2),
                  pl.BlockSpec((t, t), c2)],
        out_specs=[pl.BlockSpec((1, t, d), lambda bi, j: (bi, j, 0)),
                   pl.BlockSpec((t * TOKEN_TILE_ROWS, V7X_LANES), flat),
                   pl.BlockSpec((TOP_K, t), flat_lanes),
                   pl.BlockSpec((TOP_K, t), flat_lanes),
                   pl.BlockSpec((t, V7X_LANES), flat),
                   pl.BlockSpec((N_EXPERTS, V7X_LANES), c2)],
        out_shape=[jax.ShapeDtypeStruct((b, s, d), F32),
                   jax.ShapeDtypeStruct((n * TOKEN_TILE_ROWS, V7X_LANES), F32),
                   jax.ShapeDtypeStruct((TOP_K, n), I32),
                   jax.ShapeDtypeStruct((TOP_K, n), I32),
                   jax.ShapeDtypeStruct((n, V7X_LANES), F32),
                   jax.ShapeDtypeStruct((N_EXPERTS, V7X_LANES), F32)],
        scratch_shapes=[pltpu.VMEM((N_EXPERTS, V7X_LANES), F32)],
        compiler_params=_params("arbitrary", "arbitrary"),
        name="attend_route",
    )(x, g, wq, kmem, vmem, wo, gm, wrt_hi, wrt_lo, br, utri)


def _route_tile(x, first, g_ref, wrt_hi_ref, wrt_lo_ref, br_ref, utri_ref,
                h_ref, eid_ref, rank_ref, gate_ref, cnt_ref, carry_ref):
    t = x.shape[0]

    @pl.when(first)
    def _():
        carry_ref[...] = jnp.zeros_like(carry_ref)

    h = _rms(x, g_ref[...])
    _store_token_tiles(h_ref, (), h)
    h_hi = h.astype(BF16)
    h_lo = (h - h_hi.astype(F32)).astype(BF16)
    logits = (_dot_nt(wrt_hi_ref[...], h_hi) + _dot_nt(wrt_hi_ref[...], h_lo)
              + _dot_nt(wrt_lo_ref[...], h_hi)) + br_ref[...]

    e_iota = lax.broadcasted_iota(I32, (N_EXPERTS, t), 0)
    e_iota_f = e_iota.astype(F32)
    work = logits
    vals, onehots = [], []
    for _ in range(TOP_K):
        m = jnp.max(work, axis=0, keepdims=True)
        idx = jnp.min(jnp.where(work == m, e_iota_f, float(N_EXPERTS)), axis=0, keepdims=True)
        hit = e_iota_f == idx
        vals.append(m)
        onehots.append(hit)
        work = jnp.where(hit, -jnp.inf, work)

    ex = [jnp.exp(vk - vals[0]) for vk in vals]
    den = ex[0] + ex[1] + ex[2] + ex[3]

    member = jnp.zeros((N_EXPERTS, t), F32)
    for hit in onehots:
        member = member + jnp.where(hit, 1.0, 0.0)
    carry = carry_ref[...]
    before = _dot(member.astype(BF16), utri_ref[...]) + carry[:, 0:1]

    row4 = lax.broadcasted_iota(I32, (TOP_K, t), 0)
    grow = lax.broadcasted_iota(I32, (V7X_LANES, t), 0)
    eid = jnp.zeros((TOP_K, t), I32)
    rank = jnp.zeros((TOP_K, t), I32)
    gates_t = jnp.zeros((V7X_LANES, t), F32)
    for kk in range(TOP_K):
        idx_k = jnp.sum(jnp.where(onehots[kk], e_iota_f, 0.0), axis=0, keepdims=True)
        rank_k = jnp.sum(jnp.where(onehots[kk], before, 0.0), axis=0, keepdims=True)
        eid = jnp.where(row4 == kk, idx_k.astype(I32), eid)
        rank = jnp.where(row4 == kk, rank_k.astype(I32), rank)
        gates_t = jnp.where(grow == kk, ex[kk] / den, gates_t)
    eid_ref[...] = eid
    rank_ref[...] = rank
    gate_ref[...] = gates_t.T

    new_carry = carry + jnp.sum(member, axis=1, keepdims=True)
    carry_ref[...] = new_carry
    cnt_ref[...] = new_carry


def _plan_kernel(start_ref, eid_ref, rank_ref, dest_ref):
    e = eid_ref[...]
    dest = rank_ref[...]
    for ee in range(N_EXPERTS):
        dest = dest + jnp.where(e == ee, start_ref[ee], 0)
    dest_ref[...] = dest


def _plan(starts, eid, rank):
    n = eid.shape[1]
    t = min(PLAN_TILE, n)
    blk = pl.BlockSpec((TOP_K, t), lambda i, s: (0, i))
    return pl.pallas_call(
        _plan_kernel,
        grid_spec=pltpu.PrefetchScalarGridSpec(
            num_scalar_prefetch=1, grid=(n // t,), in_specs=[blk, blk], out_specs=blk),
        out_shape=jax.ShapeDtypeStruct((TOP_K, n), I32),
        compiler_params=_params("arbitrary"),
        name="plan",
    )(starts, eid, rank)


def _invert_kernel(dest_ref, inv_ref):
    def body(i, carry):
        inv_ref[dest_ref[i]] = i
        return carry

    lax.fori_loop(0, dest_ref.shape[0], body, 0, unroll=128)


def _invert(dest_flat):
    return pl.pallas_call(
        _invert_kernel,
        grid_spec=pltpu.PrefetchScalarGridSpec(
            num_scalar_prefetch=1, grid=(1,), in_specs=[],
            out_specs=pl.BlockSpec(memory_space=pltpu.SMEM)),
        out_shape=jax.ShapeDtypeStruct(dest_flat.shape, I32),
        compiler_params=_params("arbitrary"),
        name="invert",
    )(dest_flat)


_ITEM_VALID, _ITEM_FIRST, _ITEM_NEW_EXPERT, _ITEM_HAS_NEXT_EXPERT = 1, 2, 4, 8


def _expert_kernel(blk_ref, lo_ref, hi_ref, flag_ref, exp_ref, next_exp_ref, inv_ref,
                   h_ref, wgu_hbm, bgu_ref, wdn_hbm, bdn_ref, out_ref,
                   wgu_bf, wdn_bf, wgu_f32, wdn_f32, xs_buf, y_buf, gsem, ssem, wsem,
                   *, n_tok, n_blocks):
    tr = TOKEN_TILE_ROWS
    bm = xs_buf.shape[1] // tr
    i = pl.program_id(0)
    flags = flag_ref[i]
    blk = blk_ref[i]
    slots = xs_buf.shape[0]
    slot = lax.rem(blk, slots)

    def token_tile(ref, row):
        return ref.at[pl.ds(pl.multiple_of(row * tr, tr), tr), :]

    def gather(b, s):
        return [pltpu.make_async_copy(
            token_tile(h_ref, inv_ref[b * bm + r] & (n_tok - 1)),
            xs_buf.at[s, pl.ds(r * tr, tr), :], gsem.at[s]) for r in range(bm)]

    def scatter(b, s):
        return [pltpu.make_async_copy(
            y_buf.at[s, pl.ds(r * tr, tr), :],
            token_tile(out_ref, inv_ref[b * bm + r]), ssem.at[s]) for r in range(bm)]

    def wait_gather(s):
        pltpu.make_async_copy(h_ref.at[pl.ds(0, bm * tr), :], xs_buf.at[s], gsem.at[s]).wait()

    def wait_scatter(s):
        pltpu.make_async_copy(y_buf.at[s], out_ref.at[pl.ds(0, bm * tr), :], ssem.at[s]).wait()

    def start_all(copies):
        for r, c in enumerate(copies):
            c.start(priority=r % 2)

    @pl.when(i == 0)
    def _():
        for b in range(slots - 1):
            start_all(gather(b, b))

    def weight_copies(e):
        return (pltpu.make_async_copy(wgu_hbm.at[e], wgu_f32, wsem.at[0]),
                pltpu.make_async_copy(wdn_hbm.at[e], wdn_f32, wsem.at[1]))

    @pl.when(i == 0)
    def _():
        for c in weight_copies(exp_ref[0]):
            c.start()

    @pl.when((flags & _ITEM_NEW_EXPERT) != 0)
    def _():
        for c in weight_copies(exp_ref[i]):
            c.wait()
        wgu_bf[...] = wgu_f32[...].astype(BF16)
        wdn_bf[...] = wdn_f32[...].astype(BF16)

        @pl.when((flags & _ITEM_HAS_NEXT_EXPERT) != 0)
        def _():
            for c in weight_copies(next_exp_ref[i]):
                c.start()

    bgu = bgu_ref[exp_ref[i]]
    bdn = bdn_ref[exp_ref[i]]

    dyn_slot = (slot,)
    half = bm // 2

    def load_x(r0, n_rows):
        return _load_token_tiles(xs_buf, dyn_slot, n_rows, r0).astype(BF16)

    def ffn_rows(x, r0, merge):
        n_rows = x.shape[0]
        gu = _dot(x, wgu_bf[...]) + bgu
        gate = jnp.minimum(gu[:, :D_FF], SWIGLU_LIMIT)
        up = jnp.clip(gu[:, D_FF:], -SWIGLU_LIMIT, SWIGLU_LIMIT)
        act = gate * jax.nn.sigmoid(SWIGLU_ALPHA * gate) * (up + 1.0)
        yb = _dot(act.astype(BF16), wdn_bf[...]) + bdn
        row = r0 + lax.broadcasted_iota(I32, yb.shape, 0)
        mine = (row >= lo_ref[i]) & (row < hi_ref[i])
        other = _load_token_tiles(y_buf, dyn_slot, n_rows, r0) if merge else 0.0
        return jnp.where(mine, yb, other)

    def first_item(s, has_prev):
        wait_gather(s)
        x_top = _load_token_tiles(xs_buf, (s,), half, 0).astype(BF16)
        ahead = slots - 1
        start_all(gather(jnp.minimum(blk + ahead, n_blocks - 1), (s + ahead) % slots))
        _store_token_tiles(y_buf, dyn_slot, ffn_rows(x_top, 0, False), 0)
        if has_prev:
            start_all(scatter(blk - 1, (s - 1) % slots))
        _store_token_tiles(y_buf, dyn_slot, ffn_rows(load_x(half, half), half, False), half)

    first = (flags & (_ITEM_VALID | _ITEM_FIRST)) == (_ITEM_VALID | _ITEM_FIRST)
    later = (flags & (_ITEM_VALID | _ITEM_FIRST)) == _ITEM_VALID

    @pl.when(first & (blk == 0))
    def _():
        first_item(0, False)

    @pl.when(first & (blk >= slots))
    def _():
        for s in range(slots):
            @pl.when(slot == s)
            def _():
                wait_scatter(s)

    for s in range(slots):
        @pl.when(first & (slot == s) & (blk > 0))
        def _():
            first_item(s, True)

    for r0 in (0, half):
        @pl.when(later & (lo_ref[i] < r0 + half) & (hi_ref[i] > r0))
        def _():
            _store_token_tiles(y_buf, dyn_slot, ffn_rows(load_x(r0, half), r0, True), r0)

    @pl.when(i == pl.num_programs(0) - 1)
    def _():
        start_all(scatter(n_blocks - 1, (n_blocks - 1) % slots))
        for s in range(slots):
            wait_scatter(s)
        for ahead in range(1, slots):
            wait_gather((n_blocks - 1 + ahead) % slots)


def _expert_items(counts, n_rows):
    bm = EXPERT_ROWS
    n_items = n_rows // bm + N_EXPERTS - 1
    ends = jnp.cumsum(counts)
    starts = ends - counts
    blocks_of = jnp.where(counts > 0, (ends - 1) // bm - starts // bm + 1, 0)
    item_end = jnp.cumsum(blocks_of)
    item_start = item_end - blocks_of
    total = item_end[-1]
    idx = jnp.arange(n_items, dtype=I32)
    valid = idx < total
    idc = jnp.minimum(idx, jnp.maximum(total - 1, 0))
    owner = ((item_start[None, :] <= idc[:, None]) & (idc[:, None] < item_end[None, :])).astype(I32)
    pick = lambda v: jnp.sum(owner * v[None, :], axis=1)
    e = pick(jnp.arange(N_EXPERTS, dtype=I32))
    blk = pick(starts // bm) + idc - pick(item_start)
    lo = jnp.clip(pick(starts) - blk * bm, 0, bm)
    hi = jnp.clip(pick(ends) - blk * bm, 0, bm)
    prev_blk = jnp.concatenate([jnp.full((1,), -1, I32), blk[:-1]])
    prev_e = jnp.concatenate([jnp.full((1,), -1, I32), e[:-1]])
    ids = jnp.arange(N_EXPERTS, dtype=I32)
    owners = jnp.where(counts > 0, ids, N_EXPERTS)
    next_of = jnp.min(jnp.where(ids[None, :] > ids[:, None], owners[None, :], N_EXPERTS), axis=1)
    next_e = pick(next_of)
    new_expert = e != prev_e
    flags = jnp.where(valid,
                      _ITEM_VALID
                      + jnp.where(blk != prev_blk, _ITEM_FIRST, 0)
                      + jnp.where(new_expert, _ITEM_NEW_EXPERT, 0)
                      + jnp.where(new_expert & (next_e < N_EXPERTS), _ITEM_HAS_NEXT_EXPERT, 0), 0)
    as_i32 = lambda v: v.astype(I32)
    return (as_i32(blk), as_i32(lo), as_i32(hi), as_i32(flags), as_i32(e),
            as_i32(jnp.minimum(next_e, N_EXPERTS - 1)))


def _experts(items, inv, h, wgu, bgu, wdn, bdn):
    tr, d = TOKEN_TILE_ROWS, D_MODEL
    n = h.shape[0] // tr
    bm = EXPERT_ROWS
    n_rows = inv.shape[0]
    n_blocks = n_rows // bm
    assert n_rows % bm == 0 and n_blocks >= EXPERT_SLOTS and n & (n - 1) == 0
    blk, lo, hi, flags, e, next_e = items
    whole = lambda i, *prefetch: (0, 0, 0)
    return pl.pallas_call(
        functools.partial(_expert_kernel, n_tok=n, n_blocks=n_blocks),
        grid_spec=pltpu.PrefetchScalarGridSpec(
            num_scalar_prefetch=7, grid=(blk.shape[0],),
            in_specs=[pl.BlockSpec(memory_space=pl.ANY),
                      pl.BlockSpec(memory_space=pl.ANY),
                      pl.BlockSpec((N_EXPERTS, 1, 2 * D_FF), whole),
                      pl.BlockSpec(memory_space=pl.ANY),
                      pl.BlockSpec((N_EXPERTS, 1, d), whole)],
            out_specs=pl.BlockSpec(memory_space=pl.ANY),
            scratch_shapes=[pltpu.VMEM((d, 2 * D_FF), BF16),
                            pltpu.VMEM((D_FF, d), BF16),
                            pltpu.VMEM((d, 2 * D_FF), F32),
                            pltpu.VMEM((D_FF, d), F32),
                            pltpu.VMEM((EXPERT_SLOTS, bm * tr, V7X_LANES), F32),
                            pltpu.VMEM((EXPERT_SLOTS, bm * tr, V7X_LANES), F32),
                            pltpu.SemaphoreType.DMA((EXPERT_SLOTS,)),
                            pltpu.SemaphoreType.DMA((EXPERT_SLOTS,)),
                            pltpu.SemaphoreType.DMA((2,))]),
        out_shape=jax.ShapeDtypeStruct((n_rows * tr, V7X_LANES), F32),
        compiler_params=_params("arbitrary"),
        name="experts",
    )(blk, lo, hi, flags, e, next_e, inv, h, wgu, bgu, wdn, bdn)


def _combine_kernel(x_ref, gate_ref, g_ref, y_ref, o_ref):
    gates = gate_ref[...]
    moe = jnp.zeros(x_ref.shape, F32)
    for kk in range(TOP_K):
        moe = moe + _load_token_tiles(y_ref, (kk,), x_ref.shape[0]) * gates[:, kk:kk + 1]
    o_ref[...] = _rms(x_ref[...] + moe, g_ref[...])


def _combine(x2d, gates, g, y):
    n, d = x2d.shape
    t = min(COMBINE_TILE, n)
    return pl.pallas_call(
        _combine_kernel,
        grid=(n // t,),
        in_specs=[pl.BlockSpec((t, d), lambda i: (i, 0)),
                  pl.BlockSpec((t, V7X_LANES), lambda i: (i, 0)),
                  pl.BlockSpec((1, d), lambda i: (0, 0)),
                  pl.BlockSpec((TOP_K, t * TOKEN_TILE_ROWS, V7X_LANES), lambda i: (0, i, 0))],
        out_specs=pl.BlockSpec((t, d), lambda i: (i, 0)),
        out_shape=jax.ShapeDtypeStruct((n, d), F32),
        compiler_params=_params("arbitrary"),
        name="combine",
    )(x2d, gates, g, y.reshape(TOP_K, n * TOKEN_TILE_ROWS, V7X_LANES))


def _layer(x, mem, norm_mix_g, w_in, w_pool, pool_scale, w_gate_up, b_gate_up, gla_norm_g, w_out,
           norm_xattn_g, norm_mem_g, w_xq, w_xk, w_xv, w_xo,
           norm_moe_g, w_router, b_router, w_gu, b_gu, w_dn, b_dn, out_g):
    b, s, d = x.shape
    n = b * s
    row = lambda a: a.reshape(1, -1)

    pad = V7X_LANES - GATE_RANK
    wa = w_in[:, :_G0].astype(BF16)
    wg = jnp.pad(w_in[:, _G0:_R0], ((0, 0), (0, pad))).astype(BF16)
    wr = w_in[:, _R0:].astype(BF16)
    wgate = jnp.pad(w_gate_up, ((0, pad), (0, 0))).astype(BF16)
    wrt = w_router.T
    wrt_hi = wrt.astype(BF16)
    wrt_lo = (wrt - wrt_hi.astype(F32)).astype(BF16)

    kmem, vmem = _memkv(mem, row(norm_mem_g), w_xk.astype(BF16), w_xv.astype(BF16))
    x1 = _mix(x, row(norm_mix_g), wa, wg, wr, w_pool.astype(BF16), row(pool_scale), wgate,
              row(b_gate_up), row(gla_norm_g), w_out.astype(BF16))
    x2, h, eid, rank, gates, cnt = _attend_route(
        x1, row(norm_xattn_g), w_xq.astype(BF16), kmem, vmem, w_xo.astype(BF16),
        row(norm_moe_g), wrt_hi, wrt_lo, b_router.reshape(-1, 1))
    x2 = x2.reshape(n, d)
    counts = cnt[:, 0].astype(I32)
    starts = (jnp.cumsum(counts) - counts).astype(I32)
    dest = _plan(starts, eid, rank).reshape(TOP_K * n)
    items = _expert_items(counts, TOP_K * n)
    y = _experts(items, _invert(dest), h, w_gu, b_gu.reshape(N_EXPERTS, 1, -1),
                 w_dn, b_dn.reshape(N_EXPERTS, 1, -1))
    out = _combine(x2, gates, row(out_g), y)
    return out.reshape(b, s, d)


def kernel(x, mem, norm_mix_g, w_in, w_pool, pool_scale, w_gate_up, b_gate_up, gla_norm_g, w_out,
           norm_xattn_g, norm_mem_g, w_xq, w_xk, w_xv, w_xo, norm_moe_g, w_router, b_router,
           w_gu, b_gu, w_dn, b_dn, norm_final_g):
    depth = norm_mix_g.shape[0]
    assert depth == 1, "the final rmsnorm is fused into the (single) layer's combine stage"
    return _layer(x, mem, norm_mix_g[0], w_in[0], w_pool[0], pool_scale[0], w_gate_up[0],
                  b_gate_up[0], gla_norm_g[0], w_out[0], norm_xattn_g[0], norm_mem_g[0],
                  w_xq[0], w_xk[0], w_xv[0], w_xo[0], norm_moe_g[0], w_router[0], b_router[0],
                  w_gu[0], b_gu[0], w_dn[0], b_dn[0], norm_final_g)
```
